```python
import jax, jax.numpy as jnp
from jax import lax
import numpy as np

D_MODEL = 2048
BATCH = 4
SEQ = 4096
DEPTH = 1

MLA_HEADS = 8
QK_NOPE_DIM = 128
QK_ROPE_DIM = 64
V_HEAD_DIM = 128
Q_LORA_RANK = 512
KV_LORA_RANK = 256
ROPE_THETA = 10000.0
Q_BLOCK = 128
DIL_PATTERNS = ((128, 1), (512, 4), (2048, 16))
DIL_GROUPS = 3
DIL_HEADS_PER_GROUP = 4
DIL_HEADS = DIL_GROUPS * DIL_HEADS_PER_GROUP
DIL_HEAD_DIM = 128
DIL_BLOCK = 128
ALIBI_MAX_BIAS = 8.0
D_FF = 5504
CONV_WIDTH = 3
NORM_EPS = 1e-6

MLA_Q_DIM = MLA_HEADS * (QK_NOPE_DIM + QK_ROPE_DIM)
MLA_KV_DIM = MLA_HEADS * (QK_NOPE_DIM + V_HEAD_DIM)
DIL_QKV_DIM = DIL_HEADS * DIL_HEAD_DIM
DIL_OUT_DIM = DIL_HEADS_PER_GROUP * DIL_HEAD_DIM
IN_SPLITS = (Q_LORA_RANK, KV_LORA_RANK, QK_ROPE_DIM, DIL_QKV_DIM, DIL_QKV_DIM, DIL_QKV_DIM, D_MODEL, D_MODEL)
D_IN = Q_LORA_RANK + KV_LORA_RANK + QK_ROPE_DIM + 3 * DIL_QKV_DIM + 2 * D_MODEL

kernel_name = 'hybrid_mla_dilated_convffn'


def rmsnorm(x, g):
    xf = x.astype(jnp.float32)
    y = xf * lax.rsqrt(jnp.mean(xf * xf, axis=-1, keepdims=True) + NORM_EPS)
    return (y * g.astype(jnp.float32)).astype(x.dtype)


def rope(x, cos, sin):
    half = x.shape[-1] // 2
    xf = x.astype(jnp.float32)
    x1, x2 = xf[..., :half], xf[..., half:]
    return jnp.concatenate([x1 * cos - x2 * sin, x2 * cos + x1 * sin], axis=-1).astype(x.dtype)


def mla_attention(c_q, c_kv, k_pe_raw, q_norm_g, w_uq, kv_norm_g, w_ukv):
    B, S, _ = c_q.shape
    q = (rmsnorm(c_q, q_norm_g) @ w_uq).reshape(B, S, MLA_HEADS, QK_NOPE_DIM + QK_ROPE_DIM)
    kv = (rmsnorm(c_kv, kv_norm_g) @ w_ukv).reshape(B, S, MLA_HEADS, QK_NOPE_DIM + V_HEAD_DIM)
    q_nope, q_pe = q[..., :QK_NOPE_DIM], q[..., QK_NOPE_DIM:]
    k_nope, v = kv[..., :QK_NOPE_DIM], kv[..., QK_NOPE_DIM:]
    pos = jnp.arange(S, dtype=jnp.float32)
    inv_freq = ROPE_THETA ** (-jnp.arange(0, QK_ROPE_DIM, 2, dtype=jnp.float32) / QK_ROPE_DIM)
    ang = pos[:, None] * inv_freq[None, :]
    cos, sin = jnp.cos(ang), jnp.sin(ang)
    q_pe = rope(q_pe, cos[:, None, :], sin[:, None, :])
    k_pe = rope(k_pe_raw, cos, sin)
    scale = (QK_NOPE_DIM + QK_ROPE_DIM) ** -0.5
    nb = S // Q_BLOCK
    qn_b = q_nope.reshape(B, nb, Q_BLOCK, MLA_HEADS, QK_NOPE_DIM).transpose(1, 0, 2, 3, 4)
    qp_b = q_pe.reshape(B, nb, Q_BLOCK, MLA_HEADS, QK_ROPE_DIM).transpose(1, 0, 2, 3, 4)
    kpos = jnp.arange(S)

    def one_block(args):
        qn, qp, i = args
        s = (jnp.einsum('bqhd,bkhd->bhqk', qn, k_nope).astype(jnp.float32)
             + jnp.einsum('bqhr,bkr->bhqk', qp, k_pe).astype(jnp.float32)) * scale
        qpos = i * Q_BLOCK + jnp.arange(Q_BLOCK)
        s = jnp.where(kpos[None, :] <= qpos[:, None], s, -jnp.inf)
        p = jax.nn.softmax(s, axis=-1).astype(v.dtype)
        return jnp.einsum('bhqk,bkhd->bqhd', p, v)

    o = lax.map(one_block, (qn_b, qp_b, jnp.arange(nb)))
    return o.transpose(1, 0, 2, 3, 4).reshape(B, S, MLA_HEADS * V_HEAD_DIM)


def dilated_group(q, k, v, window, dil, slopes):
    B, S, H, D = q.shape
    w_sub = window // dil
    L = S // dil
    nb = -(-L // DIL_BLOCK)
    Lp = nb * DIL_BLOCK

    def to_blocks(t):
        t = t.reshape(B, L, dil, H, D).transpose(0, 2, 1, 3, 4)
        t = jnp.pad(t, ((0, 0), (0, 0), (0, Lp - L), (0, 0), (0, 0)))
        return t.reshape(B, dil, nb, DIL_BLOCK, H, D)

    def with_prev(t):
        prev = jnp.pad(t, ((0, 0), (0, 0), (1, 0), (0, 0), (0, 0), (0, 0)))[:, :, :-1]
        return jnp.concatenate([prev, t], axis=3)

    qb = to_blocks(q)
    kk = with_prev(to_blocks(k))
    vv = with_prev(to_blocks(v))
    s = jnp.einsum('brnqhd,brnkhd->brnhqk', qb, kk).astype(jnp.float32) * (D ** -0.5)
    p_idx = jnp.arange(DIL_BLOCK)
    k_idx = jnp.arange(2 * DIL_BLOCK)
    j = p_idx[:, None] + DIL_BLOCK - k_idx[None, :]
    valid = (j >= 0) & (j <= w_sub)
    first = jnp.arange(nb) == 0
    valid = valid[None] & ~(first[:, None, None] & (k_idx < DIL_BLOCK)[None, None, :])
    alibi = -slopes.astype(jnp.float32)[:, None, None] * (dil * j).astype(jnp.float32)[None]
    s = jnp.where(valid[None, None, :, None], s + alibi[None, None, None], -jnp.inf)
    lse = jax.nn.logsumexp(s, axis=-1)
    p = jnp.exp(s - lse[..., None]).astype(v.dtype)
    o = jnp.einsum('brnhqk,brnkhd->brnqhd', p, vv)

    def from_blocks(t):
        t = t.reshape((B, dil, Lp) + t.shape[4:])[:, :, :L]
        t = jnp.moveaxis(t, 1, 2)
        return t.reshape((B, S) + t.shape[3:])

    return from_blocks(o), from_blocks(lse.transpose(0, 1, 2, 4, 3))


def dilated_attention(dq, dk, dv):
    B, S, _ = dq.shape
    shp = (B, S, DIL_GROUPS, DIL_HEADS_PER_GROUP, DIL_HEAD_DIM)
    q, k, v = dq.reshape(shp), dk.reshape(shp), dv.reshape(shp)
    slopes = 2.0 ** (-ALIBI_MAX_BIAS * jnp.arange(1, DIL_HEADS + 1, dtype=jnp.float32) / DIL_HEADS)
    slopes = slopes.reshape(DIL_GROUPS, DIL_HEADS_PER_GROUP)
    outs, lses = [], []
    for g, (window, dil) in enumerate(DIL_PATTERNS):
        o_g, l_g = dilated_group(q[:, :, g], k[:, :, g], v[:, :, g], window, dil, slopes[g])
        outs.append(o_g)
        lses.append(l_g)
    o = jnp.stack(outs, axis=0)
    wts = jax.nn.softmax(jnp.stack(lses, axis=0), axis=0)
    out = jnp.sum(wts[..., None] * o.astype(jnp.float32), axis=0).astype(dq.dtype)
    return out.reshape(B, S, DIL_OUT_DIM)


def causal_dwconv(u, w, b):
    S = u.shape[1]
    upad = jnp.pad(u, ((0, 0), (CONV_WIDTH - 1, 0), (0, 0)))
    out = b
    for t in range(CONV_WIDTH):
        out = out + w[t] * upad[:, t:t + S]
    return out


def setup_inputs(seed: int = 0) -> dict:
    key = jax.random.key(seed)
    ks = jax.random.split(key, 17)

    def w(k, shape, fan_in):
        return jax.random.normal(k, shape, jnp.float32) * (fan_in ** -0.5)

    def gain(k, shape):
        return 1.0 + 0.02 * jax.random.normal(k, shape, jnp.float32)

    return {
        'x': jax.random.normal(ks[0], (BATCH, SEQ, D_MODEL), jnp.float32),
        'attn_norm_g': gain(ks[1], (DEPTH, D_MODEL)),
        'w_in': w(ks[2], (DEPTH, D_MODEL, D_IN), D_MODEL),
        'b_gate': 0.02 * jax.random.normal(ks[3], (DEPTH, 2 * D_MODEL), jnp.float32),
        'q_norm_g': gain(ks[4], (DEPTH, Q_LORA_RANK)),
        'w_uq': w(ks[5], (DEPTH, Q_LORA_RANK, MLA_Q_DIM), Q_LORA_RANK),
        'kv_norm_g': gain(ks[6], (DEPTH, KV_LORA_RANK)),
        'w_ukv': w(ks[7], (DEPTH, KV_LORA_RANK, MLA_KV_DIM), KV_LORA_RANK),
        'w_o_mla': w(ks[8], (DEPTH, MLA_HEADS * V_HEAD_DIM, D_MODEL), MLA_HEADS * V_HEAD_DIM),
        'w_o_dil': w(ks[9], (DEPTH, DIL_OUT_DIM, D_MODEL), DIL_OUT_DIM),
        'w_out': w(ks[10], (DEPTH, D_MODEL, D_MODEL), D_MODEL),
        'ffn_norm_g': gain(ks[11], (DEPTH, D_MODEL)),
        'w_up': w(ks[12], (DEPTH, D_MODEL, 2 * D_FF), D_MODEL),
        'conv_w': w(ks[13], (DEPTH, CONV_WIDTH, 2 * D_FF), CONV_WIDTH),
        'conv_b': 0.02 * jax.random.normal(ks[14], (DEPTH, 2 * D_FF), jnp.float32),
        'w_down': w(ks[15], (DEPTH, D_FF, D_MODEL), D_FF),
        'final_norm_g': gain(ks[16], (D_MODEL,)),
    }


def reference(x, attn_norm_g, w_in, b_gate, q_norm_g, w_uq, kv_norm_g, w_ukv, w_o_mla, w_o_dil,
              w_out, ffn_norm_g, w_up, conv_w, conv_b, w_down, final_norm_g):
    split_at = [int(c) for c in np.cumsum(IN_SPLITS)[:-1]]
    for l in range(DEPTH):
        h = rmsnorm(x, attn_norm_g[l])
        proj = h @ w_in[l]
        c_q, c_kv, k_pe, dq, dk, dv, ga, gb = jnp.split(proj, split_at, axis=-1)
        gate_a = jax.nn.sigmoid(ga + b_gate[l, :D_MODEL])
        gate_b = jax.nn.sigmoid(gb + b_gate[l, D_MODEL:])
        o_a = mla_attention(c_q, c_kv, k_pe, q_norm_g[l], w_uq[l], kv_norm_g[l], w_ukv[l]) @ w_o_mla[l]
        o_b = dilated_attention(dq, dk, dv) @ w_o_dil[l]
        x = x + (gate_a * o_a + gate_b * o_b) @ w_out[l]
        h2 = rmsnorm(x, ffn_norm_g[l])
        u = causal_dwconv(h2 @ w_up[l], conv_w[l], conv_b[l])
        up, gate = u[..., :D_FF], u[..., D_FF:]
        x = x + (jax.nn.silu(gate) * up) @ w_down[l]
    return rmsnorm(x, final_norm_g)
```

```python
import functools

import numpy as np
import jax
import jax.numpy as jnp
from jax import lax
from jax.experimental import pallas as pl
from jax.experimental.pallas import tpu as pltpu

D_MODEL = 2048
MLA_HEADS = 8
QK_NOPE_DIM = 128
QK_ROPE_DIM = 64
V_HEAD_DIM = 128
Q_LORA_RANK = 512
KV_LORA_RANK = 256
ROPE_THETA = 10000.0
DIL_PATTERNS = ((128, 1), (512, 4), (2048, 16))
DIL_GROUPS = 3
DIL_HEADS_PER_GROUP = 4
DIL_HEADS = DIL_GROUPS * DIL_HEADS_PER_GROUP
DIL_HEAD_DIM = 128
DIL_BLOCK = 128
ALIBI_MAX_BIAS = 8.0
D_FF = 5504
CONV_WIDTH = 3
NORM_EPS = 1e-6

LANE = 128
SUBLANE = 8
VMEM_LIMIT = 56 * 1024 * 1024

LAT_DIM = Q_LORA_RANK + KV_LORA_RANK + 2 * QK_ROPE_DIM
QK_SLOT = 2 * LANE
DIL_QKV_DIM = DIL_HEADS * DIL_HEAD_DIM
FF_CHUNK = 512
D_FF_PAD = -(-D_FF // FF_CHUNK) * FF_CHUNK
FF_CHUNKS = D_FF_PAD // FF_CHUNK
NEG = -1e30

TM = 512
TQ = 512
TK = 512

BF = jnp.bfloat16
F32 = jnp.float32


def _params(*sem):
    return pltpu.CompilerParams(dimension_semantics=sem, vmem_limit_bytes=VMEM_LIMIT)


def _rms(x, g):
    return x * lax.rsqrt(jnp.mean(x * x, axis=-1, keepdims=True) + NORM_EPS) * g


def _dot(a, b):
    return jnp.dot(a, b, preferred_element_type=F32)


def _dot_nt(a, b):
    return lax.dot_general(a, b, (((1,), (1,)), ((), ())), preferred_element_type=F32)


def _rope_pair(t):
    lane = lax.broadcasted_iota(jnp.int32, t.shape, 1)
    return jnp.where(lane < QK_ROPE_DIM, t + pltpu.roll(t, QK_ROPE_DIM, 1), 0.0)


def _mla_prep_kernel(x_ref, g_ref, wlat_ref, qg_ref, kvg_ref, wq_ref, wkv_ref, cs_ref,
                     q_ref, k_ref, v_ref):
    h = _rms(x_ref[...], g_ref[...]).astype(BF)
    lat = _dot(h, wlat_ref[...])
    cq = _rms(lat[:, :Q_LORA_RANK], qg_ref[...]).astype(BF)
    ckv = _rms(lat[:, Q_LORA_RANK:Q_LORA_RANK + KV_LORA_RANK], kvg_ref[...]).astype(BF)
    q = _dot(cq, wq_ref[...])
    kv = _dot(ckv, wkv_ref[...])
    cs = cs_ref[...]
    scale = (QK_NOPE_DIM + QK_ROPE_DIM) ** -0.5
    k_rope = _rope_pair(lat[:, Q_LORA_RANK + KV_LORA_RANK:] * cs).astype(BF)
    for hd in range(MLA_HEADS):
        qs = q[:, hd * QK_SLOT:(hd + 1) * QK_SLOT]
        q_ref[hd, :, :LANE] = (qs[:, :LANE] * scale).astype(BF)
        q_ref[hd, :, LANE:] = _rope_pair(qs[:, LANE:] * (cs * scale)).astype(BF)
        k_ref[hd, :, :LANE] = kv[:, hd * 256:hd * 256 + LANE].astype(BF)
        k_ref[hd, :, LANE:] = k_rope
        v_ref[hd] = kv[:, hd * 256 + LANE:(hd + 1) * 256].astype(BF)


def _mla_prep(x2, g, wlat, qg, kvg, wq, wkv, cs, seq):
    t = x2.shape[0]
    nseq = seq // TM
    const = lambda i: (0, 0)
    return pl.pallas_call(
        _mla_prep_kernel,
        grid=(t // TM,),
        in_specs=[
            pl.BlockSpec((TM, D_MODEL), lambda i: (i, 0)),
            pl.BlockSpec((1, D_MODEL), const),
            pl.BlockSpec((D_MODEL, LAT_DIM), const),
            pl.BlockSpec((1, Q_LORA_RANK), const),
            pl.BlockSpec((1, KV_LORA_RANK), const),
            pl.BlockSpec((Q_LORA_RANK, MLA_HEADS * QK_SLOT), const),
            pl.BlockSpec((KV_LORA_RANK, MLA_HEADS * 256), const),
            pl.BlockSpec((TM, LANE), lambda i: (i % nseq, 0)),
        ],
        out_specs=[
            pl.BlockSpec((MLA_HEADS, TM, QK_SLOT), lambda i: (0, i, 0)),
            pl.BlockSpec((MLA_HEADS, TM, QK_SLOT), lambda i: (0, i, 0)),
            pl.BlockSpec((MLA_HEADS, TM, V_HEAD_DIM), lambda i: (0, i, 0)),
        ],
        out_shape=[
            jax.ShapeDtypeStruct((MLA_HEADS, t, QK_SLOT), BF),
            jax.ShapeDtypeStruct((MLA_HEADS, t, QK_SLOT), BF),
            jax.ShapeDtypeStruct((MLA_HEADS, t, V_HEAD_DIM), BF),
        ],
        compiler_params=_params("arbitrary"),
        name="mla_prep",
    )(x2, g, wlat, qg, kvg, wq, wkv, cs)


DIL_TN = 512
DIL_HEADS_PER_TILE = DIL_TN // DIL_HEAD_DIM


def _dil_proj_kernel(x_ref, g_ref, w_ref, o_ref, h_ref):
    j = pl.program_id(1)

    @pl.when(j == 0)
    def _():
        h_ref[...] = _rms(x_ref[...], g_ref[...]).astype(BF)

    res = _dot(h_ref[...], w_ref[...])
    res = res * jnp.where(j < DIL_QKV_DIM // DIL_TN, DIL_HEAD_DIM ** -0.5, 1.0)
    for hh in range(DIL_HEADS_PER_TILE):
        o_ref[hh] = res[:, hh * DIL_HEAD_DIM:(hh + 1) * DIL_HEAD_DIM].astype(BF)


def _dil_proj(x2, g, w):
    t = x2.shape[0]
    n = w.shape[1]
    return pl.pallas_call(
        _dil_proj_kernel,
        grid=(t // TM, n // DIL_TN),
        in_specs=[
            pl.BlockSpec((TM, D_MODEL), lambda i, j: (i, 0)),
            pl.BlockSpec((1, D_MODEL), lambda i, j: (0, 0)),
            pl.BlockSpec((D_MODEL, DIL_TN), lambda i, j: (0, j)),
        ],
        out_specs=pl.BlockSpec((DIL_HEADS_PER_TILE, TM, DIL_HEAD_DIM), lambda i, j: (j, i, 0)),
        out_shape=jax.ShapeDtypeStruct((n // DIL_HEAD_DIM, t, DIL_HEAD_DIM), BF),
        scratch_shapes=[pltpu.VMEM((TM, D_MODEL), BF)],
        compiler_params=_params("arbitrary", "arbitrary"),
        name="dil_proj",
    )(x2, g, w)


def _flash_kernel(q_ref, k_ref, v_ref, o_ref, m_ref, l_ref, acc_ref):
    qi = pl.program_id(2)
    q = q_ref[0]
    m_ref[...] = jnp.full(m_ref.shape, NEG, F32)
    l_ref[...] = jnp.zeros(l_ref.shape, F32)
    acc_ref[...] = jnp.zeros(acc_ref.shape, F32)

    def update(s, v):
        m_old = m_ref[...]
        m_new = jnp.maximum(m_old, jnp.max(s, axis=-1, keepdims=True))
        alpha = jnp.exp(m_old - m_new)
        p = jnp.exp(s - m_new)
        l_ref[...] = alpha * l_ref[...] + jnp.sum(p, axis=-1, keepdims=True)
        acc_ref[...] = alpha * acc_ref[...] + _dot(p.astype(BF), v)
        m_ref[...] = m_new

    def body(j, carry):
        rows = pl.ds(pl.multiple_of(j * TK, TK), TK)
        update(_dot_nt(q, k_ref[0, rows, :]), v_ref[0, rows, :])
        return carry

    lax.fori_loop(0, qi * (TQ // TK), body, 0)

    for dj in range(TQ // TK):
        rows = pl.ds(pl.multiple_of(qi * TQ + dj * TK, TK), TK)
        s = _dot_nt(q, k_ref[0, rows, :])
        qpos = lax.broadcasted_iota(jnp.int32, s.shape, 0)
        kpos = lax.broadcasted_iota(jnp.int32, s.shape, 1) + dj * TK
        update(jnp.where(kpos <= qpos, s, NEG), v_ref[0, rows, :])

    o_ref[0] = (acc_ref[...] * (1.0 / l_ref[...])).astype(BF)


def _mla_flash(q, k, v, batch, seq):
    nq = seq // TQ
    t = batch * seq
    return pl.pallas_call(
        _flash_kernel,
        grid=(batch, MLA_HEADS, nq),
        in_specs=[
            pl.BlockSpec((1, TQ, QK_SLOT), lambda b, h, i: (h, b * nq + i, 0)),
            pl.BlockSpec((1, seq, QK_SLOT), lambda b, h, i: (h, b, 0)),
            pl.BlockSpec((1, seq, V_HEAD_DIM), lambda b, h, i: (h, b, 0)),
        ],
        out_specs=pl.BlockSpec((1, TQ, V_HEAD_DIM), lambda b, h, i: (h, b * nq + i, 0)),
        out_shape=jax.ShapeDtypeStruct((MLA_HEADS, t, V_HEAD_DIM), BF),
        scratch_shapes=[pltpu.VMEM((TQ, 1), F32), pltpu.VMEM((TQ, 1), F32),
                        pltpu.VMEM((TQ, V_HEAD_DIM), F32)],
        compiler_params=_params("arbitrary", "arbitrary", "arbitrary"),
        name="mla_flash",
    )(q, k, v)


def _dilated_kernel(slope_ref, q_ref, k_ref, v_ref, o_ref, lse_ref, *, group, dil, nb):
    hs = pl.program_id(1)
    slope = slope_ref[group * DIL_HEADS_PER_GROUP + hs] * float(dil)
    qi = lax.broadcasted_iota(jnp.int32, (DIL_BLOCK, DIL_BLOCK), 0)
    ki = lax.broadcasted_iota(jnp.int32, (DIL_BLOCK, DIL_BLOCK), 1)
    dist_cur = (qi - ki).astype(F32)
    dist_prev = dist_cur + float(DIL_BLOCK)
    w_sub = float(DIL_PATTERNS[group][0] // dil)
    bias_cur = jnp.where((dist_cur >= 0.0) & (dist_cur <= w_sub), -slope * dist_cur, NEG)
    bias_prev = jnp.where(dist_prev <= w_sub, -slope * dist_prev, NEG)
    eye = qi == ki

    def step(r, n):
        cols = slice(r * DIL_HEAD_DIM, (r + 1) * DIL_HEAD_DIM)
        rows = pl.ds(pl.multiple_of(n * DIL_BLOCK, DIL_BLOCK), DIL_BLOCK)
        rows_p = pl.ds(pl.multiple_of(jnp.maximum(n - 1, 0) * DIL_BLOCK, DIL_BLOCK), DIL_BLOCK)
        q = q_ref[0, 0, rows, cols]
        s_cur = _dot_nt(q, k_ref[0, 0, rows, cols]) + bias_cur
        s_prev = _dot_nt(q, k_ref[0, 0, rows_p, cols]) + jnp.where(n == 0, NEG, bias_prev)
        m = jnp.maximum(jnp.max(s_cur, axis=-1, keepdims=True), jnp.max(s_prev, axis=-1, keepdims=True))
        p_cur = jnp.exp(s_cur - m)
        p_prev = jnp.exp(s_prev - m)
        l = jnp.sum(p_cur, axis=-1, keepdims=True) + jnp.sum(p_prev, axis=-1, keepdims=True)
        o = _dot(p_cur.astype(BF), v_ref[0, 0, rows, cols]) + _dot(p_prev.astype(BF), v_ref[0, 0, rows_p, cols])
        o_ref[0, 0, rows, cols] = (o * (1.0 / l)).astype(BF)
        lse = m + jnp.log(l)
        lse_ref[0, 0, pl.ds(r * nb + n, 1), :] = jnp.sum(jnp.where(eye, lse, 0.0), axis=0, keepdims=True)

    for r in range(dil):
        if nb <= 2:
            for n in range(nb):
                step(r, n)
        else:
            def body(n, carry, r=r):
                step(r, n)
                return carry
            lax.fori_loop(0, nb, body, 0, unroll=4)


def _dilated(qkv, slopes, group, batch, seq):
    window, dil = DIL_PATTERNS[group]
    sub = seq // dil
    nb = sub // DIL_BLOCK
    hg = DIL_HEADS_PER_GROUP
    view = qkv.reshape(3 * DIL_HEADS, batch, sub, dil * DIL_HEAD_DIM)
    blk = (1, 1, sub, dil * DIL_HEAD_DIM)

    def head(which):
        return lambda b, h: (which * DIL_HEADS + group * hg + h, b, 0, 0)

    return pl.pallas_call(
        functools.partial(_dilated_kernel, group=group, dil=dil, nb=nb),
        grid=(batch, hg),
        in_specs=[
            pl.BlockSpec(memory_space=pltpu.SMEM),
            pl.BlockSpec(blk, head(0)),
            pl.BlockSpec(blk, head(1)),
            pl.BlockSpec(blk, head(2)),
        ],
        out_specs=[
            pl.BlockSpec(blk, lambda b, h: (h, b, 0, 0)),
            pl.BlockSpec((1, 1, dil * nb, DIL_BLOCK), lambda b, h: (h, b, 0, 0)),
        ],
        out_shape=[
            jax.ShapeDtypeStruct((hg, batch, sub, dil * DIL_HEAD_DIM), BF),
            jax.ShapeDtypeStruct((hg, batch, dil * nb, DIL_BLOCK), F32),
        ],
        compiler_params=_params("arbitrary", "arbitrary"),
        name=f"dilated_d{dil}",
    )(slopes, view, view, view)


MERGE_TN = 512
MERGE_TM = 256


def _merge_kernel(x_ref, g_ref, wg_ref, bg_ref, oa_ref, o1_ref, o4_ref, o16_ref, lse_ref,
                  woa_ref, wob_ref, wout_ref, y_ref, h_ref, a_ref, b_ref, mg_ref):
    x = x_ref[...]
    h_ref[...] = _rms(x, g_ref[...]).astype(BF)
    for hd in range(MLA_HEADS):
        a_ref[:, hd * V_HEAD_DIM:(hd + 1) * V_HEAD_DIM] = oa_ref[hd]
    lse = lse_ref[...]
    for hs in range(DIL_HEADS_PER_GROUP):
        ls = [lse[:, gi * DIL_HEADS_PER_GROUP + hs:gi * DIL_HEADS_PER_GROUP + hs + 1] for gi in range(DIL_GROUPS)]
        mx = jnp.maximum(jnp.maximum(ls[0], ls[1]), ls[2])
        es = [jnp.exp(v - mx) for v in ls]
        inv = 1.0 / (es[0] + es[1] + es[2])
        comb = ((es[0] * inv) * o1_ref[hs].astype(F32) + (es[1] * inv) * o4_ref[hs].astype(F32)
                + (es[2] * inv) * o16_ref[hs].astype(F32))
        b_ref[:, hs * DIL_HEAD_DIM:(hs + 1) * DIL_HEAD_DIM] = comb.astype(BF)
    for c in range(D_MODEL // MERGE_TN):
        cols = slice(c * MERGE_TN, (c + 1) * MERGE_TN)
        cols_b = slice(D_MODEL + c * MERGE_TN, D_MODEL + (c + 1) * MERGE_TN)
        gate_a = jax.nn.sigmoid(_dot(h_ref[...], wg_ref[:, cols]) + bg_ref[:, cols])
        gate_b = jax.nn.sigmoid(_dot(h_ref[...], wg_ref[:, cols_b]) + bg_ref[:, cols_b])
        o_a = _dot(a_ref[...], woa_ref[:, cols])
        o_b = _dot(b_ref[...], wob_ref[:, cols])
        mg_ref[:, cols] = (gate_a * o_a + gate_b * o_b).astype(BF)
    y_ref[...] = x + _dot(mg_ref[...], wout_ref[...])


def _merge(x2, g, wg, bg, oa, o1, o4, o16, lse, woa, wob, wout):
    t = x2.shape[0]
    const = lambda i: (0, 0)
    one = pl.Buffered(1)
    tm = MERGE_TM
    dil_blk = pl.BlockSpec((DIL_HEADS_PER_GROUP, tm, DIL_HEAD_DIM), lambda i: (0, i, 0))
    return pl.pallas_call(
        _merge_kernel,
        grid=(t // tm,),
        in_specs=[
            pl.BlockSpec((tm, D_MODEL), lambda i: (i, 0)),
            pl.BlockSpec((1, D_MODEL), const),
            pl.BlockSpec((D_MODEL, 2 * D_MODEL), const, pipeline_mode=one),
            pl.BlockSpec((1, 2 * D_MODEL), const),
            pl.BlockSpec((MLA_HEADS, tm, V_HEAD_DIM), lambda i: (0, i, 0)),
            dil_blk, dil_blk, dil_blk,
            pl.BlockSpec((tm, DIL_HEADS), lambda i: (i, 0)),
            pl.BlockSpec((MLA_HEADS * V_HEAD_DIM, D_MODEL), const, pipeline_mode=one),
            pl.BlockSpec((DIL_HEADS_PER_GROUP * DIL_HEAD_DIM, D_MODEL), const, pipeline_mode=one),
            pl.BlockSpec((D_MODEL, D_MODEL), const, pipeline_mode=one),
        ],
        out_specs=pl.BlockSpec((tm, D_MODEL), lambda i: (i, 0)),
        out_shape=jax.ShapeDtypeStruct((t, D_MODEL), F32),
        scratch_shapes=[
            pltpu.VMEM((tm, D_MODEL), BF),
            pltpu.VMEM((tm, MLA_HEADS * V_HEAD_DIM), BF),
            pltpu.VMEM((tm, DIL_HEADS_PER_GROUP * DIL_HEAD_DIM), BF),
            pltpu.VMEM((tm, D_MODEL), BF),
        ],
        compiler_params=_params("arbitrary"),
        name="merge",
    )(x2, g, wg, bg, oa, o1, o4, o16, lse, woa, wob, wout)


def _ffn_kernel(x_ref, g_ref, wu_ref, cw_ref, cb_ref, wd_ref, fg_ref, y_ref,
                h_ref, acc_ref, u_ref, carry_ref, *, tiles_per_seq):
    i = pl.program_id(0)
    c = pl.program_id(1)

    @pl.when(c == 0)
    def _():
        h_ref[...] = _rms(x_ref[...], g_ref[...]).astype(BF)
        acc_ref[...] = jnp.zeros(acc_ref.shape, F32)

    seq_start = (i % tiles_per_seq) == 0

    @pl.when(seq_start)
    def _():
        u_ref[:SUBLANE] = jnp.zeros((SUBLANE, 2 * FF_CHUNK), F32)

    @pl.when(jnp.logical_not(seq_start))
    def _():
        u_ref[:SUBLANE] = carry_ref[c]

    u_ref[SUBLANE:] = _dot(h_ref[...], wu_ref[0])
    carry_ref[c] = u_ref[TM:]
    cw = cw_ref[0]
    conv = cb_ref[0]
    for tap in range(CONV_WIDTH):
        off = SUBLANE - (CONV_WIDTH - 1) + tap
        conv = conv + cw[tap:tap + 1] * u_ref[off:off + TM]
    act = (jax.nn.silu(conv[:, FF_CHUNK:]) * conv[:, :FF_CHUNK]).astype(BF)
    acc_ref[...] += _dot(act, wd_ref[0])

    @pl.when(c == FF_CHUNKS - 1)
    def _():
        y_ref[...] = _rms(x_ref[...] + acc_ref[...], fg_ref[...])


def _ffn(x1, g, wu, cw, cb, wd, fg, seq):
    t = x1.shape[0]
    return pl.pallas_call(
        functools.partial(_ffn_kernel, tiles_per_seq=seq // TM),
        grid=(t // TM, FF_CHUNKS),
        in_specs=[
            pl.BlockSpec((TM, D_MODEL), lambda i, c: (i, 0)),
            pl.BlockSpec((1, D_MODEL), lambda i, c: (0, 0)),
            pl.BlockSpec((1, D_MODEL, 2 * FF_CHUNK), lambda i, c: (c, 0, 0)),
            pl.BlockSpec((1, SUBLANE, 2 * FF_CHUNK), lambda i, c: (c, 0, 0)),
            pl.BlockSpec((1, 1, 2 * FF_CHUNK), lambda i, c: (c, 0, 0)),
            pl.BlockSpec((1, FF_CHUNK, D_MODEL), lambda i, c: (c, 0, 0)),
            pl.BlockSpec((1, D_MODEL), lambda i, c: (0, 0)),
        ],
        out_specs=pl.BlockSpec((TM, D_MODEL), lambda i, c: (i, 0)),
        out_shape=jax.ShapeDtypeStruct((t, D_MODEL), F32),
        scratch_shapes=[
            pltpu.VMEM((TM, D_MODEL), BF),
            pltpu.VMEM((TM, D_MODEL), F32),
            pltpu.VMEM((TM + SUBLANE, 2 * FF_CHUNK), F32),
            pltpu.VMEM((FF_CHUNKS, SUBLANE, 2 * FF_CHUNK), F32),
        ],
        compiler_params=_params("arbitrary", "arbitrary"),
        name="ffn",
    )(x1, g, wu, cw, cb, wd, fg)


def _rotate_half_cols(w):
    half = w.shape[-1] // 2
    return jnp.concatenate([-w[..., half:], w[..., :half]], axis=-1)


def _chunk_ff(a):
    pad = [(0, 0)] * (a.ndim - 1) + [(0, D_FF_PAD - D_FF)]
    a = jnp.pad(a, pad).reshape(a.shape[:-1] + (FF_CHUNKS, FF_CHUNK))
    return jnp.moveaxis(a, -2, 0)


def kernel(x, attn_norm_g, w_in, b_gate, q_norm_g, w_uq, kv_norm_g, w_ukv, w_o_mla, w_o_dil,
           w_out, ffn_norm_g, w_up, conv_w, conv_b, w_down, final_norm_g):
    batch, seq, _ = x.shape
    assert w_in.shape[0] == 1, "single-layer block"
    assert seq % TM == 0 and seq % TQ == 0 and seq % (16 * DIL_BLOCK) == 0
    t = batch * seq
    x2 = x.reshape(t, D_MODEL)

    o = np.cumsum((0, Q_LORA_RANK, KV_LORA_RANK, QK_ROPE_DIM, DIL_QKV_DIM, DIL_QKV_DIM, DIL_QKV_DIM,
                   D_MODEL, D_MODEL))
    wi = w_in[0]
    w_kpe = wi[:, o[2]:o[3]]
    wlat = jnp.concatenate([wi[:, o[0]:o[2]], w_kpe, _rotate_half_cols(w_kpe)], axis=1).astype(BF)
    wdil = wi[:, o[3]:o[6]].astype(BF)
    wgate = wi[:, o[6]:o[8]].astype(BF)
    wq = w_uq[0].reshape(Q_LORA_RANK, MLA_HEADS, QK_NOPE_DIM + QK_ROPE_DIM)
    wq_pe = wq[..., QK_NOPE_DIM:]
    wq = jnp.concatenate([wq[..., :QK_NOPE_DIM], wq_pe, _rotate_half_cols(wq_pe)], axis=-1)
    wq = wq.reshape(Q_LORA_RANK, MLA_HEADS * QK_SLOT).astype(BF)
    wkv = w_ukv[0].astype(BF)
    wu = jnp.concatenate([_chunk_ff(w_up[0][:, :D_FF]), _chunk_ff(w_up[0][:, D_FF:])], axis=-1).astype(BF)
    cw = jnp.concatenate([_chunk_ff(conv_w[0][:, :D_FF]), _chunk_ff(conv_w[0][:, D_FF:])], axis=-1)
    cw = jnp.pad(cw, ((0, 0), (0, SUBLANE - CONV_WIDTH), (0, 0)))
    cb = jnp.concatenate([_chunk_ff(conv_b[0][None, :D_FF]), _chunk_ff(conv_b[0][None, D_FF:])], axis=-1)
    wd = jnp.pad(w_down[0], ((0, D_FF_PAD - D_FF), (0, 0))).reshape(FF_CHUNKS, FF_CHUNK, D_MODEL).astype(BF)

    pos = jnp.arange(seq, dtype=F32)
    inv_freq = ROPE_THETA ** (-jnp.arange(0, QK_ROPE_DIM, 2, dtype=F32) / QK_ROPE_DIM)
    ang = pos[:, None] * inv_freq[None, :]
    cs = jnp.concatenate([jnp.cos(ang), jnp.cos(ang), jnp.sin(ang), jnp.sin(ang)], axis=1)
    slopes = 2.0 ** (-ALIBI_MAX_BIAS * jnp.arange(1, DIL_HEADS + 1, dtype=F32) / DIL_HEADS)

    q, k, v = _mla_prep(x2, attn_norm_g, wlat, q_norm_g, kv_norm_g, wq, wkv, cs, seq)
    o_mla = _mla_flash(q, k, v, batch, seq)
    qkv_dil = _dil_proj(x2, attn_norm_g, wdil)
    o_dil, lse_dil = [], []
    for gi, (_, dil) in enumerate(DIL_PATTERNS):
        o_g, lse_g = _dilated(qkv_dil, slopes, gi, batch, seq)
        nb = seq // dil // DIL_BLOCK
        o_dil.append(o_g.reshape(DIL_HEADS_PER_GROUP, t, DIL_HEAD_DIM))
        lse_g = lse_g.reshape(DIL_HEADS_PER_GROUP, batch, dil, nb, DIL_BLOCK)
        lse_dil.append(lse_g.transpose(1, 3, 4, 2, 0).reshape(t, DIL_HEADS_PER_GROUP))
    lse = jnp.concatenate(lse_dil, axis=1)
    x1 = _merge(x2, attn_norm_g, wgate, b_gate, o_mla, o_dil[0], o_dil[1], o_dil[2], lse,
                w_o_mla[0].astype(BF), w_o_dil[0].astype(BF), w_out[0].astype(BF))

    y = _ffn(x1, ffn_norm_g, wu, cw, cb, wd, final_norm_g.reshape(1, D_MODEL), seq)
    return y.reshape(batch, seq, D_MODEL)
```

```python
import functools
import math

import numpy as np
import jax
import jax.numpy as jnp
from jax import lax
from jax.experimental import pallas as pl
from jax.experimental.pallas import tpu as pltpu

D_MODEL = 2048
MLA_HEADS = 8
QK_NOPE_DIM = 128
QK_ROPE_DIM = 64
V_HEAD_DIM = 128
Q_LORA_RANK = 512
KV_LORA_RANK = 256
ROPE_THETA = 10000.0
DIL_PATTERNS = ((128, 1), (512, 4), (2048, 16))
DIL_GROUPS = 3
DIL_HEADS_PER_GROUP = 4
DIL_HEADS = DIL_GROUPS * DIL_HEADS_PER_GROUP
DIL_HEAD_DIM = 128
DIL_BLOCK = 128
ALIBI_MAX_BIAS = 8.0
D_FF = 5504
CONV_WIDTH = 3
NORM_EPS = 1e-6

LANE = 128
SUBLANE = 8
VMEM_LIMIT = 56 * 1024 * 1024

LAT_DIM = Q_LORA_RANK + KV_LORA_RANK + 2 * QK_ROPE_DIM
QK_SLOT = 2 * LANE
V_SLOT = 2 * LANE
DIL_GROUP_DIM = 3 * DIL_HEADS_PER_GROUP * DIL_HEAD_DIM
FF_CHUNK = 512
D_FF_PAD = -(-D_FF // FF_CHUNK) * FF_CHUNK
FF_CHUNKS = D_FF_PAD // FF_CHUNK
NEG = -1e30

TM = 512
TQ = 512
TK = 512
FLASH_HEADS = 2

BF = jnp.bfloat16
F32 = jnp.float32


def _params(*sem):
    return pltpu.CompilerParams(dimension_semantics=sem, vmem_limit_bytes=VMEM_LIMIT)


def _rms(x, g):
    return x * lax.rsqrt(jnp.mean(x * x, axis=-1, keepdims=True) + NORM_EPS) * g


def _dot(a, b):
    return jnp.dot(a, b, preferred_element_type=F32)


def _dot_nt(a, b):
    return lax.dot_general(a, b, (((1,), (1,)), ((), ())), preferred_element_type=F32)


def _rope_pair(t):
    lane = lax.broadcasted_iota(jnp.int32, t.shape, 1)
    return jnp.where(lane < QK_ROPE_DIM, t + pltpu.roll(t, QK_ROPE_DIM, 1), 0.0)


def _mla_prep_kernel(x_ref, g_ref, wlat_ref, qg_ref, kvg_ref, wq_ref, wkv_ref, cs_ref,
                     q_ref, k_ref, v_ref):
    h = _rms(x_ref[...], g_ref[...]).astype(BF)
    lat = _dot(h, wlat_ref[...])
    cq = _rms(lat[:, :Q_LORA_RANK], qg_ref[...]).astype(BF)
    ckv = _rms(lat[:, Q_LORA_RANK:Q_LORA_RANK + KV_LORA_RANK], kvg_ref[...]).astype(BF)
    q = _dot(cq, wq_ref[...])
    kv = _dot(ckv, wkv_ref[...])
    cs = cs_ref[...]
    scale = (QK_NOPE_DIM + QK_ROPE_DIM) ** -0.5 * math.log2(math.e)
    k_rope = _rope_pair(lat[:, Q_LORA_RANK + KV_LORA_RANK:] * cs).astype(BF)
    ones = jnp.ones((x_ref.shape[0], V_SLOT - V_HEAD_DIM), BF)
    for hd in range(MLA_HEADS):
        qs = q[:, hd * QK_SLOT:(hd + 1) * QK_SLOT]
        q_ref[hd, :, :LANE] = (qs[:, :LANE] * scale).astype(BF)
        q_ref[hd, :, LANE:] = _rope_pair(qs[:, LANE:] * (cs * scale)).astype(BF)
        k_ref[hd, :, :LANE] = kv[:, hd * 256:hd * 256 + LANE].astype(BF)
        k_ref[hd, :, LANE:] = k_rope
        v_ref[hd, :, :V_HEAD_DIM] = kv[:, hd * 256 + LANE:(hd + 1) * 256].astype(BF)
        v_ref[hd, :, V_HEAD_DIM:] = ones


def _mla_prep(x2, g, wlat, qg, kvg, wq, wkv, cs, seq):
    t = x2.shape[0]
    nseq = seq // TM
    const = lambda i: (0, 0)
    return pl.pallas_call(
        _mla_prep_kernel,
        grid=(t // TM,),
        in_specs=[
            pl.BlockSpec((TM, D_MODEL), lambda i: (i, 0)),
            pl.BlockSpec((1, D_MODEL), const),
            pl.BlockSpec((D_MODEL, LAT_DIM), const),
            pl.BlockSpec((1, Q_LORA_RANK), const),
            pl.BlockSpec((1, KV_LORA_RANK), const),
            pl.BlockSpec((Q_LORA_RANK, MLA_HEADS * QK_SLOT), const),
            pl.BlockSpec((KV_LORA_RANK, MLA_HEADS * 256), const),
            pl.BlockSpec((TM, LANE), lambda i: (i % nseq, 0)),
        ],
        out_specs=[
            pl.BlockSpec((MLA_HEADS, TM, QK_SLOT), lambda i: (0, i, 0)),
            pl.BlockSpec((MLA_HEADS, TM, QK_SLOT), lambda i: (0, i, 0)),
            pl.BlockSpec((MLA_HEADS, TM, V_SLOT), lambda i: (0, i, 0)),
        ],
        out_shape=[
            jax.ShapeDtypeStruct((MLA_HEADS, t, QK_SLOT), BF),
            jax.ShapeDtypeStruct((MLA_HEADS, t, QK_SLOT), BF),
            jax.ShapeDtypeStruct((MLA_HEADS, t, V_SLOT), BF),
        ],
        compiler_params=_params("arbitrary"),
        name="mla_prep",
    )(x2, g, wlat, qg, kvg, wq, wkv, cs)


def _dil_proj_kernel(x_ref, g_ref, w_ref, *rest):
    o_refs, h_ref, res_ref = rest[:DIL_GROUPS], rest[DIL_GROUPS], rest[DIL_GROUPS + 1]
    nqkv = 3 * DIL_HEADS_PER_GROUP
    h_ref[...] = _rms(x_ref[...], g_ref[...]).astype(BF)
    for gi, (_, dil) in enumerate(DIL_PATTERNS):
        sub = TM // dil
        res = _dot(h_ref[...], w_ref[:, gi * DIL_GROUP_DIM:(gi + 1) * DIL_GROUP_DIM])
        for hh in range(nqkv):
            blk = res[:, hh * DIL_HEAD_DIM:(hh + 1) * DIL_HEAD_DIM]
            if hh < DIL_HEADS_PER_GROUP:
                blk = blk * DIL_HEAD_DIM ** -0.5
            if dil == 1:
                o_refs[gi][hh, 0, 0] = blk.astype(BF)
            else:
                res_ref[hh] = blk
        if dil > 1:
            for hh in range(nqkv):
                for r in range(dil):
                    o_refs[gi][hh, 0, r] = res_ref[hh, pl.ds(r, sub, stride=dil), :].astype(BF)


def _dil_proj(x2, g, w, batch, seq):
    nseq = seq // TM
    nqkv = 3 * DIL_HEADS_PER_GROUP
    const = lambda i: (0, 0)
    return pl.pallas_call(
        _dil_proj_kernel,
        grid=(batch * nseq,),
        in_specs=[
            pl.BlockSpec((TM, D_MODEL), lambda i: (i, 0)),
            pl.BlockSpec((1, D_MODEL), const),
            pl.BlockSpec((D_MODEL, DIL_GROUPS * DIL_GROUP_DIM), const, pipeline_mode=pl.Buffered(1)),
        ],
        out_specs=[
            pl.BlockSpec((nqkv, 1, dil, TM // dil, DIL_HEAD_DIM), lambda i: (0, i // nseq, 0, i % nseq, 0))
            for _, dil in DIL_PATTERNS
        ],
        out_shape=[
            jax.ShapeDtypeStruct((nqkv, batch, dil, seq // dil, DIL_HEAD_DIM), BF)
            for _, dil in DIL_PATTERNS
        ],
        scratch_shapes=[pltpu.VMEM((TM, D_MODEL), BF), pltpu.VMEM((nqkv, TM, DIL_HEAD_DIM), F32)],
        compiler_params=_params("arbitrary"),
        name="dil_proj",
    )(x2, g, w)


def _flash_kernel(q_ref, k_ref, v_ref, o_ref, m_ref, acc_ref):
    qi = pl.program_id(2)
    m_ref[...] = jnp.full(m_ref.shape, NEG, F32)
    acc_ref[...] = jnp.zeros(acc_ref.shape, F32)

    def update(hh, s, v):
        m_old = m_ref[hh]
        m_new = jnp.maximum(m_old, jnp.max(s, axis=-1, keepdims=True))
        p = jnp.exp2(s - m_new)
        acc_ref[hh] = jnp.exp2(m_old - m_new) * acc_ref[hh] + _dot(p.astype(BF), v)
        m_ref[hh] = m_new

    def body(j, carry):
        rows = pl.ds(pl.multiple_of(j * TK, TK), TK)
        for hh in range(FLASH_HEADS):
            update(hh, _dot_nt(q_ref[hh], k_ref[hh, rows, :]), v_ref[hh, rows, :])
        return carry

    lax.fori_loop(0, qi * (TQ // TK), body, 0)

    for dj in range(TQ // TK):
        rows = pl.ds(pl.multiple_of(qi * TQ + dj * TK, TK), TK)
        qpos = lax.broadcasted_iota(jnp.int32, (TQ, TK), 0)
        kpos = lax.broadcasted_iota(jnp.int32, (TQ, TK), 1) + dj * TK
        for hh in range(FLASH_HEADS):
            s = _dot_nt(q_ref[hh], k_ref[hh, rows, :])
            update(hh, jnp.where(kpos <= qpos, s, NEG), v_ref[hh, rows, :])

    for hh in range(FLASH_HEADS):
        acc = acc_ref[hh]
        o_ref[hh] = (acc[:, :V_HEAD_DIM] * (1.0 / acc[:, V_HEAD_DIM:])).astype(BF)


def _mla_flash(q, k, v, batch, seq):
    nq = seq // TQ
    t = batch * seq
    hb = FLASH_HEADS
    return pl.pallas_call(
        _flash_kernel,
        grid=(batch, MLA_HEADS // hb, nq),
        in_specs=[
            pl.BlockSpec((hb, TQ, QK_SLOT), lambda b, h, i: (h, b * nq + i, 0)),
            pl.BlockSpec((hb, seq, QK_SLOT), lambda b, h, i: (h, b, 0)),
            pl.BlockSpec((hb, seq, V_SLOT), lambda b, h, i: (h, b, 0)),
        ],
        out_specs=pl.BlockSpec((hb, TQ, V_HEAD_DIM), lambda b, h, i: (h, b * nq + i, 0)),
        out_shape=jax.ShapeDtypeStruct((MLA_HEADS, t, V_HEAD_DIM), BF),
        scratch_shapes=[pltpu.VMEM((hb, TQ, 1), F32), pltpu.VMEM((hb, TQ, V_SLOT), F32)],
        compiler_params=_params("arbitrary", "arbitrary", "arbitrary"),
        name="mla_flash",
    )(q, k, v)


def _dilated_kernel(slope_ref, q_ref, k_ref, v_ref, o_ref, lse_ref, *, group, dil, nb):
    hs = pl.program_id(1)
    slope = slope_ref[group * DIL_HEADS_PER_GROUP + hs] * float(dil)
    qi = lax.broadcasted_iota(jnp.int32, (DIL_BLOCK, DIL_BLOCK), 0)
    ki = lax.broadcasted_iota(jnp.int32, (DIL_BLOCK, DIL_BLOCK), 1)
    dist_cur = (qi - ki).astype(F32)
    dist_prev = dist_cur + float(DIL_BLOCK)
    w_sub = float(DIL_PATTERNS[group][0] // dil)
    bias_cur = jnp.where((dist_cur >= 0.0) & (dist_cur <= w_sub), -slope * dist_cur, NEG)
    bias_prev = jnp.where(dist_prev <= w_sub, -slope * dist_prev, NEG)
    eye = qi == ki

    def step(r, n):
        rows = pl.ds(pl.multiple_of(n * DIL_BLOCK, DIL_BLOCK), DIL_BLOCK)
        rows_p = pl.ds(pl.multiple_of(jnp.maximum(n - 1, 0) * DIL_BLOCK, DIL_BLOCK), DIL_BLOCK)
        q = q_ref[0, 0, r, rows, :]
        s_cur = _dot_nt(q, k_ref[0, 0, r, rows, :]) + bias_cur
        s_prev = _dot_nt(q, k_ref[0, 0, r, rows_p, :]) + jnp.where(n == 0, NEG, bias_prev)
        m = jnp.maximum(jnp.max(s_cur, axis=-1, keepdims=True), jnp.max(s_prev, axis=-1, keepdims=True))
        p_cur = jnp.exp(s_cur - m)
        p_prev = jnp.exp(s_prev - m)
        l = jnp.sum(p_cur, axis=-1, keepdims=True) + jnp.sum(p_prev, axis=-1, keepdims=True)
        o = _dot(p_cur.astype(BF), v_ref[0, 0, r, rows, :]) + _dot(p_prev.astype(BF), v_ref[0, 0, r, rows_p, :])
        o_ref[0, 0, r, rows, :] = (o * (1.0 / l)).astype(BF)
        lse = m + jnp.log(l)
        lse_ref[0, 0, pl.ds(r * nb + n, 1), :] = jnp.sum(jnp.where(eye, lse, 0.0), axis=0, keepdims=True)

    for r in range(dil):
        if nb <= 2:
            for n in range(nb):
                step(r, n)
        else:
            def body(n, carry, r=r):
                step(r, n)
                return carry
            lax.fori_loop(0, nb, body, 0, unroll=4)


def _dilated(qkv, slopes, group, batch, seq):
    _, dil = DIL_PATTERNS[group]
    sub = seq // dil
    nb = sub // DIL_BLOCK
    hg = DIL_HEADS_PER_GROUP
    blk = (1, 1, dil, sub, DIL_HEAD_DIM)

    def head(which):
        return lambda b, h: (which * hg + h, b, 0, 0, 0)

    return pl.pallas_call(
        functools.partial(_dilated_kernel, group=group, dil=dil, nb=nb),
        grid=(batch, hg),
        in_specs=[
            pl.BlockSpec(memory_space=pltpu.SMEM),
            pl.BlockSpec(blk, head(0)),
            pl.BlockSpec(blk, head(1)),
            pl.BlockSpec(blk, head(2)),
        ],
        out_specs=[
            pl.BlockSpec(blk, head(0)),
            pl.BlockSpec((1, 1, dil * nb, DIL_BLOCK), lambda b, h: (h, b, 0, 0)),
        ],
        out_shape=[
            jax.ShapeDtypeStruct((hg, batch, dil, sub, DIL_HEAD_DIM), BF),
            jax.ShapeDtypeStruct((hg, batch, dil * nb, DIL_BLOCK), F32),
        ],
        compiler_params=_params("arbitrary", "arbitrary"),
        name=f"dilated_d{dil}",
    )(slopes, qkv, qkv, qkv)


MERGE_TN = 512
MERGE_TM = 256


def _merge_kernel(x_ref, g_ref, wg_ref, bg_ref, oa_ref, o1_ref, o4_ref, o16_ref, lse_ref,
                  woa_ref, wob_ref, wout_ref, y_ref, h_ref, a_ref, b_ref, mg_ref, nat_ref):
    x = x_ref[...]
    tm = x.shape[0]
    h_ref[...] = _rms(x, g_ref[...]).astype(BF)
    for hd in range(MLA_HEADS):
        a_ref[:, hd * V_HEAD_DIM:(hd + 1) * V_HEAD_DIM] = oa_ref[hd]
    lse = lse_ref[...]
    o_refs = (o1_ref, o4_ref, o16_ref)
    for hs in range(DIL_HEADS_PER_GROUP):
        ls = [lse[:, gi * DIL_HEADS_PER_GROUP + hs:gi * DIL_HEADS_PER_GROUP + hs + 1] for gi in range(DIL_GROUPS)]
        mx = jnp.maximum(jnp.maximum(ls[0], ls[1]), ls[2])
        es = [jnp.exp(v - mx) for v in ls]
        inv = 1.0 / (es[0] + es[1] + es[2])
        comb = (es[0] * inv) * o_refs[0][hs, 0, 0].astype(F32)
        for gi in range(1, DIL_GROUPS):
            dil = DIL_PATTERNS[gi][1]
            for r in range(dil):
                nat_ref[pl.ds(r, tm // dil, stride=dil), :] = o_refs[gi][hs, 0, r].astype(F32)
            comb = comb + (es[gi] * inv) * nat_ref[...]
        b_ref[:, hs * DIL_HEAD_DIM:(hs + 1) * DIL_HEAD_DIM] = comb.astype(BF)
    for c in range(D_MODEL // MERGE_TN):
        cols = slice(c * MERGE_TN, (c + 1) * MERGE_TN)
        cols_b = slice(D_MODEL + c * MERGE_TN, D_MODEL + (c + 1) * MERGE_TN)
        gate_a = jax.nn.sigmoid(_dot(h_ref[...], wg_ref[:, cols]) + bg_ref[:, cols])
        gate_b = jax.nn.sigmoid(_dot(h_ref[...], wg_ref[:, cols_b]) + bg_ref[:, cols_b])
        o_a = _dot(a_ref[...], woa_ref[:, cols])
        o_b = _dot(b_ref[...], wob_ref[:, cols])
        mg_ref[:, cols] = (gate_a * o_a + gate_b * o_b).astype(BF)
    y_ref[...] = x + _dot(mg_ref[...], wout_ref[...])


def _merge(x2, g, wg, bg, oa, o_dil, lse, woa, wob, wout, seq):
    t = x2.shape[0]
    tm = MERGE_TM
    nseq = seq // tm
    const = lambda i: (0, 0)
    one = pl.Buffered(1)
    dil_specs = [
        pl.BlockSpec((DIL_HEADS_PER_GROUP, 1, dil, tm // dil, DIL_HEAD_DIM),
                     lambda i: (0, i // nseq, 0, i % nseq, 0))
        for _, dil in DIL_PATTERNS
    ]
    return pl.pallas_call(
        _merge_kernel,
        grid=(t // tm,),
        in_specs=[
            pl.BlockSpec((tm, D_MODEL), lambda i: (i, 0)),
            pl.BlockSpec((1, D_MODEL), const),
            pl.BlockSpec((D_MODEL, 2 * D_MODEL), const, pipeline_mode=one),
            pl.BlockSpec((1, 2 * D_MODEL), const),
            pl.BlockSpec((MLA_HEADS, tm, V_HEAD_DIM), lambda i: (0, i, 0)),
            *dil_specs,
            pl.BlockSpec((tm, DIL_HEADS), lambda i: (i, 0)),
            pl.BlockSpec((MLA_HEADS * V_HEAD_DIM, D_MODEL), const, pipeline_mode=one),
            pl.BlockSpec((DIL_HEADS_PER_GROUP * DIL_HEAD_DIM, D_MODEL), const, pipeline_mode=one),
            pl.BlockSpec((D_MODEL, D_MODEL), const, pipeline_mode=one),
        ],
        out_specs=pl.BlockSpec((tm, D_MODEL), lambda i: (i, 0)),
        out_shape=jax.ShapeDtypeStruct((t, D_MODEL), F32),
        scratch_shapes=[
            pltpu.VMEM((tm, D_MODEL), BF),
            pltpu.VMEM((tm, MLA_HEADS * V_HEAD_DIM), BF),
            pltpu.VMEM((tm, DIL_HEADS_PER_GROUP * DIL_HEAD_DIM), BF),
            pltpu.VMEM((tm, D_MODEL), BF),
            pltpu.VMEM((tm, DIL_HEAD_DIM), F32),
        ],
        compiler_params=_params("arbitrary"),
        name="merge",
    )(x2, g, wg, bg, oa, *o_dil, lse, woa, wob, wout)


def _ffn_kernel(x_ref, g_ref, wup_ref, wgt_ref, cwu_ref, cwg_ref, cbu_ref, cbg_ref, wd_ref, fg_ref, y_ref,
                h_ref, acc_ref, u_ref, carry_ref, *, tiles_per_seq):
    i = pl.program_id(0)
    c = pl.program_id(1)

    @pl.when(c == 0)
    def _():
        h_ref[...] = _rms(x_ref[...], g_ref[...]).astype(BF)
        acc_ref[...] = jnp.zeros(acc_ref.shape, F32)

    seq_start = (i % tiles_per_seq) == 0

    @pl.when(seq_start)
    def _():
        u_ref[:SUBLANE] = jnp.zeros((SUBLANE, 2 * FF_CHUNK), F32)

    @pl.when(jnp.logical_not(seq_start))
    def _():
        u_ref[:SUBLANE] = carry_ref[c]

    u_ref[SUBLANE:, :FF_CHUNK] = _dot(h_ref[...], wup_ref[...])
    u_ref[SUBLANE:, FF_CHUNK:] = _dot(h_ref[...], wgt_ref[...])
    carry_ref[c] = u_ref[TM:]

    def conv(cols, cw_ref, cb_ref):
        cw = cw_ref[...]
        out = cb_ref[...]
        for tap in range(CONV_WIDTH):
            off = SUBLANE - (CONV_WIDTH - 1) + tap
            out = out + cw[tap:tap + 1] * u_ref[off:off + TM, cols]
        return out

    up = conv(slice(0, FF_CHUNK), cwu_ref, cbu_ref)
    gate = conv(slice(FF_CHUNK, 2 * FF_CHUNK), cwg_ref, cbg_ref)
    acc_ref[...] += _dot((jax.nn.silu(gate) * up).astype(BF), wd_ref[...])

    @pl.when(c == FF_CHUNKS - 1)
    def _():
        y_ref[...] = _rms(x_ref[...] + acc_ref[...], fg_ref[...])


def _ffn(x1, g, wu, cw, cb, wd, fg, seq):
    t = x1.shape[0]
    up_blk = lambda i, c: (0, c)
    gate_blk = lambda i, c: (0, FF_CHUNKS + c)
    return pl.pallas_call(
        functools.partial(_ffn_kernel, tiles_per_seq=seq // TM),
        grid=(t // TM, FF_CHUNKS),
        in_specs=[
            pl.BlockSpec((TM, D_MODEL), lambda i, c: (i, 0)),
            pl.BlockSpec((1, D_MODEL), lambda i, c: (0, 0)),
            pl.BlockSpec((D_MODEL, FF_CHUNK), up_blk),
            pl.BlockSpec((D_MODEL, FF_CHUNK), gate_blk),
            pl.BlockSpec((SUBLANE, FF_CHUNK), up_blk),
            pl.BlockSpec((SUBLANE, FF_CHUNK), gate_blk),
            pl.BlockSpec((1, FF_CHUNK), up_blk),
            pl.BlockSpec((1, FF_CHUNK), gate_blk),
            pl.BlockSpec((FF_CHUNK, D_MODEL), lambda i, c: (c, 0)),
            pl.BlockSpec((1, D_MODEL), lambda i, c: (0, 0)),
        ],
        out_specs=pl.BlockSpec((TM, D_MODEL), lambda i, c: (i, 0)),
        out_shape=jax.ShapeDtypeStruct((t, D_MODEL), F32),
        scratch_shapes=[
            pltpu.VMEM((TM, D_MODEL), BF),
            pltpu.VMEM((TM, D_MODEL), F32),
            pltpu.VMEM((TM + SUBLANE, 2 * FF_CHUNK), F32),
            pltpu.VMEM((FF_CHUNKS, SUBLANE, 2 * FF_CHUNK), F32),
        ],
        compiler_params=_params("arbitrary", "arbitrary"),
        name="ffn",
    )(x1, g, wu, wu, cw, cw, cb, cb, wd, fg)


def _rotate_half_cols(w):
    half = w.shape[-1] // 2
    return jnp.concatenate([-w[..., half:], w[..., :half]], axis=-1)


def _pad_ff_halves(a, dtype):
    r = a.shape[0]
    a = jnp.pad(a.reshape(r, 2, D_FF).astype(dtype), ((0, 0), (0, 0), (0, D_FF_PAD - D_FF)))
    return a.reshape(r, 2 * D_FF_PAD)


def kernel(x, attn_norm_g, w_in, b_gate, q_norm_g, w_uq, kv_norm_g, w_ukv, w_o_mla, w_o_dil,
           w_out, ffn_norm_g, w_up, conv_w, conv_b, w_down, final_norm_g):
    batch, seq, _ = x.shape
    assert w_in.shape[0] == 1, "single-layer block"
    assert seq % TM == 0 and seq % TQ == 0 and seq % (16 * DIL_BLOCK) == 0
    t = batch * seq
    x2 = x.reshape(t, D_MODEL)

    dqkv = DIL_HEADS * DIL_HEAD_DIM
    o = np.cumsum((0, Q_LORA_RANK, KV_LORA_RANK, QK_ROPE_DIM, dqkv, dqkv, dqkv, D_MODEL, D_MODEL))
    wi = w_in[0].astype(BF)
    w_kpe = wi[:, o[2]:o[3]]
    wlat = jnp.concatenate([wi[:, o[0]:o[2]], w_kpe, _rotate_half_cols(w_kpe)], axis=1)
    wdil = wi[:, o[3]:o[6]].reshape(D_MODEL, 3, DIL_GROUPS, DIL_HEADS_PER_GROUP * DIL_HEAD_DIM)
    wdil = wdil.transpose(0, 2, 1, 3).reshape(D_MODEL, DIL_GROUPS * DIL_GROUP_DIM)
    wgate = wi[:, o[6]:o[8]]
    wq = w_uq[0].astype(BF).reshape(Q_LORA_RANK, MLA_HEADS, QK_NOPE_DIM + QK_ROPE_DIM)
    wq_pe = wq[..., QK_NOPE_DIM:]
    wq = jnp.concatenate([wq[..., :QK_NOPE_DIM], wq_pe, _rotate_half_cols(wq_pe)], axis=-1)
    wq = wq.reshape(Q_LORA_RANK, MLA_HEADS * QK_SLOT)
    wkv = w_ukv[0].astype(BF)
    wu = _pad_ff_halves(w_up[0], BF)
    cw = _pad_ff_halves(jnp.pad(conv_w[0], ((0, SUBLANE - CONV_WIDTH), (0, 0))), F32)
    cb = _pad_ff_halves(conv_b, F32)
    wd = jnp.pad(w_down[0].astype(BF), ((0, D_FF_PAD - D_FF), (0, 0)))

    pos = jnp.arange(seq, dtype=F32)
    inv_freq = ROPE_THETA ** (-jnp.arange(0, QK_ROPE_DIM, 2, dtype=F32) / QK_ROPE_DIM)
    ang = pos[:, None] * inv_freq[None, :]
    cs = jnp.concatenate([jnp.cos(ang), jnp.cos(ang), jnp.sin(ang), jnp.sin(ang)], axis=1)
    slopes = 2.0 ** (-ALIBI_MAX_BIAS * jnp.arange(1, DIL_HEADS + 1, dtype=F32) / DIL_HEADS)

    q, k, v = _mla_prep(x2, attn_norm_g, wlat, q_norm_g, kv_norm_g, wq, wkv, cs, seq)
    o_mla = _mla_flash(q, k, v, batch, seq)
    qkv_dil = _dil_proj(x2, attn_norm_g, wdil, batch, seq)
    o_dil, lse_dil = [], []
    for gi, (_, dil) in enumerate(DIL_PATTERNS):
        o_g, lse_g = _dilated(qkv_dil[gi], slopes, gi, batch, seq)
        nb = seq // dil // DIL_BLOCK
        o_dil.append(o_g)
        lse_g = lse_g.reshape(DIL_HEADS_PER_GROUP, batch, dil, nb, DIL_BLOCK)
        lse_dil.append(lse_g.transpose(1, 3, 4, 2, 0).reshape(t, DIL_HEADS_PER_GROUP))
    lse = jnp.concatenate(lse_dil, axis=1)
    x1 = _merge(x2, attn_norm_g, wgate, b_gate, o_mla, o_dil, lse,
                w_o_mla[0].astype(BF), w_o_dil[0].astype(BF), w_out[0].astype(BF), seq)

    y = _ffn(x1, ffn_norm_g, wu, cw, cb, wd, final_norm_g.reshape(1, D_MODEL), seq)
    return y.reshape(batch, seq, D_MODEL)
```

```python
import functools
import math

import numpy as np
import jax
import jax.numpy as jnp
from jax import lax
from jax.experimental import pallas as pl
from jax.experimental.pallas import tpu as pltpu

D_MODEL = 2048
MLA_HEADS = 8
QK_NOPE_DIM = 128
QK_ROPE_DIM = 64
V_HEAD_DIM = 128
Q_LORA_RANK = 512
KV_LORA_RANK = 256
ROPE_THETA = 10000.0
DIL_PATTERNS = ((128, 1), (512, 4), (2048, 16))
DIL_GROUPS = 3
DIL_HEADS_PER_GROUP = 4
DIL_HEADS = DIL_GROUPS * DIL_HEADS_PER_GROUP
DIL_HEAD_DIM = 128
DIL_BLOCK = 128
ALIBI_MAX_BIAS = 8.0
D_FF = 5504
CONV_WIDTH = 3
NORM_EPS = 1e-6

LANE = 128
SUBLANE = 8
VMEM_LIMIT = 56 * 1024 * 1024

LAT_DIM = Q_LORA_RANK + KV_LORA_RANK + 2 * QK_ROPE_DIM
QK_SLOT = 2 * LANE
V_SLOT = 2 * LANE
DIL_GROUP_DIM = 3 * DIL_HEADS_PER_GROUP * DIL_HEAD_DIM
FF_CHUNK = 512
D_FF_PAD = -(-D_FF // FF_CHUNK) * FF_CHUNK
FF_CHUNKS = D_FF_PAD // FF_CHUNK
NEG = -1e30

TM = 512
TQ = 512
TK = 512
FLASH_HEADS = 2

BF = jnp.bfloat16
F32 = jnp.float32


def _params(*sem):
    return pltpu.CompilerParams(dimension_semantics=sem, vmem_limit_bytes=VMEM_LIMIT)


def _rms(x, g):
    return x * lax.rsqrt(jnp.mean(x * x, axis=-1, keepdims=True) + NORM_EPS) * g


def _dot(a, b):
    return jnp.dot(a, b, preferred_element_type=F32)


def _dot_nt(a, b):
    return lax.dot_general(a, b, (((1,), (1,)), ((), ())), preferred_element_type=F32)


def _rope_pair(t):
    lane = lax.broadcasted_iota(jnp.int32, t.shape, 1)
    return jnp.where(lane < QK_ROPE_DIM, t + pltpu.roll(t, QK_ROPE_DIM, 1), 0.0)


def _mla_prep_kernel(x_ref, g_ref, wlat_ref, qg_ref, kvg_ref, wq_ref, wkv_ref, cs_ref,
                     q_ref, k_ref, v_ref):
    h = _rms(x_ref[...], g_ref[...]).astype(BF)
    lat = _dot(h, wlat_ref[...])
    cq = _rms(lat[:, :Q_LORA_RANK], qg_ref[...]).astype(BF)
    ckv = _rms(lat[:, Q_LORA_RANK:Q_LORA_RANK + KV_LORA_RANK], kvg_ref[...]).astype(BF)
    q = _dot(cq, wq_ref[...])
    kv = _dot(ckv, wkv_ref[...])
    cs = cs_ref[...]
    scale = (QK_NOPE_DIM + QK_ROPE_DIM) ** -0.5 * math.log2(math.e)
    k_rope = _rope_pair(lat[:, Q_LORA_RANK + KV_LORA_RANK:] * cs).astype(BF)
    ones = jnp.ones((x_ref.shape[0], V_SLOT - V_HEAD_DIM), BF)
    for hd in range(MLA_HEADS):
        qs = q[:, hd * QK_SLOT:(hd + 1) * QK_SLOT]
        q_ref[hd, :, :LANE] = (qs[:, :LANE] * scale).astype(BF)
        q_ref[hd, :, LANE:] = _rope_pair(qs[:, LANE:] * (cs * scale)).astype(BF)
        k_ref[hd, :, :LANE] = kv[:, hd * 256:hd * 256 + LANE].astype(BF)
        k_ref[hd, :, LANE:] = k_rope
        v_ref[hd, :, :V_HEAD_DIM] = kv[:, hd * 256 + LANE:(hd + 1) * 256].astype(BF)
        v_ref[hd, :, V_HEAD_DIM:] = ones


def _mla_prep(x2, g, wlat, qg, kvg, wq, wkv, cs, seq):
    t = x2.shape[0]
    nseq = seq // TM
    const = lambda i: (0, 0)
    return pl.pallas_call(
        _mla_prep_kernel,
        grid=(t // TM,),
        in_specs=[
            pl.BlockSpec((TM, D_MODEL), lambda i: (i, 0)),
            pl.BlockSpec((1, D_MODEL), const),
            pl.BlockSpec((D_MODEL, LAT_DIM), const),
            pl.BlockSpec((1, Q_LORA_RANK), const),
            pl.BlockSpec((1, KV_LORA_RANK), const),
            pl.BlockSpec((Q_LORA_RANK, MLA_HEADS * QK_SLOT), const),
            pl.BlockSpec((KV_LORA_RANK, MLA_HEADS * 256), const),
            pl.BlockSpec((TM, LANE), lambda i: (i % nseq, 0)),
        ],
        out_specs=[
            pl.BlockSpec((MLA_HEADS, TM, QK_SLOT), lambda i: (0, i, 0)),
            pl.BlockSpec((MLA_HEADS, TM, QK_SLOT), lambda i: (0, i, 0)),
            pl.BlockSpec((MLA_HEADS, TM, V_SLOT), lambda i: (0, i, 0)),
        ],
        out_shape=[
            jax.ShapeDtypeStruct((MLA_HEADS, t, QK_SLOT), BF),
            jax.ShapeDtypeStruct((MLA_HEADS, t, QK_SLOT), BF),
            jax.ShapeDtypeStruct((MLA_HEADS, t, V_SLOT), BF),
        ],
        compiler_params=_params("arbitrary"),
        name="mla_prep",
    )(x2, g, wlat, qg, kvg, wq, wkv, cs)


def _dil_proj_kernel(x_ref, g_ref, w_ref, *rest):
    o_refs, h_ref, res_ref = rest[:DIL_GROUPS], rest[DIL_GROUPS], rest[DIL_GROUPS + 1]
    nqkv = 3 * DIL_HEADS_PER_GROUP
    h_ref[...] = _rms(x_ref[...], g_ref[...]).astype(BF)
    for gi, (_, dil) in enumerate(DIL_PATTERNS):
        sub = TM // dil
        res = _dot(h_ref[...], w_ref[:, gi * DIL_GROUP_DIM:(gi + 1) * DIL_GROUP_DIM])
        for hh in range(nqkv):
            blk = res[:, hh * DIL_HEAD_DIM:(hh + 1) * DIL_HEAD_DIM]
            if hh < DIL_HEADS_PER_GROUP:
                blk = blk * DIL_HEAD_DIM ** -0.5
            if dil == 1:
                o_refs[gi][hh, 0, 0] = blk.astype(BF)
            else:
                res_ref[hh] = blk
        if dil > 1:
            for hh in range(nqkv):
                for r in range(dil):
                    o_refs[gi][hh, 0, r] = res_ref[hh, pl.ds(r, sub, stride=dil), :].astype(BF)


def _dil_proj(x2, g, w, batch, seq):
    nseq = seq // TM
    nqkv = 3 * DIL_HEADS_PER_GROUP
    const = lambda i: (0, 0)
    return pl.pallas_call(
        _dil_proj_kernel,
        grid=(batch * nseq,),
        in_specs=[
            pl.BlockSpec((TM, D_MODEL), lambda i: (i, 0)),
            pl.BlockSpec((1, D_MODEL), const),
            pl.BlockSpec((D_MODEL, DIL_GROUPS * DIL_GROUP_DIM), const, pipeline_mode=pl.Buffered(1)),
        ],
        out_specs=[
            pl.BlockSpec((nqkv, 1, dil, TM // dil, DIL_HEAD_DIM), lambda i: (0, i // nseq, 0, i % nseq, 0))
            for _, dil in DIL_PATTERNS
        ],
        out_shape=[
            jax.ShapeDtypeStruct((nqkv, batch, dil, seq // dil, DIL_HEAD_DIM), BF)
            for _, dil in DIL_PATTERNS
        ],
        scratch_shapes=[pltpu.VMEM((TM, D_MODEL), BF), pltpu.VMEM((nqkv, TM, DIL_HEAD_DIM), F32)],
        compiler_params=_params("arbitrary"),
        name="dil_proj",
    )(x2, g, w)


def _flash_kernel(q_ref, k_ref, v_ref, o_ref, *scratch):
    nh = FLASH_HEADS
    m_refs, acc_refs = scratch[:nh], scratch[nh:2 * nh]
    s_refs = (scratch[2 * nh:3 * nh], scratch[3 * nh:4 * nh])
    qi = pl.program_id(2)
    for hh in range(nh):
        m_refs[hh][...] = jnp.full(m_refs[hh].shape, NEG, F32)
        acc_refs[hh][...] = jnp.zeros(acc_refs[hh].shape, F32)

    def key_rows(j):
        return pl.ds(pl.multiple_of(j * TK, TK), TK)

    def scores(j, slot):
        for hh in range(nh):
            s_refs[slot][hh][...] = _dot_nt(q_ref[hh], k_ref[hh, key_rows(j), :])

    def mask_diagonal(slot):
        keep = lax.broadcasted_iota(jnp.int32, (TQ, TK), 1) <= lax.broadcasted_iota(jnp.int32, (TQ, TK), 0)
        for hh in range(nh):
            s_refs[slot][hh][...] = jnp.where(keep, s_refs[slot][hh][...], NEG)

    def update(j, slot):
        for hh in range(nh):
            s = s_refs[slot][hh][...]
            m_old = m_refs[hh][...]
            m_new = jnp.maximum(m_old, jnp.max(s, axis=-1, keepdims=True))
            p = jnp.exp2(s - m_new)
            pv = _dot(p.astype(BF), v_ref[hh, key_rows(j), :])
            acc_refs[hh][...] = jnp.exp2(m_old - m_new) * acc_refs[hh][...] + pv
            m_refs[hh][...] = m_new

    scores(0, 0)

    @pl.when(qi == 0)
    def _():
        mask_diagonal(0)

    def body(j, carry):
        for slot in range(2):
            @pl.when(j % 2 == slot)
            def _(slot=slot):
                scores(j, slot)
                update(j - 1, 1 - slot)

            @pl.when((j % 2 == slot) & (j == qi))
            def _(slot=slot):
                mask_diagonal(slot)
        return carry

    lax.fori_loop(1, qi + 1, body, 0)

    for slot in range(2):
        @pl.when(qi % 2 == slot)
        def _(slot=slot):
            update(qi, slot)

    for hh in range(FLASH_HEADS):
        acc = acc_refs[hh][...]
        o_ref[hh] = (acc[:, :V_HEAD_DIM] * (1.0 / acc[:, V_HEAD_DIM:])).astype(BF)


def _mla_flash(q, k, v, batch, seq):
    nq = seq // TQ
    t = batch * seq
    hb = FLASH_HEADS
    return pl.pallas_call(
        _flash_kernel,
        grid=(batch, MLA_HEADS // hb, nq),
        in_specs=[
            pl.BlockSpec((hb, TQ, QK_SLOT), lambda b, h, i: (h, b * nq + i, 0)),
            pl.BlockSpec((hb, seq, QK_SLOT), lambda b, h, i: (h, b, 0)),
            pl.BlockSpec((hb, seq, V_SLOT), lambda b, h, i: (h, b, 0)),
        ],
        out_specs=pl.BlockSpec((hb, TQ, V_HEAD_DIM), lambda b, h, i: (h, b * nq + i, 0)),
        out_shape=jax.ShapeDtypeStruct((MLA_HEADS, t, V_HEAD_DIM), BF),
        scratch_shapes=([pltpu.VMEM((TQ, 1), F32)] * hb + [pltpu.VMEM((TQ, V_SLOT), F32)] * hb
                        + [pltpu.VMEM((TQ, TK), F32)] * (2 * hb)),
        compiler_params=_params("arbitrary", "arbitrary", "arbitrary"),
        name="mla_flash",
    )(q, k, v)


def _dilated_kernel(slope_ref, q_ref, k_ref, v_ref, o_ref, lse_ref, *, group, dil, nb):
    hs = pl.program_id(1)
    slope = slope_ref[group * DIL_HEADS_PER_GROUP + hs] * float(dil)
    qi = lax.broadcasted_iota(jnp.int32, (DIL_BLOCK, DIL_BLOCK), 0)
    ki = lax.broadcasted_iota(jnp.int32, (DIL_BLOCK, DIL_BLOCK), 1)
    dist_cur = (qi - ki).astype(F32)
    dist_prev = dist_cur + float(DIL_BLOCK)
    w_sub = float(DIL_PATTERNS[group][0] // dil)
    bias_cur = jnp.where((dist_cur >= 0.0) & (dist_cur <= w_sub), -slope * dist_cur, NEG)
    bias_prev = jnp.where(dist_prev <= w_sub, -slope * dist_prev, NEG)
    eye = qi == ki

    def step(r, n):
        rows = pl.ds(pl.multiple_of(n * DIL_BLOCK, DIL_BLOCK), DIL_BLOCK)
        rows_p = pl.ds(pl.multiple_of(jnp.maximum(n - 1, 0) * DIL_BLOCK, DIL_BLOCK), DIL_BLOCK)
        q = q_ref[0, 0, r, rows, :]
        s_cur = _dot_nt(q, k_ref[0, 0, r, rows, :]) + bias_cur
        s_prev = _dot_nt(q, k_ref[0, 0, r, rows_p, :]) + jnp.where(n == 0, NEG, bias_prev)
        m = jnp.maximum(jnp.max(s_cur, axis=-1, keepdims=True), jnp.max(s_prev, axis=-1, keepdims=True))
        p_cur = jnp.exp(s_cur - m)
        p_prev = jnp.exp(s_prev - m)
        l = jnp.sum(p_cur, axis=-1, keepdims=True) + jnp.sum(p_prev, axis=-1, keepdims=True)
        o = _dot(p_cur.astype(BF), v_ref[0, 0, r, rows, :]) + _dot(p_prev.astype(BF), v_ref[0, 0, r, rows_p, :])
        o_ref[0, 0, r, rows, :] = (o * (1.0 / l)).astype(BF)
        lse = m + jnp.log(l)
        lse_ref[0, 0, pl.ds(r * nb + n, 1), :] = jnp.sum(jnp.where(eye, lse, 0.0), axis=0, keepdims=True)

    for r in range(dil):
        if nb <= 2:
            for n in range(nb):
                step(r, n)
        else:
            def body(n, carry, r=r):
                step(r, n)
                return carry
            lax.fori_loop(0, nb, body, 0, unroll=4)


def _dilated(qkv, slopes, group, batch, seq):
    _, dil = DIL_PATTERNS[group]
    sub = seq // dil
    nb = sub // DIL_BLOCK
    hg = DIL_HEADS_PER_GROUP
    blk = (1, 1, dil, sub, DIL_HEAD_DIM)

    def head(which):
        return lambda b, h: (which * hg + h, b, 0, 0, 0)

    return pl.pallas_call(
        functools.partial(_dilated_kernel, group=group, dil=dil, nb=nb),
        grid=(batch, hg),
        in_specs=[
            pl.BlockSpec(memory_space=pltpu.SMEM),
            pl.BlockSpec(blk, head(0)),
            pl.BlockSpec(blk, head(1)),
            pl.BlockSpec(blk, head(2)),
        ],
        out_specs=[
            pl.BlockSpec(blk, head(0)),
            pl.BlockSpec((1, 1, dil * nb, DIL_BLOCK), lambda b, h: (h, b, 0, 0)),
        ],
        out_shape=[
            jax.ShapeDtypeStruct((hg, batch, dil, sub, DIL_HEAD_DIM), BF),
            jax.ShapeDtypeStruct((hg, batch, dil * nb, DIL_BLOCK), F32),
        ],
        compiler_params=_params("arbitrary", "arbitrary"),
        name=f"dilated_d{dil}",
    )(slopes, qkv, qkv, qkv)


MERGE_TN = 512
MERGE_TM = 256


def _merge_kernel(x_ref, g_ref, wg_ref, bg_ref, oa_ref, o1_ref, o4_ref, o16_ref, lse_ref,
                  woa_ref, wob_ref, wout_ref, y_ref, h_ref, a_ref, b_ref, mg_ref, nat_ref):
    x = x_ref[...]
    tm = x.shape[0]
    h_ref[...] = _rms(x, g_ref[...]).astype(BF)
    for hd in range(MLA_HEADS):
        a_ref[:, hd * V_HEAD_DIM:(hd + 1) * V_HEAD_DIM] = oa_ref[hd]
    lse = lse_ref[...]
    o_refs = (o1_ref, o4_ref, o16_ref)
    for hs in range(DIL_HEADS_PER_GROUP):
        ls = [lse[:, gi * DIL_HEADS_PER_GROUP + hs:gi * DIL_HEADS_PER_GROUP + hs + 1] for gi in range(DIL_GROUPS)]
        mx = jnp.maximum(jnp.maximum(ls[0], ls[1]), ls[2])
        es = [jnp.exp(v - mx) for v in ls]
        inv = 1.0 / (es[0] + es[1] + es[2])
        comb = (es[0] * inv) * o_refs[0][hs, 0, 0].astype(F32)
        for gi in range(1, DIL_GROUPS):
            dil = DIL_PATTERNS[gi][1]
            for r in range(dil):
                nat_ref[pl.ds(r, tm // dil, stride=dil), :] = o_refs[gi][hs, 0, r].astype(F32)
            comb = comb + (es[gi] * inv) * nat_ref[...]
        b_ref[:, hs * DIL_HEAD_DIM:(hs + 1) * DIL_HEAD_DIM] = comb.astype(BF)
    for c in range(D_MODEL // MERGE_TN):
        cols = slice(c * MERGE_TN, (c + 1) * MERGE_TN)
        cols_b = slice(D_MODEL + c * MERGE_TN, D_MODEL + (c + 1) * MERGE_TN)
        gate_a = jax.nn.sigmoid(_dot(h_ref[...], wg_ref[:, cols]) + bg_ref[:, cols])
        gate_b = jax.nn.sigmoid(_dot(h_ref[...], wg_ref[:, cols_b]) + bg_ref[:, cols_b])
        o_a = _dot(a_ref[...], woa_ref[:, cols])
        o_b = _dot(b_ref[...], wob_ref[:, cols])
        mg_ref[:, cols] = (gate_a * o_a + gate_b * o_b).astype(BF)
    y_ref[...] = x + _dot(mg_ref[...], wout_ref[...])


def _merge(x2, g, wg, bg, oa, o_dil, lse, woa, wob, wout, seq):
    t = x2.shape[0]
    tm = MERGE_TM
    nseq = seq // tm
    const = lambda i: (0, 0)
    one = pl.Buffered(1)
    dil_specs = [
        pl.BlockSpec((DIL_HEADS_PER_GROUP, 1, dil, tm // dil, DIL_HEAD_DIM),
                     lambda i: (0, i // nseq, 0, i % nseq, 0))
        for _, dil in DIL_PATTERNS
    ]
    return pl.pallas_call(
        _merge_kernel,
        grid=(t // tm,),
        in_specs=[
            pl.BlockSpec((tm, D_MODEL), lambda i: (i, 0)),
            pl.BlockSpec((1, D_MODEL), const),
            pl.BlockSpec((D_MODEL, 2 * D_MODEL), const, pipeline_mode=one),
            pl.BlockSpec((1, 2 * D_MODEL), const),
            pl.BlockSpec((MLA_HEADS, tm, V_HEAD_DIM), lambda i: (0, i, 0)),
            *dil_specs,
            pl.BlockSpec((tm, DIL_HEADS), lambda i: (i, 0)),
            pl.BlockSpec((MLA_HEADS * V_HEAD_DIM, D_MODEL), const, pipeline_mode=one),
            pl.BlockSpec((DIL_HEADS_PER_GROUP * DIL_HEAD_DIM, D_MODEL), const, pipeline_mode=one),
            pl.BlockSpec((D_MODEL, D_MODEL), const, pipeline_mode=one),
        ],
        out_specs=pl.BlockSpec((tm, D_MODEL), lambda i: (i, 0)),
        out_shape=jax.ShapeDtypeStruct((t, D_MODEL), F32),
        scratch_shapes=[
            pltpu.VMEM((tm, D_MODEL), BF),
            pltpu.VMEM((tm, MLA_HEADS * V_HEAD_DIM), BF),
            pltpu.VMEM((tm, DIL_HEADS_PER_GROUP * DIL_HEAD_DIM), BF),
            pltpu.VMEM((tm, D_MODEL), BF),
            pltpu.VMEM((tm, DIL_HEAD_DIM), F32),
        ],
        compiler_params=_params("arbitrary"),
        name="merge",
    )(x2, g, wg, bg, oa, *o_dil, lse, woa, wob, wout)


def _ffn_kernel(x_ref, g_ref, wup_ref, wgt_ref, cwu_ref, cwg_ref, cbu_ref, cbg_ref, wd_ref, fg_ref, y_ref,
                h_ref, acc_ref, u0_ref, u1_ref, carry_ref, *, tiles_per_seq):
    i = pl.program_id(0)
    c = pl.program_id(1)
    u_refs = (u0_ref, u1_ref)
    seq_start = (i % tiles_per_seq) == 0

    @pl.when((i == 0) & (c == 0))
    def _():
        carry_ref[...] = jnp.zeros(carry_ref.shape, F32)

    def project(slot):
        u_ref = u_refs[slot]
        u_ref[:SUBLANE] = jnp.where(seq_start, 0.0, carry_ref[c])
        u_ref[SUBLANE:, :FF_CHUNK] = _dot(h_ref[...], wup_ref[...])
        u_ref[SUBLANE:, FF_CHUNK:] = _dot(h_ref[...], wgt_ref[...])
        carry_ref[c] = u_ref[TM:]

    def finish(slot):
        u_ref = u_refs[slot]

        def conv(cols, cw_ref, cb_ref):
            cw = cw_ref[...]
            out = cb_ref[...]
            for tap in range(CONV_WIDTH):
                off = SUBLANE - (CONV_WIDTH - 1) + tap
                out = out + cw[tap:tap + 1] * u_ref[off:off + TM, cols]
            return out

        up = conv(slice(0, FF_CHUNK), cwu_ref, cbu_ref)
        gate = conv(slice(FF_CHUNK, 2 * FF_CHUNK), cwg_ref, cbg_ref)
        acc_ref[...] += _dot((jax.nn.silu(gate) * up).astype(BF), wd_ref[...])

    @pl.when(c == 0)
    def _():
        h_ref[...] = _rms(x_ref[...], g_ref[...]).astype(BF)
        acc_ref[...] = jnp.zeros(acc_ref.shape, F32)
        project(0)

    for slot in range(2):
        @pl.when((c > 0) & (c < FF_CHUNKS) & (c % 2 == slot))
        def _(slot=slot):
            project(slot)
            finish(1 - slot)

    @pl.when(c == FF_CHUNKS)
    def _():
        finish((FF_CHUNKS - 1) % 2)
        y_ref[...] = _rms(x_ref[...] + acc_ref[...], fg_ref[...])


def _ffn(x1, g, wu, cw, cb, wd, fg, seq):
    t = x1.shape[0]
    last = FF_CHUNKS - 1
    up_blk = lambda i, c: (0, jnp.minimum(c, last))
    gate_blk = lambda i, c: (0, FF_CHUNKS + jnp.minimum(c, last))
    up_prev = lambda i, c: (0, jnp.maximum(c - 1, 0))
    gate_prev = lambda i, c: (0, FF_CHUNKS + jnp.maximum(c - 1, 0))
    return pl.pallas_call(
        functools.partial(_ffn_kernel, tiles_per_seq=seq // TM),
        grid=(t // TM, FF_CHUNKS + 1),
        in_specs=[
            pl.BlockSpec((TM, D_MODEL), lambda i, c: (i, 0)),
            pl.BlockSpec((1, D_MODEL), lambda i, c: (0, 0)),
            pl.BlockSpec((D_MODEL, FF_CHUNK), up_blk),
            pl.BlockSpec((D_MODEL, FF_CHUNK), gate_blk),
            pl.BlockSpec((SUBLANE, FF_CHUNK), up_prev),
            pl.BlockSpec((SUBLANE, FF_CHUNK), gate_prev),
            pl.BlockSpec((1, FF_CHUNK), up_prev),
            pl.BlockSpec((1, FF_CHUNK), gate_prev),
            pl.BlockSpec((FF_CHUNK, D_MODEL), lambda i, c: (jnp.maximum(c - 1, 0), 0)),
            pl.BlockSpec((1, D_MODEL), lambda i, c: (0, 0)),
        ],
        out_specs=pl.BlockSpec((TM, D_MODEL), lambda i, c: (i, 0)),
        out_shape=jax.ShapeDtypeStruct((t, D_MODEL), F32),
        scratch_shapes=[
            pltpu.VMEM((TM, D_MODEL), BF),
            pltpu.VMEM((TM, D_MODEL), F32),
            pltpu.VMEM((TM + SUBLANE, 2 * FF_CHUNK), F32),
            pltpu.VMEM((TM + SUBLANE, 2 * FF_CHUNK), F32),
            pltpu.VMEM((FF_CHUNKS, SUBLANE, 2 * FF_CHUNK), F32),
        ],
        compiler_params=_params("arbitrary", "arbitrary"),
        name="ffn",
    )(x1, g, wu, wu, cw, cw, cb, cb, wd, fg)


def _rotate_half_cols(w):
    half = w.shape[-1] // 2
    return jnp.concatenate([-w[..., half:], w[..., :half]], axis=-1)


def _pad_ff_halves(a, dtype):
    zeros = jnp.zeros((a.shape[0], D_FF_PAD - D_FF), dtype)
    return jnp.concatenate([a[:, :D_FF].astype(dtype), zeros, a[:, D_FF:].astype(dtype), zeros], axis=1)


def kernel(x, attn_norm_g, w_in, b_gate, q_norm_g, w_uq, kv_norm_g, w_ukv, w_o_mla, w_o_dil,
           w_out, ffn_norm_g, w_up, conv_w, conv_b, w_down, final_norm_g):
    batch, seq, _ = x.shape
    assert w_in.shape[0] == 1, "single-layer block"
    assert seq % TM == 0 and seq % TQ == 0 and TQ == TK and seq % (16 * DIL_BLOCK) == 0
    t = batch * seq
    x2 = x.reshape(t, D_MODEL)

    dqkv = DIL_HEADS * DIL_HEAD_DIM
    o = np.cumsum((0, Q_LORA_RANK, KV_LORA_RANK, QK_ROPE_DIM, dqkv, dqkv, dqkv, D_MODEL, D_MODEL))
    wi = w_in[0]
    w_kpe = wi[:, o[2]:o[3]]
    wlat = jnp.concatenate([wi[:, o[0]:o[2]], w_kpe, _rotate_half_cols(w_kpe)], axis=1).astype(BF)
    gw = DIL_HEADS_PER_GROUP * DIL_HEAD_DIM
    wdil = jnp.concatenate(
        [wi[:, o[3 + which] + gi * gw:o[3 + which] + (gi + 1) * gw].astype(BF)
         for gi in range(DIL_GROUPS) for which in range(3)], axis=1)
    wgate = wi[:, o[6]:o[8]].astype(BF)
    wq = w_uq[0].astype(BF).reshape(Q_LORA_RANK, MLA_HEADS, QK_NOPE_DIM + QK_ROPE_DIM)
    wq_pe = wq[..., QK_NOPE_DIM:]
    wq = jnp.concatenate([wq[..., :QK_NOPE_DIM], wq_pe, _rotate_half_cols(wq_pe)], axis=-1)
    wq = wq.reshape(Q_LORA_RANK, MLA_HEADS * QK_SLOT)
    wkv = w_ukv[0].astype(BF)
    wu = _pad_ff_halves(w_up[0], BF)
    cw = _pad_ff_halves(jnp.pad(conv_w[0], ((0, SUBLANE - CONV_WIDTH), (0, 0))), F32)
    cb = _pad_ff_halves(conv_b, F32)
    wd = jnp.concatenate([w_down[0].astype(BF), jnp.zeros((D_FF_PAD - D_FF, D_MODEL), BF)], axis=0)

    pos = jnp.arange(seq, dtype=F32)
    inv_freq = ROPE_THETA ** (-jnp.arange(0, QK_ROPE_DIM, 2, dtype=F32) / QK_ROPE_DIM)
    ang = pos[:, None] * inv_freq[None, :]
    cs = jnp.concatenate([jnp.cos(ang), jnp.cos(ang), jnp.sin(ang), jnp.sin(ang)], axis=1)
    slopes = 2.0 ** (-ALIBI_MAX_BIAS * jnp.arange(1, DIL_HEADS + 1, dtype=F32) / DIL_HEADS)

    q, k, v = _mla_prep(x2, attn_norm_g, wlat, q_norm_g, kv_norm_g, wq, wkv, cs, seq)
    o_mla = _mla_flash(q, k, v, batch, seq)
    qkv_dil = _dil_proj(x2, attn_norm_g, wdil, batch, seq)
    o_dil, lse_dil = [], []
    for gi, (_, dil) in enumerate(DIL_PATTERNS):
        o_g, lse_g = _dilated(qkv_dil[gi], slopes, gi, batch, seq)
        nb = seq // dil // DIL_BLOCK
        o_dil.append(o_g)
        lse_g = lse_g.reshape(DIL_HEADS_PER_GROUP, batch, dil, nb, DIL_BLOCK)
        lse_dil.append(lse_g.transpose(1, 3, 4, 2, 0).reshape(t, DIL_HEADS_PER_GROUP))
    lse = jnp.concatenate(lse_dil, axis=1)
    x1 = _merge(x2, attn_norm_g, wgate, b_gate, o_mla, o_dil, lse,
                w_o_mla[0].astype(BF), w_o_dil[0].astype(BF), w_out[0].astype(BF), seq)

    y = _ffn(x1, ffn_norm_g, wu, cw, cb, wd, final_norm_g.reshape(1, D_MODEL), seq)
    return y.reshape(batch, seq, D_MODEL)
```

```python
import functools
import math

import numpy as np
import jax
import jax.numpy as jnp
from jax import lax
from jax.experimental import pallas as pl
from jax.experimental.pallas import tpu as pltpu

D_MODEL = 2048
MLA_HEADS = 8
QK_NOPE_DIM = 128
QK_ROPE_DIM = 64
V_HEAD_DIM = 128
Q_LORA_RANK = 512
KV_LORA_RANK = 256
ROPE_THETA = 10000.0
DIL_PATTERNS = ((128, 1), (512, 4), (2048, 16))
DIL_GROUPS = 3
DIL_HEADS_PER_GROUP = 4
DIL_HEADS = DIL_GROUPS * DIL_HEADS_PER_GROUP
DIL_HEAD_DIM = 128
DIL_BLOCK = 128
ALIBI_MAX_BIAS = 8.0
D_FF = 5504
CONV_WIDTH = 3
NORM_EPS = 1e-6

LANE = 128
SUBLANE = 8
VMEM_LIMIT = 56 * 1024 * 1024

LAT_DIM = Q_LORA_RANK + KV_LORA_RANK + 2 * QK_ROPE_DIM
QK_SLOT = 2 * LANE
V_SLOT = 2 * LANE
DIL_GROUP_DIM = 3 * DIL_HEADS_PER_GROUP * DIL_HEAD_DIM
FF_CHUNK = 256
FFN_TM = 1024
FFN_PIECE = 128
FFN_ROWS = 64
D_FF_PAD = -(-D_FF // FF_CHUNK) * FF_CHUNK
FF_CHUNKS = D_FF_PAD // FF_CHUNK
NEG = -1e30

TM = 512
TQ = 512
TK = 512
FLASH_HEADS = 2
DIL_BATCH = 8

BF = jnp.bfloat16
F32 = jnp.float32


def _params(*sem):
    return pltpu.CompilerParams(dimension_semantics=sem, vmem_limit_bytes=VMEM_LIMIT)


def _rms(x, g):
    return x * lax.rsqrt(jnp.mean(x * x, axis=-1, keepdims=True) + NORM_EPS) * g


def _dot(a, b):
    return jnp.dot(a, b, preferred_element_type=F32)


def _dot_nt(a, b):
    return lax.dot_general(a, b, (((1,), (1,)), ((), ())), preferred_element_type=F32)


def _rope_pair(t):
    lane = lax.broadcasted_iota(jnp.int32, t.shape, 1)
    return jnp.where(lane < QK_ROPE_DIM, t + pltpu.roll(t, QK_ROPE_DIM, 1), 0.0)


def _mla_prep_kernel(x_ref, g_ref, wlat_ref, qg_ref, kvg_ref, wq_ref, wkv_ref, cs_ref,
                     q_ref, k_ref, v_ref):
    h = _rms(x_ref[...], g_ref[...]).astype(BF)
    lat = _dot(h, wlat_ref[...])
    cq = _rms(lat[:, :Q_LORA_RANK], qg_ref[...]).astype(BF)
    ckv = _rms(lat[:, Q_LORA_RANK:Q_LORA_RANK + KV_LORA_RANK], kvg_ref[...]).astype(BF)
    q = _dot(cq, wq_ref[...])
    kv = _dot(ckv, wkv_ref[...])
    cs = cs_ref[...]
    scale = (QK_NOPE_DIM + QK_ROPE_DIM) ** -0.5 * math.log2(math.e)
    k_rope = _rope_pair(lat[:, Q_LORA_RANK + KV_LORA_RANK:] * cs).astype(BF)
    ones = jnp.ones((x_ref.shape[0], V_SLOT - V_HEAD_DIM), BF)
    for hd in range(MLA_HEADS):
        qs = q[:, hd * QK_SLOT:(hd + 1) * QK_SLOT]
        q_ref[hd, :, :LANE] = (qs[:, :LANE] * scale).astype(BF)
        q_ref[hd, :, LANE:] = _rope_pair(qs[:, LANE:] * (cs * scale)).astype(BF)
        k_ref[hd, :, :LANE] = kv[:, hd * 256:hd * 256 + LANE].astype(BF)
        k_ref[hd, :, LANE:] = k_rope
        v_ref[hd, :, :V_HEAD_DIM] = kv[:, hd * 256 + LANE:(hd + 1) * 256].astype(BF)
        v_ref[hd, :, V_HEAD_DIM:] = ones


def _mla_prep(x2, g, wlat, qg, kvg, wq, wkv, cs, seq):
    t = x2.shape[0]
    nseq = seq // TM
    const = lambda i: (0, 0)
    return pl.pallas_call(
        _mla_prep_kernel,
        grid=(t // TM,),
        in_specs=[
            pl.BlockSpec((TM, D_MODEL), lambda i: (i, 0)),
            pl.BlockSpec((1, D_MODEL), const),
            pl.BlockSpec((D_MODEL, LAT_DIM), const),
            pl.BlockSpec((1, Q_LORA_RANK), const),
            pl.BlockSpec((1, KV_LORA_RANK), const),
            pl.BlockSpec((Q_LORA_RANK, MLA_HEADS * QK_SLOT), const),
            pl.BlockSpec((KV_LORA_RANK, MLA_HEADS * 256), const),
            pl.BlockSpec((TM, LANE), lambda i: (i % nseq, 0)),
        ],
        out_specs=[
            pl.BlockSpec((MLA_HEADS, TM, QK_SLOT), lambda i: (0, i, 0)),
            pl.BlockSpec((MLA_HEADS, TM, QK_SLOT), lambda i: (0, i, 0)),
            pl.BlockSpec((MLA_HEADS, TM, V_SLOT), lambda i: (0, i, 0)),
        ],
        out_shape=[
            jax.ShapeDtypeStruct((MLA_HEADS, t, QK_SLOT), BF),
            jax.ShapeDtypeStruct((MLA_HEADS, t, QK_SLOT), BF),
            jax.ShapeDtypeStruct((MLA_HEADS, t, V_SLOT), BF),
        ],
        compiler_params=_params("arbitrary"),
        name="mla_prep",
    )(x2, g, wlat, qg, kvg, wq, wkv, cs)


def _dil_proj_kernel(x_ref, g_ref, w_ref, *rest):
    o_refs, h_ref, res_ref = rest[:DIL_GROUPS], rest[DIL_GROUPS], rest[DIL_GROUPS + 1]
    nqkv = 3 * DIL_HEADS_PER_GROUP
    h_ref[...] = _rms(x_ref[...], g_ref[...]).astype(BF)
    for gi, (_, dil) in enumerate(DIL_PATTERNS):
        sub = TM // dil
        res = _dot(h_ref[...], w_ref[:, gi * DIL_GROUP_DIM:(gi + 1) * DIL_GROUP_DIM])
        for hh in range(nqkv):
            blk = res[:, hh * DIL_HEAD_DIM:(hh + 1) * DIL_HEAD_DIM]
            if hh < DIL_HEADS_PER_GROUP:
                blk = blk * DIL_HEAD_DIM ** -0.5
            if dil == 1:
                o_refs[gi][hh, 0, 0] = blk.astype(BF)
            else:
                res_ref[hh] = blk
        if dil > 1:
            for hh in range(nqkv):
                for r in range(dil):
                    o_refs[gi][hh, 0, r] = res_ref[hh, pl.ds(r, sub, stride=dil), :].astype(BF)


def _dil_proj(x2, g, w, batch, seq):
    nseq = seq // TM
    nqkv = 3 * DIL_HEADS_PER_GROUP
    const = lambda i: (0, 0)
    return pl.pallas_call(
        _dil_proj_kernel,
        grid=(batch * nseq,),
        in_specs=[
            pl.BlockSpec((TM, D_MODEL), lambda i: (i, 0)),
            pl.BlockSpec((1, D_MODEL), const),
            pl.BlockSpec((D_MODEL, DIL_GROUPS * DIL_GROUP_DIM), const, pipeline_mode=pl.Buffered(1)),
        ],
        out_specs=[
            pl.BlockSpec((nqkv, 1, dil, TM // dil, DIL_HEAD_DIM), lambda i: (0, i // nseq, 0, i % nseq, 0))
            for _, dil in DIL_PATTERNS
        ],
        out_shape=[
            jax.ShapeDtypeStruct((nqkv, batch, dil, seq // dil, DIL_HEAD_DIM), BF)
            for _, dil in DIL_PATTERNS
        ],
        scratch_shapes=[pltpu.VMEM((TM, D_MODEL), BF), pltpu.VMEM((nqkv, TM, DIL_HEAD_DIM), F32)],
        compiler_params=_params("arbitrary"),
        name="dil_proj",
    )(x2, g, w)


def _flash_kernel(q_ref, k_ref, v_ref, o_ref, *scratch):
    nh = FLASH_HEADS
    m_refs, acc_refs = scratch[:nh], scratch[nh:2 * nh]
    s_refs = (scratch[2 * nh:3 * nh], scratch[3 * nh:4 * nh])
    qi = pl.program_id(2)
    for hh in range(nh):
        m_refs[hh][...] = jnp.full(m_refs[hh].shape, NEG, F32)
        acc_refs[hh][...] = jnp.zeros(acc_refs[hh].shape, F32)

    def key_rows(j):
        return pl.ds(pl.multiple_of(j * TK, TK), TK)

    def scores(j, slot):
        for hh in range(nh):
            s_refs[slot][hh][...] = _dot_nt(q_ref[hh], k_ref[hh, key_rows(j), :])

    def mask_diagonal(slot):
        keep = lax.broadcasted_iota(jnp.int32, (TQ, TK), 1) <= lax.broadcasted_iota(jnp.int32, (TQ, TK), 0)
        for hh in range(nh):
            s_refs[slot][hh][...] = jnp.where(keep, s_refs[slot][hh][...], NEG)

    def update(j, slot):
        for hh in range(nh):
            s = s_refs[slot][hh][...]
            m_old = m_refs[hh][...]
            m_new = jnp.maximum(m_old, jnp.max(s, axis=-1, keepdims=True))
            p = jnp.exp2(s - m_new)
            pv = _dot(p.astype(BF), v_ref[hh, key_rows(j), :])
            acc_refs[hh][...] = jnp.exp2(m_old - m_new) * acc_refs[hh][...] + pv
            m_refs[hh][...] = m_new

    scores(0, 0)

    @pl.when(qi == 0)
    def _():
        mask_diagonal(0)

    def body(j, carry):
        for slot in range(2):
            @pl.when(j % 2 == slot)
            def _(slot=slot):
                scores(j, slot)
                update(j - 1, 1 - slot)

            @pl.when((j % 2 == slot) & (j == qi))
            def _(slot=slot):
                mask_diagonal(slot)
        return carry

    lax.fori_loop(1, qi + 1, body, 0)

    for slot in range(2):
        @pl.when(qi % 2 == slot)
        def _(slot=slot):
            update(qi, slot)

    for hh in range(FLASH_HEADS):
        acc = acc_refs[hh][...]
        o_ref[hh] = (acc[:, :V_HEAD_DIM] * (1.0 / acc[:, V_HEAD_DIM:])).astype(BF)


def _mla_flash(q, k, v, batch, seq):
    nq = seq // TQ
    t = batch * seq
    hb = FLASH_HEADS
    return pl.pallas_call(
        _flash_kernel,
        grid=(batch, MLA_HEADS // hb, nq),
        in_specs=[
            pl.BlockSpec((hb, TQ, QK_SLOT), lambda b, h, i: (h, b * nq + i, 0)),
            pl.BlockSpec((hb, seq, QK_SLOT), lambda b, h, i: (h, b, 0)),
            pl.BlockSpec((hb, seq, V_SLOT), lambda b, h, i: (h, b, 0)),
        ],
        out_specs=pl.BlockSpec((hb, TQ, V_HEAD_DIM), lambda b, h, i: (h, b * nq + i, 0)),
        out_shape=jax.ShapeDtypeStruct((MLA_HEADS, t, V_HEAD_DIM), BF),
        scratch_shapes=([pltpu.VMEM((TQ, 1), F32)] * hb + [pltpu.VMEM((TQ, V_SLOT), F32)] * hb
                        + [pltpu.VMEM((TQ, TK), F32)] * (2 * hb)),
        compiler_params=_params("arbitrary", "arbitrary", "arbitrary"),
        name="mla_flash",
    )(q, k, v)


def _dilated_kernel(slope_ref, q_ref, k_ref, v_ref, o_ref, lse_ref, *, group, dil, nb):
    hs = pl.program_id(1)
    slope = slope_ref[group * DIL_HEADS_PER_GROUP + hs] * float(dil)
    qi = lax.broadcasted_iota(jnp.int32, (DIL_BLOCK, DIL_BLOCK), 0)
    ki = lax.broadcasted_iota(jnp.int32, (DIL_BLOCK, DIL_BLOCK), 1)
    dist_cur = (qi - ki).astype(F32)
    dist_prev = dist_cur + float(DIL_BLOCK)
    w_sub = float(DIL_PATTERNS[group][0] // dil)
    bias_cur = jnp.where((dist_cur >= 0.0) & (dist_cur <= w_sub), -slope * dist_cur, NEG)
    bias_prev = jnp.where(dist_prev <= w_sub, -slope * dist_prev, NEG)
    neg = jnp.full((DIL_BLOCK, DIL_BLOCK), NEG, F32)
    bias_inner = jnp.concatenate([bias_prev, bias_cur], axis=1)
    bias_first = jnp.concatenate([bias_cur, neg], axis=1)
    eye = qi == ki

    def window(n):
        start = max(n - 1, 0) * DIL_BLOCK
        return slice(start, start + 2 * DIL_BLOCK)

    blocks = [(r, n) for r in range(dil) for n in range(nb)]
    for g0 in range(0, len(blocks), DIL_BATCH):
        batch = blocks[g0:g0 + DIL_BATCH]
        scores = [_dot_nt(q_ref[0, 0, r, n * DIL_BLOCK:(n + 1) * DIL_BLOCK, :], k_ref[0, 0, r, window(n), :])
                  + (bias_first if n == 0 else bias_inner) for r, n in batch]
        maxes = [jnp.max(s, axis=-1, keepdims=True) for s in scores]
        probs = [jnp.exp(s - m) for s, m in zip(scores, maxes)]
        sums = [jnp.sum(p, axis=-1, keepdims=True) for p in probs]
        outs = [_dot(p.astype(BF), v_ref[0, 0, r, window(n), :]) for p, (r, n) in zip(probs, batch)]
        for (r, n), o, l in zip(batch, outs, sums):
            o_ref[0, 0, r, n * DIL_BLOCK:(n + 1) * DIL_BLOCK, :] = (o * (1.0 / l)).astype(BF)
        rows = [jnp.sum(jnp.where(eye, m + jnp.log(l), 0.0), axis=0, keepdims=True) for m, l in zip(maxes, sums)]
        lse_ref[0, 0, g0:g0 + DIL_BATCH, :] = jnp.concatenate(rows, axis=0)


def _dilated(qkv, slopes, group, batch, seq):
    _, dil = DIL_PATTERNS[group]
    sub = seq // dil
    nb = sub // DIL_BLOCK
    hg = DIL_HEADS_PER_GROUP
    blk = (1, 1, dil, sub, DIL_HEAD_DIM)

    def head(which):
        return lambda b, h: (which * hg + h, b, 0, 0, 0)

    return pl.pallas_call(
        functools.partial(_dilated_kernel, group=group, dil=dil, nb=nb),
        grid=(batch, hg),
        in_specs=[
            pl.BlockSpec(memory_space=pltpu.SMEM),
            pl.BlockSpec(blk, head(0)),
            pl.BlockSpec(blk, head(1)),
            pl.BlockSpec(blk, head(2)),
        ],
        out_specs=[
            pl.BlockSpec(blk, head(0)),
            pl.BlockSpec((1, 1, dil * nb, DIL_BLOCK), lambda b, h: (h, b, 0, 0)),
        ],
        out_shape=[
            jax.ShapeDtypeStruct((hg, batch, dil, sub, DIL_HEAD_DIM), BF),
            jax.ShapeDtypeStruct((hg, batch, dil * nb, DIL_BLOCK), F32),
        ],
        compiler_params=_params("arbitrary", "arbitrary"),
        name=f"dilated_d{dil}",
    )(slopes, qkv, qkv, qkv)


MERGE_TN = 512
MERGE_TM = 256


def _merge_kernel(x_ref, g_ref, wg_ref, bg_ref, oa_ref, o1_ref, o4_ref, o16_ref, lse_ref,
                  woa_ref, wob_ref, wout_ref, y_ref, h_ref, a_ref, b_ref, mg_ref, nat_ref):
    x = x_ref[...]
    tm = x.shape[0]
    h_ref[...] = _rms(x, g_ref[...]).astype(BF)
    for hd in range(MLA_HEADS):
        a_ref[:, hd * V_HEAD_DIM:(hd + 1) * V_HEAD_DIM] = oa_ref[hd]
    lse = lse_ref[...]
    o_refs = (o1_ref, o4_ref, o16_ref)
    for hs in range(DIL_HEADS_PER_GROUP):
        ls = [lse[:, gi * DIL_HEADS_PER_GROUP + hs:gi * DIL_HEADS_PER_GROUP + hs + 1] for gi in range(DIL_GROUPS)]
        mx = jnp.maximum(jnp.maximum(ls[0], ls[1]), ls[2])
        es = [jnp.exp(v - mx) for v in ls]
        inv = 1.0 / (es[0] + es[1] + es[2])
        comb = (es[0] * inv) * o_refs[0][hs, 0, 0].astype(F32)
        for gi in range(1, DIL_GROUPS):
            dil = DIL_PATTERNS[gi][1]
            for r in range(dil):
                nat_ref[pl.ds(r, tm // dil, stride=dil), :] = o_refs[gi][hs, 0, r].astype(F32)
            comb = comb + (es[gi] * inv) * nat_ref[...]
        b_ref[:, hs * DIL_HEAD_DIM:(hs + 1) * DIL_HEAD_DIM] = comb.astype(BF)
    for c in range(D_MODEL // MERGE_TN):
        cols = slice(c * MERGE_TN, (c + 1) * MERGE_TN)
        cols_b = slice(D_MODEL + c * MERGE_TN, D_MODEL + (c + 1) * MERGE_TN)
        gate_a = jax.nn.sigmoid(_dot(h_ref[...], wg_ref[:, cols]) + bg_ref[:, cols])
        gate_b = jax.nn.sigmoid(_dot(h_ref[...], wg_ref[:, cols_b]) + bg_ref[:, cols_b])
        o_a = _dot(a_ref[...], woa_ref[:, cols])
        o_b = _dot(b_ref[...], wob_ref[:, cols])
        mg_ref[:, cols] = (gate_a * o_a + gate_b * o_b).astype(BF)
    y_ref[...] = x + _dot(mg_ref[...], wout_ref[...])


def _merge(x2, g, wg, bg, oa, o_dil, lse, woa, wob, wout, seq):
    t = x2.shape[0]
    tm = MERGE_TM
    nseq = seq // tm
    const = lambda i: (0, 0)
    one = pl.Buffered(1)
    dil_specs = [
        pl.BlockSpec((DIL_HEADS_PER_GROUP, 1, dil, tm // dil, DIL_HEAD_DIM),
                     lambda i: (0, i // nseq, 0, i % nseq, 0))
        for _, dil in DIL_PATTERNS
    ]
    return pl.pallas_call(
        _merge_kernel,
        grid=(t // tm,),
        in_specs=[
            pl.BlockSpec((tm, D_MODEL), lambda i: (i, 0)),
            pl.BlockSpec((1, D_MODEL), const),
            pl.BlockSpec((D_MODEL, 2 * D_MODEL), const, pipeline_mode=one),
            pl.BlockSpec((1, 2 * D_MODEL), const),
            pl.BlockSpec((MLA_HEADS, tm, V_HEAD_DIM), lambda i: (0, i, 0)),
            *dil_specs,
            pl.BlockSpec((tm, DIL_HEADS), lambda i: (i, 0)),
            pl.BlockSpec((MLA_HEADS * V_HEAD_DIM, D_MODEL), const, pipeline_mode=one),
            pl.BlockSpec((DIL_HEADS_PER_GROUP * DIL_HEAD_DIM, D_MODEL), const, pipeline_mode=one),
            pl.BlockSpec((D_MODEL, D_MODEL), const, pipeline_mode=one),
        ],
        out_specs=pl.BlockSpec((tm, D_MODEL), lambda i: (i, 0)),
        out_shape=jax.ShapeDtypeStruct((t, D_MODEL), F32),
        scratch_shapes=[
            pltpu.VMEM((tm, D_MODEL), BF),
            pltpu.VMEM((tm, MLA_HEADS * V_HEAD_DIM), BF),
            pltpu.VMEM((tm, DIL_HEADS_PER_GROUP * DIL_HEAD_DIM), BF),
            pltpu.VMEM((tm, D_MODEL), BF),
            pltpu.VMEM((tm, DIL_HEAD_DIM), F32),
        ],
        compiler_params=_params("arbitrary"),
        name="merge",
    )(x2, g, wg, bg, oa, *o_dil, lse, woa, wob, wout)


def _ffn_kernel(x_ref, g_ref, wup_ref, wgt_ref, cwu_ref, cwg_ref, cbu_ref, cbg_ref, wd_ref, fg_ref, y_ref,
                h_ref, u0_ref, u1_ref, a0_ref, a1_ref, carry_ref, *, tiles_per_seq):
    i = pl.program_id(0)
    c = pl.program_id(1)
    tm = x_ref.shape[0]
    u_refs = (u0_ref, u1_ref)
    a_refs = (a0_ref, a1_ref)
    seq_start = (i % tiles_per_seq) == 0

    @pl.when((i == 0) & (c == 0))
    def _():
        carry_ref[...] = jnp.zeros(carry_ref.shape, F32)

    def project(slot, r0):
        u_ref = u_refs[slot]
        rows = slice(SUBLANE + r0, SUBLANE + r0 + FFN_PIECE)
        h = h_ref[r0:r0 + FFN_PIECE, :]
        u_ref[rows, :FF_CHUNK] = _dot(h, wup_ref[...])
        u_ref[rows, FF_CHUNK:] = _dot(h, wgt_ref[...])

    def activate(slot, r0):
        u_ref = u_refs[slot]

        def conv(r1, cols, cw_ref, cb_ref):
            cw = cw_ref[...]
            out = cb_ref[...]
            for tap in range(CONV_WIDTH):
                off = r1 + SUBLANE - (CONV_WIDTH - 1) + tap
                out = out + cw[tap:tap + 1] * u_ref[off:off + FFN_ROWS, cols]
            return out

        for r1 in range(r0, r0 + FFN_PIECE, FFN_ROWS):
            up = conv(r1, slice(0, FF_CHUNK), cwu_ref, cbu_ref)
            gate = conv(r1, slice(FF_CHUNK, 2 * FF_CHUNK), cwg_ref, cbg_ref)
            a_refs[slot][r1:r1 + FFN_ROWS, :] = (jax.nn.silu(gate) * up).astype(BF)

    def down(slot, r0):
        rows = slice(r0, r0 + FFN_PIECE)
        y_ref[rows, :] += _dot(a_refs[slot][rows, :], wd_ref[...])

    def run(project_slot=None, activate_slot=None, down_slot=None):
        if project_slot is not None:
            u_refs[project_slot][:SUBLANE] = jnp.where(seq_start, 0.0, carry_ref[c])
        for r0 in range(0, tm, FFN_PIECE):
            if down_slot is not None:
                down(down_slot, r0)
            if project_slot is not None:
                project(project_slot, r0)
            if activate_slot is not None:
                activate(activate_slot, r0)
        if project_slot is not None:
            carry_ref[c] = u_refs[project_slot][tm:]

    @pl.when(c == 0)
    def _():
        h_ref[...] = _rms(x_ref[...], g_ref[...]).astype(BF)
        y_ref[...] = jnp.zeros(y_ref.shape, F32)
        run(project_slot=0)

    @pl.when(c == 1)
    def _():
        run(project_slot=1, activate_slot=0)

    for slot in range(2):
        @pl.when((c > 1) & (c < FF_CHUNKS) & (c % 2 == slot))
        def _(slot=slot):
            run(project_slot=slot, activate_slot=1 - slot, down_slot=slot)

    @pl.when(c == FF_CHUNKS)
    def _():
        run(activate_slot=(FF_CHUNKS - 1) % 2, down_slot=FF_CHUNKS % 2)

    @pl.when(c == FF_CHUNKS + 1)
    def _():
        run(down_slot=(FF_CHUNKS - 1) % 2)
        y_ref[...] = _rms(x_ref[...] + y_ref[...], fg_ref[...])


def _ffn(x1, g, wu, cw, cb, wd, fg, seq):
    t = x1.shape[0]
    tm = FFN_TM
    chunk = lambda c, lag: jnp.clip(c - lag, 0, FF_CHUNKS - 1)
    up_blk = lambda i, c: (0, chunk(c, 0))
    gate_blk = lambda i, c: (0, FF_CHUNKS + chunk(c, 0))
    up_prev = lambda i, c: (0, chunk(c, 1))
    gate_prev = lambda i, c: (0, FF_CHUNKS + chunk(c, 1))
    return pl.pallas_call(
        functools.partial(_ffn_kernel, tiles_per_seq=seq // tm),
        grid=(t // tm, FF_CHUNKS + 2),
        in_specs=[
            pl.BlockSpec((tm, D_MODEL), lambda i, c: (i, 0)),
            pl.BlockSpec((1, D_MODEL), lambda i, c: (0, 0)),
            pl.BlockSpec((D_MODEL, FF_CHUNK), up_blk),
            pl.BlockSpec((D_MODEL, FF_CHUNK), gate_blk),
            pl.BlockSpec((SUBLANE, FF_CHUNK), up_prev),
            pl.BlockSpec((SUBLANE, FF_CHUNK), gate_prev),
            pl.BlockSpec((1, FF_CHUNK), up_prev),
            pl.BlockSpec((1, FF_CHUNK), gate_prev),
            pl.BlockSpec((FF_CHUNK, D_MODEL), lambda i, c: (chunk(c, 2), 0)),
            pl.BlockSpec((1, D_MODEL), lambda i, c: (0, 0)),
        ],
        out_specs=pl.BlockSpec((tm, D_MODEL), lambda i, c: (i, 0)),
        out_shape=jax.ShapeDtypeStruct((t, D_MODEL), F32),
        scratch_shapes=[
            pltpu.VMEM((tm, D_MODEL), BF),
            pltpu.VMEM((tm + SUBLANE, 2 * FF_CHUNK), F32),
            pltpu.VMEM((tm + SUBLANE, 2 * FF_CHUNK), F32),
            pltpu.VMEM((tm, FF_CHUNK), BF),
            pltpu.VMEM((tm, FF_CHUNK), BF),
            pltpu.VMEM((FF_CHUNKS, SUBLANE, 2 * FF_CHUNK), F32),
        ],
        compiler_params=_params("arbitrary", "arbitrary"),
        name="ffn",
    )(x1, g, wu, wu, cw, cw, cb, cb, wd, fg)


def _rotate_half_cols(w):
    half = w.shape[-1] // 2
    return jnp.concatenate([-w[..., half:], w[..., :half]], axis=-1)


def _pad_ff_halves(a, dtype):
    zeros = jnp.zeros((a.shape[0], D_FF_PAD - D_FF), dtype)
    return jnp.concatenate([a[:, :D_FF].astype(dtype), zeros, a[:, D_FF:].astype(dtype), zeros], axis=1)


def kernel(x, attn_norm_g, w_in, b_gate, q_norm_g, w_uq, kv_norm_g, w_ukv, w_o_mla, w_o_dil,
           w_out, ffn_norm_g, w_up, conv_w, conv_b, w_down, final_norm_g):
    batch, seq, _ = x.shape
    assert w_in.shape[0] == 1, "single-layer block"
    assert seq % TM == 0 and seq % FFN_TM == 0 and seq % TQ == 0 and TQ == TK and seq % (16 * DIL_BLOCK) == 0
    assert FF_CHUNKS >= 3
    t = batch * seq
    x2 = x.reshape(t, D_MODEL)

    dqkv = DIL_HEADS * DIL_HEAD_DIM
    o = np.cumsum((0, Q_LORA_RANK, KV_LORA_RANK, QK_ROPE_DIM, dqkv, dqkv, dqkv, D_MODEL, D_MODEL))
    wi = w_in[0]
    w_kpe = wi[:, o[2]:o[3]]
    wlat = jnp.concatenate([wi[:, o[0]:o[2]], w_kpe, _rotate_half_cols(w_kpe)], axis=1).astype(BF)
    gw = DIL_HEADS_PER_GROUP * DIL_HEAD_DIM
    wdil = jnp.concatenate(
        [wi[:, o[3 + which] + gi * gw:o[3 + which] + (gi + 1) * gw].astype(BF)
         for gi in range(DIL_GROUPS) for which in range(3)], axis=1)
    wgate = wi[:, o[6]:o[8]].astype(BF)
    wq = w_uq[0].astype(BF).reshape(Q_LORA_RANK, MLA_HEADS, QK_NOPE_DIM + QK_ROPE_DIM)
    wq_pe = wq[..., QK_NOPE_DIM:]
    wq = jnp.concatenate([wq[..., :QK_NOPE_DIM], wq_pe, _rotate_half_cols(wq_pe)], axis=-1)
    wq = wq.reshape(Q_LORA_RANK, MLA_HEADS * QK_SLOT)
    wkv = w_ukv[0].astype(BF)
    wu = _pad_ff_halves(w_up[0], BF)
    cw = _pad_ff_halves(jnp.pad(conv_w[0], ((0, SUBLANE - CONV_WIDTH), (0, 0))), F32)
    cb = _pad_ff_halves(conv_b, F32)
    wd = jnp.concatenate([w_down[0].astype(BF), jnp.zeros((D_FF_PAD - D_FF, D_MODEL), BF)], axis=0)

    pos = jnp.arange(seq, dtype=F32)
    inv_freq = ROPE_THETA ** (-jnp.arange(0, QK_ROPE_DIM, 2, dtype=F32) / QK_ROPE_DIM)
    ang = pos[:, None] * inv_freq[None, :]
    cs = jnp.concatenate([jnp.cos(ang), jnp.cos(ang), jnp.sin(ang), jnp.sin(ang)], axis=1)
    slopes = 2.0 ** (-ALIBI_MAX_BIAS * jnp.arange(1, DIL_HEADS + 1, dtype=F32) / DIL_HEADS)

    q, k, v = _mla_prep(x2, attn_norm_g, wlat, q_norm_g, kv_norm_g, wq, wkv, cs, seq)
    o_mla = _mla_flash(q, k, v, batch, seq)
    qkv_dil = _dil_proj(x2, attn_norm_g, wdil, batch, seq)
    o_dil, lse_dil = [], []
    for gi, (_, dil) in enumerate(DIL_PATTERNS):
        o_g, lse_g = _dilated(qkv_dil[gi], slopes, gi, batch, seq)
        nb = seq // dil // DIL_BLOCK
        o_dil.append(o_g)
        lse_g = lse_g.reshape(DIL_HEADS_PER_GROUP, batch, dil, nb, DIL_BLOCK)
        lse_dil.append(lse_g.transpose(1, 3, 4, 2, 0).reshape(t, DIL_HEADS_PER_GROUP))
    lse = jnp.concatenate(lse_dil, axis=1)
    x1 = _merge(x2, attn_norm_g, wgate, b_gate, o_mla, o_dil, lse,
                w_o_mla[0].astype(BF), w_o_dil[0].astype(BF), w_out[0].astype(BF), seq)

    y = _ffn(x1, ffn_norm_g, wu, cw, cb, wd, final_norm_g.reshape(1, D_MODEL), seq)
    return y.reshape(batch, seq, D_MODEL)
```

```python
import functools
import math

import numpy as np
import jax
import jax.numpy as jnp
from jax import lax
from jax.experimental import pallas as pl
from jax.experimental.pallas import tpu as pltpu

D_MODEL = 2048
MLA_HEADS = 8
QK_NOPE_DIM = 128
QK_ROPE_DIM = 64
V_HEAD_DIM = 128
Q_LORA_RANK = 512
KV_LORA_RANK = 256
ROPE_THETA = 10000.0
DIL_PATTERNS = ((128, 1), (512, 4), (2048, 16))
DIL_GROUPS = 3
DIL_HEADS_PER_GROUP = 4
DIL_HEADS = DIL_GROUPS * DIL_HEADS_PER_GROUP
DIL_HEAD_DIM = 128
DIL_BLOCK = 128
ALIBI_MAX_BIAS = 8.0
D_FF = 5504
CONV_WIDTH = 3
NORM_EPS = 1e-6

LANE = 128
SUBLANE = 8
VMEM_LIMIT = 56 * 1024 * 1024

LAT_DIM = Q_LORA_RANK + KV_LORA_RANK + 2 * QK_ROPE_DIM
QK_SLOT = 2 * LANE
V_SLOT = 2 * LANE
DIL_GROUP_DIM = 3 * DIL_HEADS_PER_GROUP * DIL_HEAD_DIM
FF_CHUNK = 512
FFN_TM = 1024
FFN_DOWN_TM = 256
FFN_PIECE = 128
FFN_NSPLIT = 256
FFN_ROWS = 32
D_FF_PAD = -(-D_FF // FF_CHUNK) * FF_CHUNK
FF_CHUNKS = D_FF_PAD // FF_CHUNK
NEG = -1e30

TM = 512
TQ = 512
TK = 512
FLASH_HEADS = 2
DIL_BATCH = 8

BF = jnp.bfloat16
F32 = jnp.float32


def _params(*sem):
    return pltpu.CompilerParams(dimension_semantics=sem, vmem_limit_bytes=VMEM_LIMIT)


def _rms(x, g):
    return x * lax.rsqrt(jnp.mean(x * x, axis=-1, keepdims=True) + NORM_EPS) * g


def _dot(a, b):
    return jnp.dot(a, b, preferred_element_type=F32)


def _dot_nt(a, b):
    return lax.dot_general(a, b, (((1,), (1,)), ((), ())), preferred_element_type=F32)


def _rope_pair(t):
    lane = lax.broadcasted_iota(jnp.int32, t.shape, 1)
    return jnp.where(lane < QK_ROPE_DIM, t + pltpu.roll(t, QK_ROPE_DIM, 1), 0.0)


def _mla_prep_kernel(x_ref, g_ref, wlat_ref, qg_ref, kvg_ref, wq_ref, wkv_ref, cs_ref,
                     q_ref, k_ref, v_ref):
    h = _rms(x_ref[...], g_ref[...]).astype(BF)
    lat = _dot(h, wlat_ref[...])
    cq = _rms(lat[:, :Q_LORA_RANK], qg_ref[...]).astype(BF)
    ckv = _rms(lat[:, Q_LORA_RANK:Q_LORA_RANK + KV_LORA_RANK], kvg_ref[...]).astype(BF)
    q = _dot(cq, wq_ref[...])
    kv = _dot(ckv, wkv_ref[...])
    cs = cs_ref[...]
    scale = (QK_NOPE_DIM + QK_ROPE_DIM) ** -0.5 * math.log2(math.e)
    k_rope = _rope_pair(lat[:, Q_LORA_RANK + KV_LORA_RANK:] * cs).astype(BF)
    ones = jnp.ones((x_ref.shape[0], V_SLOT - V_HEAD_DIM), BF)
    for hd in range(MLA_HEADS):
        qs = q[:, hd * QK_SLOT:(hd + 1) * QK_SLOT]
        q_ref[hd, :, :LANE] = (qs[:, :LANE] * scale).astype(BF)
        q_ref[hd, :, LANE:] = _rope_pair(qs[:, LANE:] * (cs * scale)).astype(BF)
        k_ref[hd, :, :LANE] = kv[:, hd * 256:hd * 256 + LANE].astype(BF)
        k_ref[hd, :, LANE:] = k_rope
        v_ref[hd, :, :V_HEAD_DIM] = kv[:, hd * 256 + LANE:(hd + 1) * 256].astype(BF)
        v_ref[hd, :, V_HEAD_DIM:] = ones


def _mla_prep(x2, g, wlat, qg, kvg, wq, wkv, cs, seq):
    t = x2.shape[0]
    nseq = seq // TM
    const = lambda i: (0, 0)
    return pl.pallas_call(
        _mla_prep_kernel,
        grid=(t // TM,),
        in_specs=[
            pl.BlockSpec((TM, D_MODEL), lambda i: (i, 0)),
            pl.BlockSpec((1, D_MODEL), const),
            pl.BlockSpec((D_MODEL, LAT_DIM), const),
            pl.BlockSpec((1, Q_LORA_RANK), const),
            pl.BlockSpec((1, KV_LORA_RANK), const),
            pl.BlockSpec((Q_LORA_RANK, MLA_HEADS * QK_SLOT), const),
            pl.BlockSpec((KV_LORA_RANK, MLA_HEADS * 256), const),
            pl.BlockSpec((TM, LANE), lambda i: (i % nseq, 0)),
        ],
        out_specs=[
            pl.BlockSpec((MLA_HEADS, TM, QK_SLOT), lambda i: (0, i, 0)),
            pl.BlockSpec((MLA_HEADS, TM, QK_SLOT), lambda i: (0, i, 0)),
            pl.BlockSpec((MLA_HEADS, TM, V_SLOT), lambda i: (0, i, 0)),
        ],
        out_shape=[
            jax.ShapeDtypeStruct((MLA_HEADS, t, QK_SLOT), BF),
            jax.ShapeDtypeStruct((MLA_HEADS, t, QK_SLOT), BF),
            jax.ShapeDtypeStruct((MLA_HEADS, t, V_SLOT), BF),
        ],
        compiler_params=_params("arbitrary"),
        name="mla_prep",
    )(x2, g, wlat, qg, kvg, wq, wkv, cs)


def _dil_proj_kernel(x_ref, g_ref, w_ref, *rest):
    o_refs, h_ref, res_ref = rest[:DIL_GROUPS], rest[DIL_GROUPS], rest[DIL_GROUPS + 1]
    nqkv = 3 * DIL_HEADS_PER_GROUP
    h_ref[...] = _rms(x_ref[...], g_ref[...]).astype(BF)
    for gi, (_, dil) in enumerate(DIL_PATTERNS):
        sub = TM // dil
        res = _dot(h_ref[...], w_ref[:, gi * DIL_GROUP_DIM:(gi + 1) * DIL_GROUP_DIM])
        for hh in range(nqkv):
            blk = res[:, hh * DIL_HEAD_DIM:(hh + 1) * DIL_HEAD_DIM]
            if hh < DIL_HEADS_PER_GROUP:
                blk = blk * DIL_HEAD_DIM ** -0.5
            if dil == 1:
                o_refs[gi][hh, 0, 0] = blk.astype(BF)
            else:
                res_ref[hh] = blk
        if dil > 1:
            for hh in range(nqkv):
                for r in range(dil):
                    o_refs[gi][hh, 0, r] = res_ref[hh, pl.ds(r, sub, stride=dil), :].astype(BF)


def _dil_proj(x2, g, w, batch, seq):
    nseq = seq // TM
    nqkv = 3 * DIL_HEADS_PER_GROUP
    const = lambda i: (0, 0)
    return pl.pallas_call(
        _dil_proj_kernel,
        grid=(batch * nseq,),
        in_specs=[
            pl.BlockSpec((TM, D_MODEL), lambda i: (i, 0)),
            pl.BlockSpec((1, D_MODEL), const),
            pl.BlockSpec((D_MODEL, DIL_GROUPS * DIL_GROUP_DIM), const, pipeline_mode=pl.Buffered(1)),
        ],
        out_specs=[
            pl.BlockSpec((nqkv, 1, dil, TM // dil, DIL_HEAD_DIM), lambda i: (0, i // nseq, 0, i % nseq, 0))
            for _, dil in DIL_PATTERNS
        ],
        out_shape=[
            jax.ShapeDtypeStruct((nqkv, batch, dil, seq // dil, DIL_HEAD_DIM), BF)
            for _, dil in DIL_PATTERNS
        ],
        scratch_shapes=[pltpu.VMEM((TM, D_MODEL), BF), pltpu.VMEM((nqkv, TM, DIL_HEAD_DIM), F32)],
        compiler_params=_params("arbitrary"),
        name="dil_proj",
    )(x2, g, w)


def _flash_kernel(q_ref, k_ref, v_ref, o_ref, *scratch):
    nh = FLASH_HEADS
    m_refs, acc_refs = scratch[:nh], scratch[nh:2 * nh]
    s_refs = (scratch[2 * nh:3 * nh], scratch[3 * nh:4 * nh])
    qi = pl.program_id(2)
    for hh in range(nh):
        m_refs[hh][...] = jnp.full(m_refs[hh].shape, NEG, F32)
        acc_refs[hh][...] = jnp.zeros(acc_refs[hh].shape, F32)

    def key_rows(j):
        return pl.ds(pl.multiple_of(j * TK, TK), TK)

    def scores(j, slot):
        for hh in range(nh):
            s_refs[slot][hh][...] = _dot_nt(q_ref[hh], k_ref[hh, key_rows(j), :])

    def mask_diagonal(slot):
        keep = lax.broadcasted_iota(jnp.int32, (TQ, TK), 1) <= lax.broadcasted_iota(jnp.int32, (TQ, TK), 0)
        for hh in range(nh):
            s_refs[slot][hh][...] = jnp.where(keep, s_refs[slot][hh][...], NEG)

    def update(j, slot):
        for hh in range(nh):
            s = s_refs[slot][hh][...]
            m_old = m_refs[hh][...]
            m_new = jnp.maximum(m_old, jnp.max(s, axis=-1, keepdims=True))
            p = jnp.exp2(s - m_new)
            pv = _dot(p.astype(BF), v_ref[hh, key_rows(j), :])
            acc_refs[hh][...] = jnp.exp2(m_old - m_new) * acc_refs[hh][...] + pv
            m_refs[hh][...] = m_new

    scores(0, 0)

    @pl.when(qi == 0)
    def _():
        mask_diagonal(0)

    def body(j, carry):
        for slot in range(2):
            @pl.when(j % 2 == slot)
            def _(slot=slot):
                scores(j, slot)
                update(j - 1, 1 - slot)

            @pl.when((j % 2 == slot) & (j == qi))
            def _(slot=slot):
                mask_diagonal(slot)
        return carry

    lax.fori_loop(1, qi + 1, body, 0)

    for slot in range(2):
        @pl.when(qi % 2 == slot)
        def _(slot=slot):
            update(qi, slot)

    for hh in range(FLASH_HEADS):
        acc = acc_refs[hh][...]
        o_ref[hh] = (acc[:, :V_HEAD_DIM] * (1.0 / acc[:, V_HEAD_DIM:])).astype(BF)


def _mla_flash(q, k, v, batch, seq):
    nq = seq // TQ
    t = batch * seq
    hb = FLASH_HEADS
    return pl.pallas_call(
        _flash_kernel,
        grid=(batch, MLA_HEADS // hb, nq),
        in_specs=[
            pl.BlockSpec((hb, TQ, QK_SLOT), lambda b, h, i: (h, b * nq + i, 0)),
            pl.BlockSpec((hb, seq, QK_SLOT), lambda b, h, i: (h, b, 0)),
            pl.BlockSpec((hb, seq, V_SLOT), lambda b, h, i: (h, b, 0)),
        ],
        out_specs=pl.BlockSpec((hb, TQ, V_HEAD_DIM), lambda b, h, i: (h, b * nq + i, 0)),
        out_shape=jax.ShapeDtypeStruct((MLA_HEADS, t, V_HEAD_DIM), BF),
        scratch_shapes=([pltpu.VMEM((TQ, 1), F32)] * hb + [pltpu.VMEM((TQ, V_SLOT), F32)] * hb
                        + [pltpu.VMEM((TQ, TK), F32)] * (2 * hb)),
        compiler_params=_params("arbitrary", "arbitrary", "arbitrary"),
        name="mla_flash",
    )(q, k, v)


def _dilated_kernel(slope_ref, q_ref, k_ref, v_ref, o_ref, lse_ref, *, group, dil, nb):
    hs = pl.program_id(1)
    slope = slope_ref[group * DIL_HEADS_PER_GROUP + hs] * float(dil)
    qi = lax.broadcasted_iota(jnp.int32, (DIL_BLOCK, DIL_BLOCK), 0)
    ki = lax.broadcasted_iota(jnp.int32, (DIL_BLOCK, DIL_BLOCK), 1)
    dist_cur = (qi - ki).astype(F32)
    dist_prev = dist_cur + float(DIL_BLOCK)
    w_sub = float(DIL_PATTERNS[group][0] // dil)
    bias_cur = jnp.where((dist_cur >= 0.0) & (dist_cur <= w_sub), -slope * dist_cur, NEG)
    bias_prev = jnp.where(dist_prev <= w_sub, -slope * dist_prev, NEG)
    neg = jnp.full((DIL_BLOCK, DIL_BLOCK), NEG, F32)
    bias_inner = jnp.concatenate([bias_prev, bias_cur], axis=1)
    bias_first = jnp.concatenate([bias_cur, neg], axis=1)
    eye = qi == ki

    def window(n):
        start = max(n - 1, 0) * DIL_BLOCK
        return slice(start, start + 2 * DIL_BLOCK)

    blocks = [(r, n) for r in range(dil) for n in range(nb)]
    for g0 in range(0, len(blocks), DIL_BATCH):
        batch = blocks[g0:g0 + DIL_BATCH]
        scores = [_dot_nt(q_ref[0, 0, r, n * DIL_BLOCK:(n + 1) * DIL_BLOCK, :], k_ref[0, 0, r, window(n), :])
                  + (bias_first if n == 0 else bias_inner) for r, n in batch]
        maxes = [jnp.max(s, axis=-1, keepdims=True) for s in scores]
        probs = [jnp.exp(s - m) for s, m in zip(scores, maxes)]
        sums = [jnp.sum(p, axis=-1, keepdims=True) for p in probs]
        outs = [_dot(p.astype(BF), v_ref[0, 0, r, window(n), :]) for p, (r, n) in zip(probs, batch)]
        for (r, n), o, l in zip(batch, outs, sums):
            o_ref[0, 0, r, n * DIL_BLOCK:(n + 1) * DIL_BLOCK, :] = (o * (1.0 / l)).astype(BF)
        rows = [jnp.sum(jnp.where(eye, m + jnp.log(l), 0.0), axis=0, keepdims=True) for m, l in zip(maxes, sums)]
        lse_ref[0, 0, g0:g0 + DIL_BATCH, :] = jnp.concatenate(rows, axis=0)


def _dilated(qkv, slopes, group, batch, seq):
    _, dil = DIL_PATTERNS[group]
    sub = seq // dil
    nb = sub // DIL_BLOCK
    hg = DIL_HEADS_PER_GROUP
    blk = (1, 1, dil, sub, DIL_HEAD_DIM)

    def head(which):
        return lambda b, h: (which * hg + h, b, 0, 0, 0)

    return pl.pallas_call(
        functools.partial(_dilated_kernel, group=group, dil=dil, nb=nb),
        grid=(batch, hg),
        in_specs=[
            pl.BlockSpec(memory_space=pltpu.SMEM),
            pl.BlockSpec(blk, head(0)),
            pl.BlockSpec(blk, head(1)),
            pl.BlockSpec(blk, head(2)),
        ],
        out_specs=[
            pl.BlockSpec(blk, head(0)),
            pl.BlockSpec((1, 1, dil * nb, DIL_BLOCK), lambda b, h: (h, b, 0, 0)),
        ],
        out_shape=[
            jax.ShapeDtypeStruct((hg, batch, dil, sub, DIL_HEAD_DIM), BF),
            jax.ShapeDtypeStruct((hg, batch, dil * nb, DIL_BLOCK), F32),
        ],
        compiler_params=_params("arbitrary", "arbitrary"),
        name=f"dilated_d{dil}",
    )(slopes, qkv, qkv, qkv)


MERGE_TN = 512
MERGE_TM = 256


def _merge_kernel(x_ref, g_ref, wg_ref, bg_ref, oa_ref, o1_ref, o4_ref, o16_ref, lse_ref,
                  woa_ref, wob_ref, wout_ref, fg_ref, y_ref, h2_ref, h_ref, a_ref, b_ref, mg_ref, nat_ref):
    x = x_ref[...]
    tm = x.shape[0]
    h_ref[...] = _rms(x, g_ref[...]).astype(BF)
    for hd in range(MLA_HEADS):
        a_ref[:, hd * V_HEAD_DIM:(hd + 1) * V_HEAD_DIM] = oa_ref[hd]
    lse = lse_ref[...]
    o_refs = (o1_ref, o4_ref, o16_ref)
    for hs in range(DIL_HEADS_PER_GROUP):
        ls = [lse[:, gi * DIL_HEADS_PER_GROUP + hs:gi * DIL_HEADS_PER_GROUP + hs + 1] for gi in range(DIL_GROUPS)]
        mx = jnp.maximum(jnp.maximum(ls[0], ls[1]), ls[2])
        es = [jnp.exp(v - mx) for v in ls]
        inv = 1.0 / (es[0] + es[1] + es[2])
        comb = (es[0] * inv) * o_refs[0][hs, 0, 0].astype(F32)
        for gi in range(1, DIL_GROUPS):
            dil = DIL_PATTERNS[gi][1]
            for r in range(dil):
                nat_ref[pl.ds(r, tm // dil, stride=dil), :] = o_refs[gi][hs, 0, r].astype(F32)
            comb = comb + (es[gi] * inv) * nat_ref[...]
        b_ref[:, hs * DIL_HEAD_DIM:(hs + 1) * DIL_HEAD_DIM] = comb.astype(BF)
    for c in range(D_MODEL // MERGE_TN):
        cols = slice(c * MERGE_TN, (c + 1) * MERGE_TN)
        cols_b = slice(D_MODEL + c * MERGE_TN, D_MODEL + (c + 1) * MERGE_TN)
        gate_a = jax.nn.sigmoid(_dot(h_ref[...], wg_ref[:, cols]) + bg_ref[:, cols])
        gate_b = jax.nn.sigmoid(_dot(h_ref[...], wg_ref[:, cols_b]) + bg_ref[:, cols_b])
        o_a = _dot(a_ref[...], woa_ref[:, cols])
        o_b = _dot(b_ref[...], wob_ref[:, cols])
        mg_ref[:, cols] = (gate_a * o_a + gate_b * o_b).astype(BF)
    y = x + _dot(mg_ref[...], wout_ref[...])
    y_ref[...] = y
    h2_ref[...] = _rms(y, fg_ref[...]).astype(BF)


def _merge(x2, g, wg, bg, oa, o_dil, lse, woa, wob, wout, fg, seq):
    t = x2.shape[0]
    tm = MERGE_TM
    nseq = seq // tm
    const = lambda i: (0, 0)
    one = pl.Buffered(1)
    dil_specs = [
        pl.BlockSpec((DIL_HEADS_PER_GROUP, 1, dil, tm // dil, DIL_HEAD_DIM),
                     lambda i: (0, i // nseq, 0, i % nseq, 0))
        for _, dil in DIL_PATTERNS
    ]
    return pl.pallas_call(
        _merge_kernel,
        grid=(t // tm,),
        in_specs=[
            pl.BlockSpec((tm, D_MODEL), lambda i: (i, 0)),
            pl.BlockSpec((1, D_MODEL), const),
            pl.BlockSpec((D_MODEL, 2 * D_MODEL), const, pipeline_mode=one),
            pl.BlockSpec((1, 2 * D_MODEL), const),
            pl.BlockSpec((MLA_HEADS, tm, V_HEAD_DIM), lambda i: (0, i, 0)),
            *dil_specs,
            pl.BlockSpec((tm, DIL_HEADS), lambda i: (i, 0)),
            pl.BlockSpec((MLA_HEADS * V_HEAD_DIM, D_MODEL), const, pipeline_mode=one),
            pl.BlockSpec((DIL_HEADS_PER_GROUP * DIL_HEAD_DIM, D_MODEL), const, pipeline_mode=one),
            pl.BlockSpec((D_MODEL, D_MODEL), const, pipeline_mode=one),
            pl.BlockSpec((1, D_MODEL), const),
        ],
        out_specs=[pl.BlockSpec((tm, D_MODEL), lambda i: (i, 0)), pl.BlockSpec((tm, D_MODEL), lambda i: (i, 0))],
        out_shape=[jax.ShapeDtypeStruct((t, D_MODEL), F32), jax.ShapeDtypeStruct((t, D_MODEL), BF)],
        scratch_shapes=[
            pltpu.VMEM((tm, D_MODEL), BF),
            pltpu.VMEM((tm, MLA_HEADS * V_HEAD_DIM), BF),
            pltpu.VMEM((tm, DIL_HEADS_PER_GROUP * DIL_HEAD_DIM), BF),
            pltpu.VMEM((tm, D_MODEL), BF),
            pltpu.VMEM((tm, DIL_HEAD_DIM), F32),
        ],
        compiler_params=_params("arbitrary"),
        name="merge",
    )(x2, g, wg, bg, oa, *o_dil, lse, woa, wob, wout, fg)


def _ffn_up_kernel(h_ref, w_ref, cw_ref, a_ref, u0_ref, u1_ref, *, tiles, tiles_per_seq):
    i = pl.program_id(1)
    tm = h_ref.shape[0]
    u_refs = (u0_ref, u1_ref)

    def project(slot, r0, n):
        rows = slice(SUBLANE + r0, SUBLANE + r0 + FFN_PIECE)
        cols = slice(n * FFN_NSPLIT, (n + 1) * FFN_NSPLIT)
        u_refs[slot][rows, cols] = _dot(h_ref[r0:r0 + FFN_PIECE, :], w_ref[:, cols])

    def activate(slot, r1):
        u_ref = u_refs[slot]
        groups = FFN_ROWS // SUBLANE

        def conv(cols):
            window = u_ref[r1:r1 + FFN_ROWS + SUBLANE, cols]
            out = None
            for tap in range(CONV_WIDTH):
                back = CONV_WIDTH - 1 - tap
                shifted = window if back == 0 else pltpu.roll(window, back, 0)
                x = shifted[SUBLANE:].reshape(groups, SUBLANE, FF_CHUNK)
                term = cw_ref[tap * SUBLANE:(tap + 1) * SUBLANE, cols][None] * x
                out = term if out is None else out + term
            out = out + cw_ref[CONV_WIDTH * SUBLANE:(CONV_WIDTH + 1) * SUBLANE, cols][None]
            return out.reshape(FFN_ROWS, FF_CHUNK)

        up = conv(slice(0, FF_CHUNK))
        gate = conv(slice(FF_CHUNK, 2 * FF_CHUNK))
        a_ref[r1:r1 + FFN_ROWS, :] = (jax.nn.silu(gate) * up).astype(BF)

    def run(project_slot=None, activate_slot=None):
        if project_slot is not None:
            seq_start = (i % tiles_per_seq) == 0
            halo = u_refs[1 - project_slot][tm:]
            u_refs[project_slot][:SUBLANE] = jnp.where(seq_start, 0.0, halo)
        nsplit = 2 * FF_CHUNK // FFN_NSPLIT
        per_dot = FFN_PIECE // FFN_ROWS // nsplit
        for r0 in range(0, tm, FFN_PIECE):
            for n in range(nsplit):
                if project_slot is not None:
                    project(project_slot, r0, n)
                if activate_slot is not None:
                    for k in range(per_dot):
                        activate(activate_slot, r0 + (n * per_dot + k) * FFN_ROWS)

    @pl.when(i == 0)
    def _():
        u0_ref[:SUBLANE] = jnp.zeros((SUBLANE, 2 * FF_CHUNK), F32)
        for r0 in range(0, tm, FFN_PIECE):
            for n in range(2 * FF_CHUNK // FFN_NSPLIT):
                project(0, r0, n)

    for slot in range(2):
        @pl.when((i > 0) & (i < tiles) & (i % 2 == slot))
        def _(slot=slot):
            run(project_slot=slot, activate_slot=1 - slot)

    @pl.when(i == tiles)
    def _():
        run(activate_slot=(tiles - 1) % 2)


def _ffn_down_kernel(a_ref, wd_ref, x_ref, fg_ref, y_ref):
    y_ref[...] = _rms(x_ref[...] + _dot(a_ref[...], wd_ref[...]), fg_ref[...])


def _ffn_up(h2, wu, cw, seq):
    t = h2.shape[0]
    tm = FFN_TM
    tiles = t // tm
    return pl.pallas_call(
        functools.partial(_ffn_up_kernel, tiles=tiles, tiles_per_seq=seq // tm),
        grid=(FF_CHUNKS, tiles + 1),
        in_specs=[
            pl.BlockSpec((tm, D_MODEL), lambda c, i: (jnp.minimum(i, tiles - 1), 0)),
            pl.BlockSpec((D_MODEL, 2 * FF_CHUNK), lambda c, i: (0, c)),
            pl.BlockSpec(((CONV_WIDTH + 1) * SUBLANE, 2 * FF_CHUNK), lambda c, i: (0, c)),
        ],
        out_specs=pl.BlockSpec((tm, FF_CHUNK), lambda c, i: (jnp.maximum(i - 1, 0), c)),
        out_shape=jax.ShapeDtypeStruct((t, D_FF_PAD), BF),
        scratch_shapes=[
            pltpu.VMEM((tm + SUBLANE, 2 * FF_CHUNK), F32),
            pltpu.VMEM((tm + SUBLANE, 2 * FF_CHUNK), F32),
        ],
        compiler_params=_params("arbitrary", "arbitrary"),
        name="ffn_up",
    )(h2, wu, cw)


def _ffn_down(act, wd, x1, fg):
    t = x1.shape[0]
    tm = FFN_DOWN_TM
    return pl.pallas_call(
        _ffn_down_kernel,
        grid=(t // tm,),
        in_specs=[
            pl.BlockSpec((tm, D_FF_PAD), lambda i: (i, 0)),
            pl.BlockSpec((D_FF_PAD, D_MODEL), lambda i: (0, 0), pipeline_mode=pl.Buffered(1)),
            pl.BlockSpec((tm, D_MODEL), lambda i: (i, 0)),
            pl.BlockSpec((1, D_MODEL), lambda i: (0, 0)),
        ],
        out_specs=pl.BlockSpec((tm, D_MODEL), lambda i: (i, 0)),
        out_shape=jax.ShapeDtypeStruct((t, D_MODEL), F32),
        compiler_params=_params("arbitrary"),
        name="ffn_down",
    )(act, wd, x1, fg)


def _rotate_half_cols(w):
    half = w.shape[-1] // 2
    return jnp.concatenate([-w[..., half:], w[..., :half]], axis=-1)


def _chunk_ff_columns(a, dtype):
    parts = []
    for c in range(FF_CHUNKS):
        lo, hi = c * FF_CHUNK, min((c + 1) * FF_CHUNK, D_FF)
        for half in (0, D_FF):
            parts.append(a[:, half + lo:half + hi].astype(dtype))
            if hi - lo < FF_CHUNK:
                parts.append(jnp.zeros((a.shape[0], FF_CHUNK - (hi - lo)), dtype))
    return jnp.concatenate(parts, axis=1)


def kernel(x, attn_norm_g, w_in, b_gate, q_norm_g, w_uq, kv_norm_g, w_ukv, w_o_mla, w_o_dil,
           w_out, ffn_norm_g, w_up, conv_w, conv_b, w_down, final_norm_g):
    batch, seq, _ = x.shape
    assert w_in.shape[0] == 1, "single-layer block"
    assert seq % TM == 0 and seq % FFN_TM == 0 and seq % TQ == 0 and TQ == TK and seq % (16 * DIL_BLOCK) == 0
    assert FF_CHUNKS >= 3
    t = batch * seq
    x2 = x.reshape(t, D_MODEL)

    dqkv = DIL_HEADS * DIL_HEAD_DIM
    o = np.cumsum((0, Q_LORA_RANK, KV_LORA_RANK, QK_ROPE_DIM, dqkv, dqkv, dqkv, D_MODEL, D_MODEL))
    wi = w_in[0]
    w_kpe = wi[:, o[2]:o[3]]
    wlat = jnp.concatenate([wi[:, o[0]:o[2]], w_kpe, _rotate_half_cols(w_kpe)], axis=1).astype(BF)
    gw = DIL_HEADS_PER_GROUP * DIL_HEAD_DIM
    wdil = jnp.concatenate(
        [wi[:, o[3 + which] + gi * gw:o[3 + which] + (gi + 1) * gw].astype(BF)
         for gi in range(DIL_GROUPS) for which in range(3)], axis=1)
    wgate = wi[:, o[6]:o[8]].astype(BF)
    wq = w_uq[0].astype(BF).reshape(Q_LORA_RANK, MLA_HEADS, QK_NOPE_DIM + QK_ROPE_DIM)
    wq_pe = wq[..., QK_NOPE_DIM:]
    wq = jnp.concatenate([wq[..., :QK_NOPE_DIM], wq_pe, _rotate_half_cols(wq_pe)], axis=-1)
    wq = wq.reshape(Q_LORA_RANK, MLA_HEADS * QK_SLOT)
    wkv = w_ukv[0].astype(BF)
    wu = _chunk_ff_columns(w_up[0], BF)
    cw = jnp.repeat(jnp.concatenate([conv_w[0], conv_b], axis=0), SUBLANE, axis=0)
    cw = _chunk_ff_columns(cw, F32)
    wd =jnp.concatenate([w_down[0].astype(BF), jnp.zeros((D_FF_PAD - D_FF, D_MODEL), BF)], axis=0)

    pos = jnp.arange(seq, dtype=F32)
    inv_freq = ROPE_THETA ** (-jnp.arange(0, QK_ROPE_DIM, 2, dtype=F32) / QK_ROPE_DIM)
    ang = pos[:, None] * inv_freq[None, :]
    cs = jnp.concatenate([jnp.cos(ang), jnp.cos(ang), jnp.sin(ang), jnp.sin(ang)], axis=1)
    slopes = 2.0 ** (-ALIBI_MAX_BIAS * jnp.arange(1, DIL_HEADS + 1, dtype=F32) / DIL_HEADS)

    q, k, v = _mla_prep(x2, attn_norm_g, wlat, q_norm_g, kv_norm_g, wq, wkv, cs, seq)
    o_mla = _mla_flash(q, k, v, batch, seq)
    qkv_dil = _dil_proj(x2, attn_norm_g, wdil, batch, seq)
    o_dil, lse_dil = [], []
    for gi, (_, dil) in enumerate(DIL_PATTERNS):
        o_g, lse_g = _dilated(qkv_dil[gi], slopes, gi, batch, seq)
        nb = seq // dil // DIL_BLOCK
        o_dil.append(o_g)
        lse_g = lse_g.reshape(DIL_HEADS_PER_GROUP, batch, dil, nb, DIL_BLOCK)
        lse_dil.append(lse_g.transpose(1, 3, 4, 2, 0).reshape(t, DIL_HEADS_PER_GROUP))
    lse = jnp.concatenate(lse_dil, axis=1)
    x1, h2 = _merge(x2, attn_norm_g, wgate, b_gate, o_mla, o_dil, lse,
                    w_o_mla[0].astype(BF), w_o_dil[0].astype(BF), w_out[0].astype(BF), ffn_norm_g, seq)

    act = _ffn_up(h2, wu, cw, seq)
    y = _ffn_down(act, wd, x1, final_norm_g.reshape(1, D_MODEL))
    return y.reshape(batch, seq, D_MODEL)
```

```python
import functools
import math

import numpy as np
import jax
import jax.numpy as jnp
from jax import lax
from jax.experimental import pallas as pl
from jax.experimental.pallas import tpu as pltpu

D_MODEL = 2048
MLA_HEADS = 8
QK_NOPE_DIM = 128
QK_ROPE_DIM = 64
V_HEAD_DIM = 128
Q_LORA_RANK = 512
KV_LORA_RANK = 256
ROPE_THETA = 10000.0
DIL_PATTERNS = ((128, 1), (512, 4), (2048, 16))
DIL_GROUPS = 3
DIL_HEADS_PER_GROUP = 4
DIL_HEADS = DIL_GROUPS * DIL_HEADS_PER_GROUP
DIL_HEAD_DIM = 128
DIL_BLOCK = 128
ALIBI_MAX_BIAS = 8.0
D_FF = 5504
CONV_WIDTH = 3
NORM_EPS = 1e-6

LANE = 128
SUBLANE = 8
VMEM_LIMIT = 56 * 1024 * 1024

LAT_DIM = Q_LORA_RANK + KV_LORA_RANK + 2 * QK_ROPE_DIM
QK_SLOT = 2 * LANE
V_SLOT = 2 * LANE
DIL_GROUP_DIM = 3 * DIL_HEADS_PER_GROUP * DIL_HEAD_DIM
FF_CHUNK = 512
FFN_TM = 1024
FFN_DOWN_TM = 256
FFN_PIECE = 128
FFN_NSPLIT = 256
FFN_ROWS = 32
D_FF_PAD = -(-D_FF // FF_CHUNK) * FF_CHUNK
FF_CHUNKS = D_FF_PAD // FF_CHUNK
NEG = -1e30

TM = 512
TQ = 512
TK = 512
FLASH_HEADS = 2
DIL_BATCH = 8

BF = jnp.bfloat16
F32 = jnp.float32


def _params(*sem):
    return pltpu.CompilerParams(dimension_semantics=sem, vmem_limit_bytes=VMEM_LIMIT)


def _rms(x, g):
    return x * lax.rsqrt(jnp.mean(x * x, axis=-1, keepdims=True) + NORM_EPS) * g


def _dot(a, b):
    return jnp.dot(a, b, preferred_element_type=F32)


def _dot_nt(a, b):
    return lax.dot_general(a, b, (((1,), (1,)), ((), ())), preferred_element_type=F32)


def _rope_pair(t):
    lane = lax.broadcasted_iota(jnp.int32, t.shape, 1)
    return jnp.where(lane < QK_ROPE_DIM, t + pltpu.roll(t, QK_ROPE_DIM, 1), 0.0)


def _mla_prep_kernel(x_ref, g_ref, wlat_ref, qg_ref, kvg_ref, wq_ref, wkv_ref, cs_ref,
                     q_ref, k_ref, v_ref):
    h = _rms(x_ref[...], g_ref[...]).astype(BF)
    lat = _dot(h, wlat_ref[...])
    cq = _rms(lat[:, :Q_LORA_RANK], qg_ref[...]).astype(BF)
    ckv = _rms(lat[:, Q_LORA_RANK:Q_LORA_RANK + KV_LORA_RANK], kvg_ref[...]).astype(BF)
    q = _dot(cq, wq_ref[...])
    kv = _dot(ckv, wkv_ref[...])
    cs = cs_ref[...]
    scale = (QK_NOPE_DIM + QK_ROPE_DIM) ** -0.5 * math.log2(math.e)
    k_rope = _rope_pair(lat[:, Q_LORA_RANK + KV_LORA_RANK:] * cs).astype(BF)
    ones = jnp.ones((x_ref.shape[0], V_SLOT - V_HEAD_DIM), BF)
    for hd in range(MLA_HEADS):
        qs = q[:, hd * QK_SLOT:(hd + 1) * QK_SLOT]
        q_ref[hd, :, :LANE] = (qs[:, :LANE] * scale).astype(BF)
        q_ref[hd, :, LANE:] = _rope_pair(qs[:, LANE:] * (cs * scale)).astype(BF)
        k_ref[hd, :, :LANE] = kv[:, hd * 256:hd * 256 + LANE].astype(BF)
        k_ref[hd, :, LANE:] = k_rope
        v_ref[hd, :, :V_HEAD_DIM] = kv[:, hd * 256 + LANE:(hd + 1) * 256].astype(BF)
        v_ref[hd, :, V_HEAD_DIM:] = ones


def _mla_prep(x2, g, wlat, qg, kvg, wq, wkv, cs, seq):
    t = x2.shape[0]
    nseq = seq // TM
    const = lambda i: (0, 0)
    return pl.pallas_call(
        _mla_prep_kernel,
        grid=(t // TM,),
        in_specs=[
            pl.BlockSpec((TM, D_MODEL), lambda i: (i, 0)),
            pl.BlockSpec((1, D_MODEL), const),
            pl.BlockSpec((D_MODEL, LAT_DIM), const),
            pl.BlockSpec((1, Q_LORA_RANK), const),
            pl.BlockSpec((1, KV_LORA_RANK), const),
            pl.BlockSpec((Q_LORA_RANK, MLA_HEADS * QK_SLOT), const),
            pl.BlockSpec((KV_LORA_RANK, MLA_HEADS * 256), const),
            pl.BlockSpec((TM, LANE), lambda i: (i % nseq, 0)),
        ],
        out_specs=[
            pl.BlockSpec((MLA_HEADS, TM, QK_SLOT), lambda i: (0, i, 0)),
            pl.BlockSpec((MLA_HEADS, TM, QK_SLOT), lambda i: (0, i, 0)),
            pl.BlockSpec((MLA_HEADS, TM, V_SLOT), lambda i: (0, i, 0)),
        ],
        out_shape=[
            jax.ShapeDtypeStruct((MLA_HEADS, t, QK_SLOT), BF),
            jax.ShapeDtypeStruct((MLA_HEADS, t, QK_SLOT), BF),
            jax.ShapeDtypeStruct((MLA_HEADS, t, V_SLOT), BF),
        ],
        compiler_params=_params("arbitrary"),
        name="mla_prep",
    )(x2, g, wlat, qg, kvg, wq, wkv, cs)


def _dil_proj_kernel(x_ref, g_ref, w_ref, *rest):
    o_refs, h_ref, res_ref = rest[:DIL_GROUPS], rest[DIL_GROUPS], rest[DIL_GROUPS + 1]
    nqkv = 3 * DIL_HEADS_PER_GROUP
    h_ref[...] = _rms(x_ref[...], g_ref[...]).astype(BF)
    for gi, (_, dil) in enumerate(DIL_PATTERNS):
        sub = TM // dil
        res = _dot(h_ref[...], w_ref[:, gi * DIL_GROUP_DIM:(gi + 1) * DIL_GROUP_DIM])
        for hh in range(nqkv):
            blk = res[:, hh * DIL_HEAD_DIM:(hh + 1) * DIL_HEAD_DIM]
            if hh < DIL_HEADS_PER_GROUP:
                blk = blk * DIL_HEAD_DIM ** -0.5
            if dil == 1:
                o_refs[gi][hh, 0, 0] = blk.astype(BF)
            else:
                res_ref[hh] = blk
        if dil > 1:
            for hh in range(nqkv):
                for r in range(dil):
                    o_refs[gi][hh, 0, r] = res_ref[hh, pl.ds(r, sub, stride=dil), :].astype(BF)


def _dil_proj(x2, g, w, batch, seq):
    nseq = seq // TM
    nqkv = 3 * DIL_HEADS_PER_GROUP
    const = lambda i: (0, 0)
    return pl.pallas_call(
        _dil_proj_kernel,
        grid=(batch * nseq,),
        in_specs=[
            pl.BlockSpec((TM, D_MODEL), lambda i: (i, 0)),
            pl.BlockSpec((1, D_MODEL), const),
            pl.BlockSpec((D_MODEL, DIL_GROUPS * DIL_GROUP_DIM), const, pipeline_mode=pl.Buffered(1)),
        ],
        out_specs=[
            pl.BlockSpec((nqkv, 1, dil, TM // dil, DIL_HEAD_DIM), lambda i: (0, i // nseq, 0, i % nseq, 0))
            for _, dil in DIL_PATTERNS
        ],
        out_shape=[
            jax.ShapeDtypeStruct((nqkv, batch, dil, seq // dil, DIL_HEAD_DIM), BF)
            for _, dil in DIL_PATTERNS
        ],
        scratch_shapes=[pltpu.VMEM((TM, D_MODEL), BF), pltpu.VMEM((nqkv, TM, DIL_HEAD_DIM), F32)],
        compiler_params=_params("arbitrary"),
        name="dil_proj",
    )(x2, g, w)


def _flash_kernel(q_ref, k_ref, v_ref, o_ref, *scratch):
    nh = FLASH_HEADS
    m_refs, acc_refs = scratch[:nh], scratch[nh:2 * nh]
    s_refs = (scratch[2 * nh:3 * nh], scratch[3 * nh:4 * nh])
    qi = pl.program_id(2)
    for hh in range(nh):
        m_refs[hh][...] = jnp.full(m_refs[hh].shape, NEG, F32)
        acc_refs[hh][...] = jnp.zeros(acc_refs[hh].shape, F32)

    def key_rows(j):
        return pl.ds(pl.multiple_of(j * TK, TK), TK)

    def scores(j, slot):
        for hh in range(nh):
            s_refs[slot][hh][...] = _dot_nt(q_ref[hh], k_ref[hh, key_rows(j), :])

    def mask_diagonal(slot):
        keep = lax.broadcasted_iota(jnp.int32, (TQ, TK), 1) <= lax.broadcasted_iota(jnp.int32, (TQ, TK), 0)
        for hh in range(nh):
            s_refs[slot][hh][...] = jnp.where(keep, s_refs[slot][hh][...], NEG)

    def update(j, slot):
        for hh in range(nh):
            s = s_refs[slot][hh][...]
            m_old = m_refs[hh][...]
            m_new = jnp.maximum(m_old, jnp.max(s, axis=-1, keepdims=True))
            p = jnp.exp2(s - m_new)
            pv = _dot(p.astype(BF), v_ref[hh, key_rows(j), :])
            acc_refs[hh][...] = jnp.exp2(m_old - m_new) * acc_refs[hh][...] + pv
            m_refs[hh][...] = m_new

    scores(0, 0)

    @pl.when(qi == 0)
    def _():
        mask_diagonal(0)

    def body(j, carry):
        for slot in range(2):
            @pl.when(j % 2 == slot)
            def _(slot=slot):
                scores(j, slot)
                update(j - 1, 1 - slot)

            @pl.when((j % 2 == slot) & (j == qi))
            def _(slot=slot):
                mask_diagonal(slot)
        return carry

    lax.fori_loop(1, qi + 1, body, 0)

    for slot in range(2):
        @pl.when(qi % 2 == slot)
        def _(slot=slot):
            update(qi, slot)

    for hh in range(FLASH_HEADS):
        acc = acc_refs[hh][...]
        o_ref[hh] = (acc[:, :V_HEAD_DIM] * (1.0 / acc[:, V_HEAD_DIM:])).astype(BF)


def _mla_flash(q, k, v, batch, seq):
    nq = seq // TQ
    t = batch * seq
    hb = FLASH_HEADS
    return pl.pallas_call(
        _flash_kernel,
        grid=(batch, MLA_HEADS // hb, nq),
        in_specs=[
            pl.BlockSpec((hb, TQ, QK_SLOT), lambda b, h, i: (h, b * nq + i, 0)),
            pl.BlockSpec((hb, seq, QK_SLOT), lambda b, h, i: (h, b, 0)),
            pl.BlockSpec((hb, seq, V_SLOT), lambda b, h, i: (h, b, 0)),
        ],
        out_specs=pl.BlockSpec((hb, TQ, V_HEAD_DIM), lambda b, h, i: (h, b * nq + i, 0)),
        out_shape=jax.ShapeDtypeStruct((MLA_HEADS, t, V_HEAD_DIM), BF),
        scratch_shapes=([pltpu.VMEM((TQ, 1), F32)] * hb + [pltpu.VMEM((TQ, V_SLOT), F32)] * hb
                        + [pltpu.VMEM((TQ, TK), F32)] * (2 * hb)),
        compiler_params=_params("arbitrary", "arbitrary", "arbitrary"),
        name="mla_flash",
    )(q, k, v)


def _dilated_kernel(slope_ref, q_ref, k_ref, v_ref, o_ref, lse_ref, *, group, dil, nb):
    hs = pl.program_id(1)
    slope = slope_ref[group * DIL_HEADS_PER_GROUP + hs] * float(dil)
    qi = lax.broadcasted_iota(jnp.int32, (DIL_BLOCK, DIL_BLOCK), 0)
    ki = lax.broadcasted_iota(jnp.int32, (DIL_BLOCK, DIL_BLOCK), 1)
    dist_cur = (qi - ki).astype(F32)
    dist_prev = dist_cur + float(DIL_BLOCK)
    w_sub = float(DIL_PATTERNS[group][0] // dil)
    bias_cur = jnp.where((dist_cur >= 0.0) & (dist_cur <= w_sub), -slope * dist_cur, NEG)
    bias_prev = jnp.where(dist_prev <= w_sub, -slope * dist_prev, NEG)
    neg = jnp.full((DIL_BLOCK, DIL_BLOCK), NEG, F32)
    bias_inner = jnp.concatenate([bias_prev, bias_cur], axis=1)
    bias_first = jnp.concatenate([bias_cur, neg], axis=1)
    eye = qi == ki

    def window(n):
        start = max(n - 1, 0) * DIL_BLOCK
        return slice(start, start + 2 * DIL_BLOCK)

    blocks = [(r, n) for r in range(dil) for n in range(nb)]
    for g0 in range(0, len(blocks), DIL_BATCH):
        batch = blocks[g0:g0 + DIL_BATCH]
        scores = [_dot_nt(q_ref[0, 0, r, n * DIL_BLOCK:(n + 1) * DIL_BLOCK, :], k_ref[0, 0, r, window(n), :])
                  + (bias_first if n == 0 else bias_inner) for r, n in batch]
        maxes = [jnp.max(s, axis=-1, keepdims=True) for s in scores]
        probs = [jnp.exp(s - m) for s, m in zip(scores, maxes)]
        sums = [jnp.sum(p, axis=-1, keepdims=True) for p in probs]
        outs = [_dot(p.astype(BF), v_ref[0, 0, r, window(n), :]) for p, (r, n) in zip(probs, batch)]
        for (r, n), o, l in zip(batch, outs, sums):
            o_ref[0, 0, r, n * DIL_BLOCK:(n + 1) * DIL_BLOCK, :] = (o * (1.0 / l)).astype(BF)
        rows = [jnp.sum(jnp.where(eye, m + jnp.log(l), 0.0), axis=0, keepdims=True) for m, l in zip(maxes, sums)]
        lse_ref[0, 0, g0:g0 + DIL_BATCH, :] = jnp.concatenate(rows, axis=0)


def _dilated(qkv, slopes, group, batch, seq):
    _, dil = DIL_PATTERNS[group]
    sub = seq // dil
    nb = sub // DIL_BLOCK
    hg = DIL_HEADS_PER_GROUP
    blk = (1, 1, dil, sub, DIL_HEAD_DIM)

    def head(which):
        return lambda b, h: (which * hg + h, b, 0, 0, 0)

    return pl.pallas_call(
        functools.partial(_dilated_kernel, group=group, dil=dil, nb=nb),
        grid=(batch, hg),
        in_specs=[
            pl.BlockSpec(memory_space=pltpu.SMEM),
            pl.BlockSpec(blk, head(0)),
            pl.BlockSpec(blk, head(1)),
            pl.BlockSpec(blk, head(2)),
        ],
        out_specs=[
            pl.BlockSpec(blk, head(0)),
            pl.BlockSpec((1, 1, dil * nb, DIL_BLOCK), lambda b, h: (h, b, 0, 0)),
        ],
        out_shape=[
            jax.ShapeDtypeStruct((hg, batch, dil, sub, DIL_HEAD_DIM), BF),
            jax.ShapeDtypeStruct((hg, batch, dil * nb, DIL_BLOCK), F32),
        ],
        compiler_params=_params("arbitrary", "arbitrary"),
        name=f"dilated_d{dil}",
    )(slopes, qkv, qkv, qkv)


MERGE_TN = 512
MERGE_TM = 256


def _merge_kernel(x_ref, g_ref, wg_ref, bg_ref, oa_ref, o1_ref, o4_ref, o16_ref, lse_ref,
                  woa_ref, wob_ref, wout_ref, fg_ref, y_ref, h2_ref, h_ref, a_ref, b_ref, mg_ref, nat_ref):
    x = x_ref[...]
    tm = x.shape[0]
    h_ref[...] = _rms(x, g_ref[...]).astype(BF)
    for hd in range(MLA_HEADS):
        a_ref[:, hd * V_HEAD_DIM:(hd + 1) * V_HEAD_DIM] = oa_ref[hd]
    lse = lse_ref[...]
    o_refs = (o1_ref, o4_ref, o16_ref)
    for hs in range(DIL_HEADS_PER_GROUP):
        ls = [lse[:, gi * DIL_HEADS_PER_GROUP + hs:gi * DIL_HEADS_PER_GROUP + hs + 1] for gi in range(DIL_GROUPS)]
        mx = jnp.maximum(jnp.maximum(ls[0], ls[1]), ls[2])
        es = [jnp.exp(v - mx) for v in ls]
        inv = 1.0 / (es[0] + es[1] + es[2])
        comb = (es[0] * inv) * o_refs[0][hs, 0, 0].astype(F32)
        for gi in range(1, DIL_GROUPS):
            dil = DIL_PATTERNS[gi][1]
            for r in range(dil):
                nat_ref[pl.ds(r, tm // dil, stride=dil), :] = o_refs[gi][hs, 0, r].astype(F32)
            comb = comb + (es[gi] * inv) * nat_ref[...]
        b_ref[:, hs * DIL_HEAD_DIM:(hs + 1) * DIL_HEAD_DIM] = comb.astype(BF)
    for c in range(D_MODEL // MERGE_TN):
        cols = slice(c * MERGE_TN, (c + 1) * MERGE_TN)
        cols_b = slice(D_MODEL + c * MERGE_TN, D_MODEL + (c + 1) * MERGE_TN)
        gate_a = jax.nn.sigmoid(_dot(h_ref[...], wg_ref[:, cols]) + bg_ref[:, cols])
        gate_b = jax.nn.sigmoid(_dot(h_ref[...], wg_ref[:, cols_b]) + bg_ref[:, cols_b])
        o_a = _dot(a_ref[...], woa_ref[:, cols])
        o_b = _dot(b_ref[...], wob_ref[:, cols])
        mg_ref[:, cols] = (gate_a * o_a + gate_b * o_b).astype(BF)
    y = x + _dot(mg_ref[...], wout_ref[...])
    y_ref[...] = y
    h2_ref[...] = _rms(y, fg_ref[...]).astype(BF)


def _merge(x2, g, wg, bg, oa, o_dil, lse, woa, wob, wout, fg, seq):
    t = x2.shape[0]
    tm = MERGE_TM
    nseq = seq // tm
    const = lambda i: (0, 0)
    one = pl.Buffered(1)
    dil_specs = [
        pl.BlockSpec((DIL_HEADS_PER_GROUP, 1, dil, tm // dil, DIL_HEAD_DIM),
                     lambda i: (0, i // nseq, 0, i % nseq, 0))
        for _, dil in DIL_PATTERNS
    ]
    return pl.pallas_call(
        _merge_kernel,
        grid=(t // tm,),
        in_specs=[
            pl.BlockSpec((tm, D_MODEL), lambda i: (i, 0)),
            pl.BlockSpec((1, D_MODEL), const),
            pl.BlockSpec((D_MODEL, 2 * D_MODEL), const, pipeline_mode=one),
            pl.BlockSpec((1, 2 * D_MODEL), const),
            pl.BlockSpec((MLA_HEADS, tm, V_HEAD_DIM), lambda i: (0, i, 0)),
            *dil_specs,
            pl.BlockSpec((tm, DIL_HEADS), lambda i: (i, 0)),
            pl.BlockSpec((MLA_HEADS * V_HEAD_DIM, D_MODEL), const, pipeline_mode=one),
            pl.BlockSpec((DIL_HEADS_PER_GROUP * DIL_HEAD_DIM, D_MODEL), const, pipeline_mode=one),
            pl.BlockSpec((D_MODEL, D_MODEL), const, pipeline_mode=one),
            pl.BlockSpec((1, D_MODEL), const),
        ],
        out_specs=[pl.BlockSpec((tm, D_MODEL), lambda i: (i, 0)), pl.BlockSpec((tm, D_MODEL), lambda i: (i, 0))],
        out_shape=[jax.ShapeDtypeStruct((t, D_MODEL), F32), jax.ShapeDtypeStruct((t, D_MODEL), BF)],
        scratch_shapes=[
            pltpu.VMEM((tm, D_MODEL), BF),
            pltpu.VMEM((tm, MLA_HEADS * V_HEAD_DIM), BF),
            pltpu.VMEM((tm, DIL_HEADS_PER_GROUP * DIL_HEAD_DIM), BF),
            pltpu.VMEM((tm, D_MODEL), BF),
            pltpu.VMEM((tm, DIL_HEAD_DIM), F32),
        ],
        compiler_params=_params("arbitrary"),
        name="merge",
    )(x2, g, wg, bg, oa, *o_dil, lse, woa, wob, wout, fg)


def _ffn_up_kernel(h_ref, w_ref, cw_ref, a_ref, u0_ref, u1_ref, *, tiles, tiles_per_seq):
    i = pl.program_id(1)
    tm = h_ref.shape[0]
    u_refs = (u0_ref, u1_ref)

    def project(slot, r0, n):
        rows = slice(SUBLANE + r0, SUBLANE + r0 + FFN_PIECE)
        cols = slice(n * FFN_NSPLIT, (n + 1) * FFN_NSPLIT)
        u_refs[slot][rows, cols] = _dot(h_ref[r0:r0 + FFN_PIECE, :], w_ref[:, cols])

    def activate(slot, r1):
        u_ref = u_refs[slot]
        groups = FFN_ROWS // SUBLANE

        def conv(cols):
            window = u_ref[r1:r1 + FFN_ROWS + SUBLANE, cols]
            out = None
            for tap in range(CONV_WIDTH):
                back = CONV_WIDTH - 1 - tap
                shifted = window if back == 0 else pltpu.roll(window, back, 0)
                x = shifted[SUBLANE:].reshape(groups, SUBLANE, FF_CHUNK)
                term = cw_ref[tap * SUBLANE:(tap + 1) * SUBLANE, cols][None] * x
                out = term if out is None else out + term
            out = out + cw_ref[CONV_WIDTH * SUBLANE:(CONV_WIDTH + 1) * SUBLANE, cols][None]
            return out.reshape(FFN_ROWS, FF_CHUNK)

        up = conv(slice(0, FF_CHUNK))
        gate = conv(slice(FF_CHUNK, 2 * FF_CHUNK))
        a_ref[r1:r1 + FFN_ROWS, :] = (jax.nn.silu(gate) * up).astype(BF)

    def run(project_slot=None, activate_slot=None):
        if project_slot is not None:
            seq_start = (i % tiles_per_seq) == 0
            halo = u_refs[1 - project_slot][tm:]
            u_refs[project_slot][:SUBLANE] = jnp.where(seq_start, 0.0, halo)
        nsplit = 2 * FF_CHUNK // FFN_NSPLIT
        per_dot = FFN_PIECE // FFN_ROWS // nsplit
        for r0 in range(0, tm, FFN_PIECE):
            for n in range(nsplit):
                if project_slot is not None:
                    project(project_slot, r0, n)
                if activate_slot is not None:
                    for k in range(per_dot):
                        activate(activate_slot, (r0 + tm // 2 + (n * per_dot + k) * FFN_ROWS) % tm)

    @pl.when(i == 0)
    def _():
        u0_ref[:SUBLANE] = jnp.zeros((SUBLANE, 2 * FF_CHUNK), F32)
        for r0 in range(0, tm, FFN_PIECE):
            for n in range(2 * FF_CHUNK // FFN_NSPLIT):
                project(0, r0, n)

    for slot in range(2):
        @pl.when((i > 0) & (i < tiles) & (i % 2 == slot))
        def _(slot=slot):
            run(project_slot=slot, activate_slot=1 - slot)

    @pl.when(i == tiles)
    def _():
        run(activate_slot=(tiles - 1) % 2)


def _ffn_down_kernel(a_ref, wd_ref, x_ref, fg_ref, y_ref):
    y_ref[...] = _rms(x_ref[...] + _dot(a_ref[...], wd_ref[...]), fg_ref[...])


def _ffn_up(h2, wu, cw, seq):
    t = h2.shape[0]
    tm = FFN_TM
    tiles = t // tm
    return pl.pallas_call(
        functools.partial(_ffn_up_kernel, tiles=tiles, tiles_per_seq=seq // tm),
        grid=(FF_CHUNKS, tiles + 1),
        in_specs=[
            pl.BlockSpec((tm, D_MODEL), lambda c, i: (jnp.minimum(i, tiles - 1), 0)),
            pl.BlockSpec((D_MODEL, 2 * FF_CHUNK), lambda c, i: (0, c)),
            pl.BlockSpec(((CONV_WIDTH + 1) * SUBLANE, 2 * FF_CHUNK), lambda c, i: (0, c)),
        ],
        out_specs=pl.BlockSpec((tm, FF_CHUNK), lambda c, i: (jnp.maximum(i - 1, 0), c)),
        out_shape=jax.ShapeDtypeStruct((t, D_FF_PAD), BF),
        scratch_shapes=[
            pltpu.VMEM((tm + SUBLANE, 2 * FF_CHUNK), F32),
            pltpu.VMEM((tm + SUBLANE, 2 * FF_CHUNK), F32),
        ],
        compiler_params=_params("arbitrary", "arbitrary"),
        name="ffn_up",
    )(h2, wu, cw)


def _ffn_down(act, wd, x1, fg):
    t = x1.shape[0]
    tm = FFN_DOWN_TM
    return pl.pallas_call(
        _ffn_down_kernel,
        grid=(t // tm,),
        in_specs=[
            pl.BlockSpec((tm, D_FF_PAD), lambda i: (i, 0)),
            pl.BlockSpec((D_FF_PAD, D_MODEL), lambda i: (0, 0), pipeline_mode=pl.Buffered(1)),
            pl.BlockSpec((tm, D_MODEL), lambda i: (i, 0)),
            pl.BlockSpec((1, D_MODEL), lambda i: (0, 0)),
        ],
        out_specs=pl.BlockSpec((tm, D_MODEL), lambda i: (i, 0)),
        out_shape=jax.ShapeDtypeStruct((t, D_MODEL), F32),
        compiler_params=_params("arbitrary"),
        name="ffn_down",
    )(act, wd, x1, fg)


def _rotate_half_cols(w):
    half = w.shape[-1] // 2
    return jnp.concatenate([-w[..., half:], w[..., :half]], axis=-1)


PREP_PIECE = 512
PREP_ROWS = 256


def _relayout_kernel(src_ref, *out_refs, plans):
    ncols = src_ref.shape[1]
    for out_ref, plan in zip(out_refs, plans):
        col = 0
        for start, width in plan:
            for off in range(0, width, PREP_PIECE):
                w = min(PREP_PIECE, width - off)
                dst = slice(col + off, col + off + w)
                if start is None:
                    out_ref[:, dst] = jnp.zeros((out_ref.shape[0], w), out_ref.dtype)
                    continue
                lead = (start + off) % LANE
                lo = start + off - lead
                hi = min(-(-(start + off + w) // LANE) * LANE, ncols)
                out_ref[:, dst] = src_ref[:, lo:hi][:, lead:lead + w].astype(out_ref.dtype)
            col += width


def _relayout(src, plans, dtype, row_tile, name):
    rows, ncols = src.shape
    widths = [sum(w for _, w in plan) for plan in plans]
    assert rows % row_tile == 0 and all(w % LANE == 0 for plan in plans for _, w in plan)
    outs = pl.pallas_call(
        functools.partial(_relayout_kernel, plans=plans),
        grid=(rows // row_tile,),
        in_specs=[pl.BlockSpec((row_tile, ncols), lambda i: (i, 0))],
        out_specs=[pl.BlockSpec((row_tile, w), lambda i: (i, 0)) for w in widths],
        out_shape=[jax.ShapeDtypeStruct((rows, w), dtype) for w in widths],
        compiler_params=_params("arbitrary"),
        name=name,
    )(src)
    return outs


def _ff_chunk_plan():
    plan = []
    for c in range(FF_CHUNKS):
        lo, hi = c * FF_CHUNK, min((c + 1) * FF_CHUNK, D_FF)
        for half in (0, D_FF):
            plan.append((half + lo, hi - lo))
            if hi - lo < FF_CHUNK:
                plan.append((None, FF_CHUNK - (hi - lo)))
    return plan


def kernel(x, attn_norm_g, w_in, b_gate, q_norm_g, w_uq, kv_norm_g, w_ukv, w_o_mla, w_o_dil,
           w_out, ffn_norm_g, w_up, conv_w, conv_b, w_down, final_norm_g):
    batch, seq, _ = x.shape
    assert w_in.shape[0] == 1, "single-layer block"
    assert seq % TM == 0 and seq % FFN_TM == 0 and seq % TQ == 0 and TQ == TK and seq % (16 * DIL_BLOCK) == 0
    assert FF_CHUNKS >= 3
    t = batch * seq
    x2 = x.reshape(t, D_MODEL)

    dqkv = DIL_HEADS * DIL_HEAD_DIM
    o = np.cumsum((0, Q_LORA_RANK, KV_LORA_RANK, QK_ROPE_DIM, dqkv, dqkv, dqkv, D_MODEL, D_MODEL))
    wi = w_in[0]
    w_kpe = wi[:, o[2]:o[3]]
    wlat = jnp.concatenate([wi[:, o[0]:o[2]], w_kpe, _rotate_half_cols(w_kpe)], axis=1).astype(BF)
    gw = DIL_HEADS_PER_GROUP * DIL_HEAD_DIM
    dil_plan = [(int(o[3 + which]) + gi * gw, gw) for gi in range(DIL_GROUPS) for which in range(3)]
    wdil, wgate = _relayout(wi, [dil_plan, [(int(o[6]), 2 * D_MODEL)]], BF, PREP_ROWS, "relayout_w_in")
    wq = w_uq[0].astype(BF).reshape(Q_LORA_RANK, MLA_HEADS, QK_NOPE_DIM + QK_ROPE_DIM)
    wq_pe = wq[..., QK_NOPE_DIM:]
    wq = jnp.concatenate([wq[..., :QK_NOPE_DIM], wq_pe, _rotate_half_cols(wq_pe)], axis=-1)
    wq = wq.reshape(Q_LORA_RANK, MLA_HEADS * QK_SLOT)
    wkv = w_ukv[0].astype(BF)
    (wu,) = _relayout(w_up[0], [_ff_chunk_plan()], BF, PREP_ROWS, "relayout_w_up")
    cw = jnp.repeat(jnp.concatenate([conv_w[0], conv_b], axis=0), SUBLANE, axis=0)
    (cw,) = _relayout(cw, [_ff_chunk_plan()], F32, cw.shape[0], "relayout_conv")
    wd =jnp.concatenate([w_down[0].astype(BF), jnp.zeros((D_FF_PAD - D_FF, D_MODEL), BF)], axis=0)

    pos = jnp.arange(seq, dtype=F32)
    inv_freq = ROPE_THETA ** (-jnp.arange(0, QK_ROPE_DIM, 2, dtype=F32) / QK_ROPE_DIM)
    ang = pos[:, None] * inv_freq[None, :]
    cs = jnp.concatenate([jnp.cos(ang), jnp.cos(ang), jnp.sin(ang), jnp.sin(ang)], axis=1)
    slopes = 2.0 ** (-ALIBI_MAX_BIAS * jnp.arange(1, DIL_HEADS + 1, dtype=F32) / DIL_HEADS)

    q, k, v = _mla_prep(x2, attn_norm_g, wlat, q_norm_g, kv_norm_g, wq, wkv, cs, seq)
    o_mla = _mla_flash(q, k, v, batch, seq)
    qkv_dil = _dil_proj(x2, attn_norm_g, wdil, batch, seq)
    o_dil, lse_dil = [], []
    for gi, (_, dil) in enumerate(DIL_PATTERNS):
        o_g, lse_g = _dilated(qkv_dil[gi], slopes, gi, batch, seq)
        nb = seq // dil // DIL_BLOCK
        o_dil.append(o_g)
        lse_g = lse_g.reshape(DIL_HEADS_PER_GROUP, batch, dil, nb, DIL_BLOCK)
        lse_dil.append(lse_g.transpose(1, 3, 4, 2, 0).reshape(t, DIL_HEADS_PER_GROUP))
    lse = jnp.concatenate(lse_dil, axis=1)
    x1, h2 = _merge(x2, attn_norm_g, wgate, b_gate, o_mla, o_dil, lse,
                    w_o_mla[0].astype(BF), w_o_dil[0].astype(BF), w_out[0].astype(BF), ffn_norm_g, seq)

    act = _ffn_up(h2, wu, cw, seq)
    y = _ffn_down(act, wd, x1, final_norm_g.reshape(1, D_MODEL))
    return y.reshape(batch, seq, D_MODEL)
```

```python
import functools
import math

import numpy as np
import jax
import jax.numpy as jnp
from jax import lax
from jax.experimental import pallas as pl
from jax.experimental.pallas import tpu as pltpu

D_MODEL = 2048
MLA_HEADS = 8
QK_NOPE_DIM = 128
QK_ROPE_DIM = 64
V_HEAD_DIM = 128
Q_LORA_RANK = 512
KV_LORA_RANK = 256
ROPE_THETA = 10000.0
DIL_PATTERNS = ((128, 1), (512, 4), (2048, 16))
DIL_GROUPS = 3
DIL_HEADS_PER_GROUP = 4
DIL_HEADS = DIL_GROUPS * DIL_HEADS_PER_GROUP
DIL_HEAD_DIM = 128
DIL_BLOCK = 128
ALIBI_MAX_BIAS = 8.0
D_FF = 5504
CONV_WIDTH = 3
NORM_EPS = 1e-6

LANE = 128
SUBLANE = 8
VMEM_LIMIT = 56 * 1024 * 1024

LAT_DIM = Q_LORA_RANK + KV_LORA_RANK + 2 * QK_ROPE_DIM
QK_SLOT = 2 * LANE
V_SLOT = 2 * LANE
DIL_GROUP_DIM = 3 * DIL_HEADS_PER_GROUP * DIL_HEAD_DIM
FF_CHUNK = 512
FFN_TM = 1024
FFN_DOWN_TM = 256
FFN_PIECE = 256
FFN_NSPLIT = 1024
FFN_ROWS = 32
D_FF_PAD = -(-D_FF // FF_CHUNK) * FF_CHUNK
FF_CHUNKS = D_FF_PAD // FF_CHUNK
NEG = -1e30

TM = 512
TQ = 512
TK = 512
FLASH_HEADS = 2
DIL_BATCH = 8

BF = jnp.bfloat16
F32 = jnp.float32


def _params(*sem):
    return pltpu.CompilerParams(dimension_semantics=sem, vmem_limit_bytes=VMEM_LIMIT)


def _rms(x, g):
    return x * lax.rsqrt(jnp.mean(x * x, axis=-1, keepdims=True) + NORM_EPS) * g


def _dot(a, b):
    return jnp.dot(a, b, preferred_element_type=F32)


def _dot_nt(a, b):
    return lax.dot_general(a, b, (((1,), (1,)), ((), ())), preferred_element_type=F32)


def _rope_pair(t):
    lane = lax.broadcasted_iota(jnp.int32, t.shape, 1)
    return jnp.where(lane < QK_ROPE_DIM, t + pltpu.roll(t, QK_ROPE_DIM, 1), 0.0)


def _mla_prep_kernel(x_ref, g_ref, wlat_ref, qg_ref, kvg_ref, wq_ref, wkv_ref, cs_ref,
                     q_ref, k_ref, v_ref):
    h = _rms(x_ref[...], g_ref[...]).astype(BF)
    lat = _dot(h, wlat_ref[...])
    cq = _rms(lat[:, :Q_LORA_RANK], qg_ref[...]).astype(BF)
    ckv = _rms(lat[:, Q_LORA_RANK:Q_LORA_RANK + KV_LORA_RANK], kvg_ref[...]).astype(BF)
    q = _dot(cq, wq_ref[...])
    kv = _dot(ckv, wkv_ref[...])
    cs = cs_ref[...]
    scale = (QK_NOPE_DIM + QK_ROPE_DIM) ** -0.5 * math.log2(math.e)
    k_rope = _rope_pair(lat[:, Q_LORA_RANK + KV_LORA_RANK:] * cs).astype(BF)
    ones = jnp.ones((x_ref.shape[0], V_SLOT - V_HEAD_DIM), BF)
    for hd in range(MLA_HEADS):
        qs = q[:, hd * QK_SLOT:(hd + 1) * QK_SLOT]
        q_ref[hd, :, :LANE] = (qs[:, :LANE] * scale).astype(BF)
        q_ref[hd, :, LANE:] = _rope_pair(qs[:, LANE:] * (cs * scale)).astype(BF)
        k_ref[hd, :, :LANE] = kv[:, hd * 256:hd * 256 + LANE].astype(BF)
        k_ref[hd, :, LANE:] = k_rope
        v_ref[hd, :, :V_HEAD_DIM] = kv[:, hd * 256 + LANE:(hd + 1) * 256].astype(BF)
        v_ref[hd, :, V_HEAD_DIM:] = ones


def _mla_prep(x2, g, wlat, qg, kvg, wq, wkv, cs, seq):
    t = x2.shape[0]
    nseq = seq // TM
    const = lambda i: (0, 0)
    return pl.pallas_call(
        _mla_prep_kernel,
        grid=(t // TM,),
        in_specs=[
            pl.BlockSpec((TM, D_MODEL), lambda i: (i, 0)),
            pl.BlockSpec((1, D_MODEL), const),
            pl.BlockSpec((D_MODEL, LAT_DIM), const),
            pl.BlockSpec((1, Q_LORA_RANK), const),
            pl.BlockSpec((1, KV_LORA_RANK), const),
            pl.BlockSpec((Q_LORA_RANK, MLA_HEADS * QK_SLOT), const),
            pl.BlockSpec((KV_LORA_RANK, MLA_HEADS * 256), const),
            pl.BlockSpec((TM, LANE), lambda i: (i % nseq, 0)),
        ],
        out_specs=[
            pl.BlockSpec((MLA_HEADS, TM, QK_SLOT), lambda i: (0, i, 0)),
            pl.BlockSpec((MLA_HEADS, TM, QK_SLOT), lambda i: (0, i, 0)),
            pl.BlockSpec((MLA_HEADS, TM, V_SLOT), lambda i: (0, i, 0)),
        ],
        out_shape=[
            jax.ShapeDtypeStruct((MLA_HEADS, t, QK_SLOT), BF),
            jax.ShapeDtypeStruct((MLA_HEADS, t, QK_SLOT), BF),
            jax.ShapeDtypeStruct((MLA_HEADS, t, V_SLOT), BF),
        ],
        compiler_params=_params("arbitrary"),
        name="mla_prep",
    )(x2, g, wlat, qg, kvg, wq, wkv, cs)


def _dil_proj_kernel(x_ref, g_ref, w_ref, *rest):
    o_refs, h_ref, res_ref = rest[:DIL_GROUPS], rest[DIL_GROUPS], rest[DIL_GROUPS + 1]
    nqkv = 3 * DIL_HEADS_PER_GROUP
    h_ref[...] = _rms(x_ref[...], g_ref[...]).astype(BF)
    for gi, (_, dil) in enumerate(DIL_PATTERNS):
        sub = TM // dil
        res = _dot(h_ref[...], w_ref[:, gi * DIL_GROUP_DIM:(gi + 1) * DIL_GROUP_DIM])
        for hh in range(nqkv):
            blk = res[:, hh * DIL_HEAD_DIM:(hh + 1) * DIL_HEAD_DIM]
            if hh < DIL_HEADS_PER_GROUP:
                blk = blk * DIL_HEAD_DIM ** -0.5
            if dil == 1:
                o_refs[gi][hh, 0, 0] = blk.astype(BF)
            else:
                res_ref[hh] = blk
        if dil > 1:
            for hh in range(nqkv):
                for r in range(dil):
                    o_refs[gi][hh, 0, r] = res_ref[hh, pl.ds(r, sub, stride=dil), :].astype(BF)


def _dil_proj(x2, g, w, batch, seq):
    nseq = seq // TM
    nqkv = 3 * DIL_HEADS_PER_GROUP
    const = lambda i: (0, 0)
    return pl.pallas_call(
        _dil_proj_kernel,
        grid=(batch * nseq,),
        in_specs=[
            pl.BlockSpec((TM, D_MODEL), lambda i: (i, 0)),
            pl.BlockSpec((1, D_MODEL), const),
            pl.BlockSpec((D_MODEL, DIL_GROUPS * DIL_GROUP_DIM), const, pipeline_mode=pl.Buffered(1)),
        ],
        out_specs=[
            pl.BlockSpec((nqkv, 1, dil, TM // dil, DIL_HEAD_DIM), lambda i: (0, i // nseq, 0, i % nseq, 0))
            for _, dil in DIL_PATTERNS
        ],
        out_shape=[
            jax.ShapeDtypeStruct((nqkv, batch, dil, seq // dil, DIL_HEAD_DIM), BF)
            for _, dil in DIL_PATTERNS
        ],
        scratch_shapes=[pltpu.VMEM((TM, D_MODEL), BF), pltpu.VMEM((nqkv, TM, DIL_HEAD_DIM), F32)],
        compiler_params=_params("arbitrary"),
        name="dil_proj",
    )(x2, g, w)


def _flash_kernel(q_ref, k_ref, v_ref, o_ref, *scratch):
    nh = FLASH_HEADS
    m_refs, acc_refs = scratch[:nh], scratch[nh:2 * nh]
    s_refs = (scratch[2 * nh:3 * nh], scratch[3 * nh:4 * nh])
    qi = pl.program_id(2)
    for hh in range(nh):
        m_refs[hh][...] = jnp.full(m_refs[hh].shape, NEG, F32)
        acc_refs[hh][...] = jnp.zeros(acc_refs[hh].shape, F32)

    def key_rows(j):
        return pl.ds(pl.multiple_of(j * TK, TK), TK)

    def scores(j, slot):
        for hh in range(nh):
            s_refs[slot][hh][...] = _dot_nt(q_ref[hh], k_ref[hh, key_rows(j), :])

    def mask_diagonal(slot):
        keep = lax.broadcasted_iota(jnp.int32, (TQ, TK), 1) <= lax.broadcasted_iota(jnp.int32, (TQ, TK), 0)
        for hh in range(nh):
            s_refs[slot][hh][...] = jnp.where(keep, s_refs[slot][hh][...], NEG)

    def update(j, slot):
        for hh in range(nh):
            s = s_refs[slot][hh][...]
            m_old = m_refs[hh][...]
            m_new = jnp.maximum(m_old, jnp.max(s, axis=-1, keepdims=True))
            p = jnp.exp2(s - m_new)
            pv = _dot(p.astype(BF), v_ref[hh, key_rows(j), :])
            acc_refs[hh][...] = jnp.exp2(m_old - m_new) * acc_refs[hh][...] + pv
            m_refs[hh][...] = m_new

    scores(0, 0)

    @pl.when(qi == 0)
    def _():
        mask_diagonal(0)

    def body(j, carry):
        for slot in range(2):
            @pl.when(j % 2 == slot)
            def _(slot=slot):
                scores(j, slot)
                update(j - 1, 1 - slot)

            @pl.when((j % 2 == slot) & (j == qi))
            def _(slot=slot):
                mask_diagonal(slot)
        return carry

    lax.fori_loop(1, qi + 1, body, 0)

    for slot in range(2):
        @pl.when(qi % 2 == slot)
        def _(slot=slot):
            update(qi, slot)

    for hh in range(FLASH_HEADS):
        acc = acc_refs[hh][...]
        o_ref[hh] = (acc[:, :V_HEAD_DIM] * (1.0 / acc[:, V_HEAD_DIM:])).astype(BF)


def _mla_flash(q, k, v, batch, seq):
    nq = seq // TQ
    t = batch * seq
    hb = FLASH_HEADS
    return pl.pallas_call(
        _flash_kernel,
        grid=(batch, MLA_HEADS // hb, nq),
        in_specs=[
            pl.BlockSpec((hb, TQ, QK_SLOT), lambda b, h, i: (h, b * nq + i, 0)),
            pl.BlockSpec((hb, seq, QK_SLOT), lambda b, h, i: (h, b, 0)),
            pl.BlockSpec((hb, seq, V_SLOT), lambda b, h, i: (h, b, 0)),
        ],
        out_specs=pl.BlockSpec((hb, TQ, V_HEAD_DIM), lambda b, h, i: (h, b * nq + i, 0)),
        out_shape=jax.ShapeDtypeStruct((MLA_HEADS, t, V_HEAD_DIM), BF),
        scratch_shapes=([pltpu.VMEM((TQ, 1), F32)] * hb + [pltpu.VMEM((TQ, V_SLOT), F32)] * hb
                        + [pltpu.VMEM((TQ, TK), F32)] * (2 * hb)),
        compiler_params=_params("arbitrary", "arbitrary", "arbitrary"),
        name="mla_flash",
    )(q, k, v)


def _dilated_kernel(slope_ref, q_ref, k_ref, v_ref, o_ref, lse_ref, *, group, dil, nb):
    hs = pl.program_id(1)
    slope = slope_ref[group * DIL_HEADS_PER_GROUP + hs] * float(dil)
    qi = lax.broadcasted_iota(jnp.int32, (DIL_BLOCK, DIL_BLOCK), 0)
    ki = lax.broadcasted_iota(jnp.int32, (DIL_BLOCK, DIL_BLOCK), 1)
    dist_cur = (qi - ki).astype(F32)
    dist_prev = dist_cur + float(DIL_BLOCK)
    w_sub = float(DIL_PATTERNS[group][0] // dil)
    bias_cur = jnp.where((dist_cur >= 0.0) & (dist_cur <= w_sub), -slope * dist_cur, NEG)
    bias_prev = jnp.where(dist_prev <= w_sub, -slope * dist_prev, NEG)
    neg = jnp.full((DIL_BLOCK, DIL_BLOCK), NEG, F32)
    bias_inner = jnp.concatenate([bias_prev, bias_cur], axis=1)
    bias_first = jnp.concatenate([bias_cur, neg], axis=1)
    eye = qi == ki

    def window(n):
        start = max(n - 1, 0) * DIL_BLOCK
        return slice(start, start + 2 * DIL_BLOCK)

    blocks = [(r, n) for r in range(dil) for n in range(nb)]
    for g0 in range(0, len(blocks), DIL_BATCH):
        batch = blocks[g0:g0 + DIL_BATCH]
        scores = [_dot_nt(q_ref[0, 0, r, n * DIL_BLOCK:(n + 1) * DIL_BLOCK, :], k_ref[0, 0, r, window(n), :])
                  + (bias_first if n == 0 else bias_inner) for r, n in batch]
        maxes = [jnp.max(s, axis=-1, keepdims=True) for s in scores]
        probs = [jnp.exp(s - m) for s, m in zip(scores, maxes)]
        sums = [jnp.sum(p, axis=-1, keepdims=True) for p in probs]
        outs = [_dot(p.astype(BF), v_ref[0, 0, r, window(n), :]) for p, (r, n) in zip(probs, batch)]
        for (r, n), o, l in zip(batch, outs, sums):
            o_ref[0, 0, r, n * DIL_BLOCK:(n + 1) * DIL_BLOCK, :] = (o * (1.0 / l)).astype(BF)
        rows = [jnp.sum(jnp.where(eye, m + jnp.log(l), 0.0), axis=0, keepdims=True) for m, l in zip(maxes, sums)]
        lse_ref[0, 0, g0:g0 + DIL_BATCH, :] = jnp.concatenate(rows, axis=0)


def _dilated(qkv, slopes, group, batch, seq):
    _, dil = DIL_PATTERNS[group]
    sub = seq // dil
    nb = sub // DIL_BLOCK
    hg = DIL_HEADS_PER_GROUP
    blk = (1, 1, dil, sub, DIL_HEAD_DIM)

    def head(which):
        return lambda b, h: (which * hg + h, b, 0, 0, 0)

    return pl.pallas_call(
        functools.partial(_dilated_kernel, group=group, dil=dil, nb=nb),
        grid=(batch, hg),
        in_specs=[
            pl.BlockSpec(memory_space=pltpu.SMEM),
            pl.BlockSpec(blk, head(0)),
            pl.BlockSpec(blk, head(1)),
            pl.BlockSpec(blk, head(2)),
        ],
        out_specs=[
            pl.BlockSpec(blk, head(0)),
            pl.BlockSpec((1, 1, dil * nb, DIL_BLOCK), lambda b, h: (h, b, 0, 0)),
        ],
        out_shape=[
            jax.ShapeDtypeStruct((hg, batch, dil, sub, DIL_HEAD_DIM), BF),
            jax.ShapeDtypeStruct((hg, batch, dil * nb, DIL_BLOCK), F32),
        ],
        compiler_params=_params("arbitrary", "arbitrary"),
        name=f"dilated_d{dil}",
    )(slopes, qkv, qkv, qkv)


MERGE_TN = 512
MERGE_TM = 256


def _merge_kernel(x_ref, g_ref, wg_ref, bg_ref, oa_ref, o1_ref, o4_ref, o16_ref, lse_ref,
                  woa_ref, wob_ref, wout_ref, fg_ref, y_ref, h2_ref, h_ref, a_ref, b_ref, mg_ref, nat_ref):
    x = x_ref[...]
    tm = x.shape[0]
    h_ref[...] = _rms(x, g_ref[...]).astype(BF)
    for hd in range(MLA_HEADS):
        a_ref[:, hd * V_HEAD_DIM:(hd + 1) * V_HEAD_DIM] = oa_ref[hd]
    lse = lse_ref[...]
    o_refs = (o1_ref, o4_ref, o16_ref)
    for hs in range(DIL_HEADS_PER_GROUP):
        ls = [lse[:, gi * DIL_HEADS_PER_GROUP + hs:gi * DIL_HEADS_PER_GROUP + hs + 1] for gi in range(DIL_GROUPS)]
        mx = jnp.maximum(jnp.maximum(ls[0], ls[1]), ls[2])
        es = [jnp.exp(v - mx) for v in ls]
        inv = 1.0 / (es[0] + es[1] + es[2])
        comb = (es[0] * inv) * o_refs[0][hs, 0, 0].astype(F32)
        for gi in range(1, DIL_GROUPS):
            dil = DIL_PATTERNS[gi][1]
            for r in range(dil):
                nat_ref[pl.ds(r, tm // dil, stride=dil), :] = o_refs[gi][hs, 0, r].astype(F32)
            comb = comb + (es[gi] * inv) * nat_ref[...]
        b_ref[:, hs * DIL_HEAD_DIM:(hs + 1) * DIL_HEAD_DIM] = comb.astype(BF)
    for c in range(D_MODEL // MERGE_TN):
        cols = slice(c * MERGE_TN, (c + 1) * MERGE_TN)
        cols_b = slice(D_MODEL + c * MERGE_TN, D_MODEL + (c + 1) * MERGE_TN)
        gate_a = jax.nn.sigmoid(_dot(h_ref[...], wg_ref[:, cols]) + bg_ref[:, cols])
        gate_b = jax.nn.sigmoid(_dot(h_ref[...], wg_ref[:, cols_b]) + bg_ref[:, cols_b])
        o_a = _dot(a_ref[...], woa_ref[:, cols])
        o_b = _dot(b_ref[...], wob_ref[:, cols])
        mg_ref[:, cols] = (gate_a * o_a + gate_b * o_b).astype(BF)
    y = x + _dot(mg_ref[...], wout_ref[...])
    y_ref[...] = y
    h2_ref[...] = _rms(y, fg_ref[...]).astype(BF)


def _merge(x2, g, wg, bg, oa, o_dil, lse, woa, wob, wout, fg, seq):
    t = x2.shape[0]
    tm = MERGE_TM
    nseq = seq // tm
    const = lambda i: (0, 0)
    one = pl.Buffered(1)
    dil_specs = [
        pl.BlockSpec((DIL_HEADS_PER_GROUP, 1, dil, tm // dil, DIL_HEAD_DIM),
                     lambda i: (0, i // nseq, 0, i % nseq, 0))
        for _, dil in DIL_PATTERNS
    ]
    return pl.pallas_call(
        _merge_kernel,
        grid=(t // tm,),
        in_specs=[
            pl.BlockSpec((tm, D_MODEL), lambda i: (i, 0)),
            pl.BlockSpec((1, D_MODEL), const),
            pl.BlockSpec((D_MODEL, 2 * D_MODEL), const, pipeline_mode=one),
            pl.BlockSpec((1, 2 * D_MODEL), const),
            pl.BlockSpec((MLA_HEADS, tm, V_HEAD_DIM), lambda i: (0, i, 0)),
            *dil_specs,
            pl.BlockSpec((tm, DIL_HEADS), lambda i: (i, 0)),
            pl.BlockSpec((MLA_HEADS * V_HEAD_DIM, D_MODEL), const, pipeline_mode=one),
            pl.BlockSpec((DIL_HEADS_PER_GROUP * DIL_HEAD_DIM, D_MODEL), const, pipeline_mode=one),
            pl.BlockSpec((D_MODEL, D_MODEL), const, pipeline_mode=one),
            pl.BlockSpec((1, D_MODEL), const),
        ],
        out_specs=[pl.BlockSpec((tm, D_MODEL), lambda i: (i, 0)), pl.BlockSpec((tm, D_MODEL), lambda i: (i, 0))],
        out_shape=[jax.ShapeDtypeStruct((t, D_MODEL), F32), jax.ShapeDtypeStruct((t, D_MODEL), BF)],
        scratch_shapes=[
            pltpu.VMEM((tm, D_MODEL), BF),
            pltpu.VMEM((tm, MLA_HEADS * V_HEAD_DIM), BF),
            pltpu.VMEM((tm, DIL_HEADS_PER_GROUP * DIL_HEAD_DIM), BF),
            pltpu.VMEM((tm, D_MODEL), BF),
            pltpu.VMEM((tm, DIL_HEAD_DIM), F32),
        ],
        compiler_params=_params("arbitrary"),
        name="merge",
    )(x2, g, wg, bg, oa, *o_dil, lse, woa, wob, wout, fg)


def _ffn_up_kernel(h_ref, w_ref, cw_ref, a_ref, u0_ref, u1_ref, *, tiles, tiles_per_seq):
    i = pl.program_id(1)
    tm = h_ref.shape[0]
    u_refs = (u0_ref, u1_ref)

    def project(slot, r0, n):
        rows = slice(SUBLANE + r0, SUBLANE + r0 + FFN_PIECE)
        cols = slice(n * FFN_NSPLIT, (n + 1) * FFN_NSPLIT)
        u_refs[slot][rows, cols] = _dot(h_ref[r0:r0 + FFN_PIECE, :], w_ref[:, cols])

    def activate(slot, r1):
        u_ref = u_refs[slot]
        groups = FFN_ROWS // SUBLANE

        def conv(cols):
            window = u_ref[r1:r1 + FFN_ROWS + SUBLANE, cols]
            out = None
            for tap in range(CONV_WIDTH):
                back = CONV_WIDTH - 1 - tap
                shifted = window if back == 0 else pltpu.roll(window, back, 0)
                x = shifted[SUBLANE:].reshape(groups, SUBLANE, FF_CHUNK)
                term = cw_ref[tap * SUBLANE:(tap + 1) * SUBLANE, cols][None] * x
                out = term if out is None else out + term
            out = out + cw_ref[CONV_WIDTH * SUBLANE:(CONV_WIDTH + 1) * SUBLANE, cols][None]
            return out.reshape(FFN_ROWS, FF_CHUNK)

        up = conv(slice(0, FF_CHUNK))
        gate = conv(slice(FF_CHUNK, 2 * FF_CHUNK))
        a_ref[r1:r1 + FFN_ROWS, :] = (jax.nn.silu(gate) * up).astype(BF)

    def run(project_slot=None, activate_slot=None):
        if project_slot is not None:
            seq_start = (i % tiles_per_seq) == 0
            halo = u_refs[1 - project_slot][tm:]
            u_refs[project_slot][:SUBLANE] = jnp.where(seq_start, 0.0, halo)
        nsplit = 2 * FF_CHUNK // FFN_NSPLIT
        per_dot = FFN_PIECE // FFN_ROWS // nsplit
        for r0 in range(0, tm, FFN_PIECE):
            for n in range(nsplit):
                if project_slot is not None:
                    project(project_slot, r0, n)
                if activate_slot is not None:
                    for k in range(per_dot):
                        activate(activate_slot, r0 + (n * per_dot + k) * FFN_ROWS)

    @pl.when(i == 0)
    def _():
        u0_ref[:SUBLANE] = jnp.zeros((SUBLANE, 2 * FF_CHUNK), F32)
        for r0 in range(0, tm, FFN_PIECE):
            for n in range(2 * FF_CHUNK // FFN_NSPLIT):
                project(0, r0, n)

    for slot in range(2):
        @pl.when((i > 0) & (i < tiles) & (i % 2 == slot))
        def _(slot=slot):
            run(project_slot=slot, activate_slot=1 - slot)

    @pl.when(i == tiles)
    def _():
        run(activate_slot=(tiles - 1) % 2)


def _ffn_down_kernel(a_ref, wd_ref, x_ref, fg_ref, y_ref):
    y_ref[...] = _rms(x_ref[...] + _dot(a_ref[...], wd_ref[...]), fg_ref[...])


def _ffn_up(h2, wu, cw, seq):
    t = h2.shape[0]
    tm = FFN_TM
    tiles = t // tm
    return pl.pallas_call(
        functools.partial(_ffn_up_kernel, tiles=tiles, tiles_per_seq=seq // tm),
        grid=(FF_CHUNKS, tiles + 1),
        in_specs=[
            pl.BlockSpec((tm, D_MODEL), lambda c, i: (jnp.minimum(i, tiles - 1), 0)),
            pl.BlockSpec((D_MODEL, 2 * FF_CHUNK), lambda c, i: (0, c)),
            pl.BlockSpec(((CONV_WIDTH + 1) * SUBLANE, 2 * FF_CHUNK), lambda c, i: (0, c)),
        ],
        out_specs=pl.BlockSpec((tm, FF_CHUNK), lambda c, i: (jnp.maximum(i - 1, 0), c)),
        out_shape=jax.ShapeDtypeStruct((t, D_FF_PAD), BF),
        scratch_shapes=[
            pltpu.VMEM((tm + SUBLANE, 2 * FF_CHUNK), F32),
            pltpu.VMEM((tm + SUBLANE, 2 * FF_CHUNK), F32),
        ],
        compiler_params=_params("arbitrary", "arbitrary"),
        name="ffn_up",
    )(h2, wu, cw)


def _ffn_down(act, wd, x1, fg):
    t = x1.shape[0]
    tm = FFN_DOWN_TM
    return pl.pallas_call(
        _ffn_down_kernel,
        grid=(t // tm,),
        in_specs=[
            pl.BlockSpec((tm, D_FF_PAD), lambda i: (i, 0)),
            pl.BlockSpec((D_FF_PAD, D_MODEL), lambda i: (0, 0), pipeline_mode=pl.Buffered(1)),
            pl.BlockSpec((tm, D_MODEL), lambda i: (i, 0)),
            pl.BlockSpec((1, D_MODEL), lambda i: (0, 0)),
        ],
        out_specs=pl.BlockSpec((tm, D_MODEL), lambda i: (i, 0)),
        out_shape=jax.ShapeDtypeStruct((t, D_MODEL), F32),
        compiler_params=_params("arbitrary"),
        name="ffn_down",
    )(act, wd, x1, fg)


def _rotate_half_cols(w):
    half = w.shape[-1] // 2
    return jnp.concatenate([-w[..., half:], w[..., :half]], axis=-1)


PREP_PIECE = 512
PREP_ROWS = 256


def _relayout_kernel(src_ref, *out_refs, plans):
    ncols = src_ref.shape[1]
    for out_ref, plan in zip(out_refs, plans):
        col = 0
        for start, width in plan:
            for off in range(0, width, PREP_PIECE):
                w = min(PREP_PIECE, width - off)
                dst = slice(col + off, col + off + w)
                if start is None:
                    out_ref[:, dst] = jnp.zeros((out_ref.shape[0], w), out_ref.dtype)
                    continue
                lead = (start + off) % LANE
                lo = start + off - lead
                hi = min(-(-(start + off + w) // LANE) * LANE, ncols)
                out_ref[:, dst] = src_ref[:, lo:hi][:, lead:lead + w].astype(out_ref.dtype)
            col += width


def _relayout(src, plans, dtype, row_tile, name):
    rows, ncols = src.shape
    widths = [sum(w for _, w in plan) for plan in plans]
    assert rows % row_tile == 0 and all(w % LANE == 0 for plan in plans for _, w in plan)
    outs = pl.pallas_call(
        functools.partial(_relayout_kernel, plans=plans),
        grid=(rows // row_tile,),
        in_specs=[pl.BlockSpec((row_tile, ncols), lambda i: (i, 0))],
        out_specs=[pl.BlockSpec((row_tile, w), lambda i: (i, 0)) for w in widths],
        out_shape=[jax.ShapeDtypeStruct((rows, w), dtype) for w in widths],
        compiler_params=_params("arbitrary"),
        name=name,
    )(src)
    return outs


def _ff_chunk_plan():
    plan = []
    for c in range(FF_CHUNKS):
        lo, hi = c * FF_CHUNK, min((c + 1) * FF_CHUNK, D_FF)
        for half in (0, D_FF):
            plan.append((half + lo, hi - lo))
            if hi - lo < FF_CHUNK:
                plan.append((None, FF_CHUNK - (hi - lo)))
    return plan


def kernel(x, attn_norm_g, w_in, b_gate, q_norm_g, w_uq, kv_norm_g, w_ukv, w_o_mla, w_o_dil,
           w_out, ffn_norm_g, w_up, conv_w, conv_b, w_down, final_norm_g):
    batch, seq, _ = x.shape
    assert w_in.shape[0] == 1, "single-layer block"
    assert seq % TM == 0 and seq % FFN_TM == 0 and seq % TQ == 0 and TQ == TK and seq % (16 * DIL_BLOCK) == 0
    assert FF_CHUNKS >= 3
    t = batch * seq
    x2 = x.reshape(t, D_MODEL)

    dqkv = DIL_HEADS * DIL_HEAD_DIM
    o = np.cumsum((0, Q_LORA_RANK, KV_LORA_RANK, QK_ROPE_DIM, dqkv, dqkv, dqkv, D_MODEL, D_MODEL))
    wi = w_in[0]
    w_head = lax.slice(wi, (0, 0), (D_MODEL, -(-int(o[3]) // LANE) * LANE))
    w_kpe = w_head[:, o[2]:o[3]]
    wlat = jnp.concatenate([w_head[:, o[0]:o[2]], w_kpe, _rotate_half_cols(w_kpe)], axis=1).astype(BF)
    gw = DIL_HEADS_PER_GROUP * DIL_HEAD_DIM
    dil_plan = [(int(o[3 + which]) + gi * gw, gw) for gi in range(DIL_GROUPS) for which in range(3)]
    wdil, wgate = _relayout(wi, [dil_plan, [(int(o[6]), 2 * D_MODEL)]], BF, PREP_ROWS, "relayout_w_in")
    wq = w_uq[0].astype(BF).reshape(Q_LORA_RANK, MLA_HEADS, QK_NOPE_DIM + QK_ROPE_DIM)
    wq_pe = wq[..., QK_NOPE_DIM:]
    wq = jnp.concatenate([wq[..., :QK_NOPE_DIM], wq_pe, _rotate_half_cols(wq_pe)], axis=-1)
    wq = wq.reshape(Q_LORA_RANK, MLA_HEADS * QK_SLOT)
    wkv = w_ukv[0].astype(BF)
    (wu,) = _relayout(w_up[0], [_ff_chunk_plan()], BF, PREP_ROWS, "relayout_w_up")
    cw = jnp.repeat(jnp.concatenate([conv_w[0], conv_b], axis=0), SUBLANE, axis=0)
    (cw,) = _relayout(cw, [_ff_chunk_plan()], F32, cw.shape[0], "relayout_conv")
    wd =jnp.concatenate([w_down[0].astype(BF), jnp.zeros((D_FF_PAD - D_FF, D_MODEL), BF)], axis=0)

    pos = jnp.arange(seq, dtype=F32)
    inv_freq = ROPE_THETA ** (-jnp.arange(0, QK_ROPE_DIM, 2, dtype=F32) / QK_ROPE_DIM)
    ang = pos[:, None] * inv_freq[None, :]
    cs = jnp.concatenate([jnp.cos(ang), jnp.cos(ang), jnp.sin(ang), jnp.sin(ang)], axis=1)
    slopes = 2.0 ** (-ALIBI_MAX_BIAS * jnp.arange(1, DIL_HEADS + 1, dtype=F32) / DIL_HEADS)

    q, k, v = _mla_prep(x2, attn_norm_g, wlat, q_norm_g, kv_norm_g, wq, wkv, cs, seq)
    o_mla = _mla_flash(q, k, v, batch, seq)
    qkv_dil = _dil_proj(x2, attn_norm_g, wdil, batch, seq)
    o_dil, lse_dil = [], []
    for gi, (_, dil) in enumerate(DIL_PATTERNS):
        o_g, lse_g = _dilated(qkv_dil[gi], slopes, gi, batch, seq)
        nb = seq // dil // DIL_BLOCK
        o_dil.append(o_g)
        lse_g = lse_g.reshape(DIL_HEADS_PER_GROUP, batch, dil, nb, DIL_BLOCK)
        lse_dil.append(lse_g.transpose(1, 3, 4, 2, 0).reshape(t, DIL_HEADS_PER_GROUP))
    lse = jnp.concatenate(lse_dil, axis=1)
    x1, h2 = _merge(x2, attn_norm_g, wgate, b_gate, o_mla, o_dil, lse,
                    w_o_mla[0].astype(BF), w_o_dil[0].astype(BF), w_out[0].astype(BF), ffn_norm_g, seq)

    act = _ffn_up(h2, wu, cw, seq)
    y = _ffn_down(act, wd, x1, final_norm_g.reshape(1, D_MODEL))
    return y.reshape(batch, seq, D_MODEL)
```

```python
import functools
import math

import numpy as np
import jax
import jax.numpy as jnp
from jax import lax
from jax.experimental import pallas as pl
from jax.experimental.pallas import tpu as pltpu

D_MODEL = 2048
MLA_HEADS = 8
QK_NOPE_DIM = 128
QK_ROPE_DIM = 64
V_HEAD_DIM = 128
Q_LORA_RANK = 512
KV_LORA_RANK = 256
ROPE_THETA = 10000.0
DIL_PATTERNS = ((128, 1), (512, 4), (2048, 16))
DIL_GROUPS = 3
DIL_HEADS_PER_GROUP = 4
DIL_HEADS = DIL_GROUPS * DIL_HEADS_PER_GROUP
DIL_HEAD_DIM = 128
DIL_BLOCK = 128
ALIBI_MAX_BIAS = 8.0
D_FF = 5504
CONV_WIDTH = 3
NORM_EPS = 1e-6

LANE = 128
SUBLANE = 8
VMEM_LIMIT = 56 * 1024 * 1024

LAT_DIM = Q_LORA_RANK + KV_LORA_RANK + 2 * QK_ROPE_DIM
QK_SLOT = 2 * LANE
V_SLOT = 2 * LANE
DIL_GROUP_DIM = 3 * DIL_HEADS_PER_GROUP * DIL_HEAD_DIM
FF_CHUNK = 512
FFN_TM = 1024
FFN_DOWN_TM = 256
FFN_PIECE = 256
FFN_NSPLIT = 1024
FFN_ROWS = 32
D_FF_PAD = -(-D_FF // FF_CHUNK) * FF_CHUNK
FF_CHUNKS = D_FF_PAD // FF_CHUNK
NEG = -1e30

TM = 512
TQ = 512
TK = 512
FLASH_HEADS = 2
DIL_BATCH = 8

BF = jnp.bfloat16
F32 = jnp.float32


def _params(*sem):
    return pltpu.CompilerParams(dimension_semantics=sem, vmem_limit_bytes=VMEM_LIMIT)


def _rms(x, g):
    return x * lax.rsqrt(jnp.mean(x * x, axis=-1, keepdims=True) + NORM_EPS) * g


def _dot(a, b):
    return jnp.dot(a, b, preferred_element_type=F32)


def _dot_nt(a, b):
    return lax.dot_general(a, b, (((1,), (1,)), ((), ())), preferred_element_type=F32)


def _rope_pair(t):
    lane = lax.broadcasted_iota(jnp.int32, t.shape, 1)
    return jnp.where(lane < QK_ROPE_DIM, t + pltpu.roll(t, QK_ROPE_DIM, 1), 0.0)


def _mla_prep_kernel(x_ref, g_ref, wlat_ref, qg_ref, kvg_ref, wq_ref, wkv_ref, cs_ref,
                     q_ref, k_ref, v_ref):
    h = _rms(x_ref[...], g_ref[...]).astype(BF)
    lat = _dot(h, wlat_ref[...])
    cq = _rms(lat[:, :Q_LORA_RANK], qg_ref[...]).astype(BF)
    ckv = _rms(lat[:, Q_LORA_RANK:Q_LORA_RANK + KV_LORA_RANK], kvg_ref[...]).astype(BF)
    q = _dot(cq, wq_ref[...])
    kv = _dot(ckv, wkv_ref[...])
    cs = cs_ref[...]
    scale = (QK_NOPE_DIM + QK_ROPE_DIM) ** -0.5 * math.log2(math.e)
    k_rope = _rope_pair(lat[:, Q_LORA_RANK + KV_LORA_RANK:] * cs).astype(BF)
    ones = jnp.ones((x_ref.shape[0], V_SLOT - V_HEAD_DIM), BF)
    for hd in range(MLA_HEADS):
        qs = q[:, hd * QK_SLOT:(hd + 1) * QK_SLOT]
        q_ref[hd, :, :LANE] = (qs[:, :LANE] * scale).astype(BF)
        q_ref[hd, :, LANE:] = _rope_pair(qs[:, LANE:] * (cs * scale)).astype(BF)
        k_ref[hd, :, :LANE] = kv[:, hd * 256:hd * 256 + LANE].astype(BF)
        k_ref[hd, :, LANE:] = k_rope
        v_ref[hd, :, :V_HEAD_DIM] = kv[:, hd * 256 + LANE:(hd + 1) * 256].astype(BF)
        v_ref[hd, :, V_HEAD_DIM:] = ones


def _mla_prep(x2, g, wlat, qg, kvg, wq, wkv, cs, seq):
    t = x2.shape[0]
    nseq = seq // TM
    const = lambda i: (0, 0)
    return pl.pallas_call(
        _mla_prep_kernel,
        grid=(t // TM,),
        in_specs=[
            pl.BlockSpec((TM, D_MODEL), lambda i: (i, 0)),
            pl.BlockSpec((1, D_MODEL), const),
            pl.BlockSpec((D_MODEL, LAT_DIM), const),
            pl.BlockSpec((1, Q_LORA_RANK), const),
            pl.BlockSpec((1, KV_LORA_RANK), const),
            pl.BlockSpec((Q_LORA_RANK, MLA_HEADS * QK_SLOT), const),
            pl.BlockSpec((KV_LORA_RANK, MLA_HEADS * 256), const),
            pl.BlockSpec((TM, LANE), lambda i: (i % nseq, 0)),
        ],
        out_specs=[
            pl.BlockSpec((MLA_HEADS, TM, QK_SLOT), lambda i: (0, i, 0)),
            pl.BlockSpec((MLA_HEADS, TM, QK_SLOT), lambda i: (0, i, 0)),
            pl.BlockSpec((MLA_HEADS, TM, V_SLOT), lambda i: (0, i, 0)),
        ],
        out_shape=[
            jax.ShapeDtypeStruct((MLA_HEADS, t, QK_SLOT), BF),
            jax.ShapeDtypeStruct((MLA_HEADS, t, QK_SLOT), BF),
            jax.ShapeDtypeStruct((MLA_HEADS, t, V_SLOT), BF),
        ],
        compiler_params=_params("arbitrary"),
        name="mla_prep",
    )(x2, g, wlat, qg, kvg, wq, wkv, cs)


def _dil_proj_kernel(x_ref, g_ref, w_ref, *rest):
    o_refs, h_ref, res_ref = rest[:DIL_GROUPS], rest[DIL_GROUPS], rest[DIL_GROUPS + 1]
    nqkv = 3 * DIL_HEADS_PER_GROUP
    h_ref[...] = _rms(x_ref[...], g_ref[...]).astype(BF)
    for gi, (_, dil) in enumerate(DIL_PATTERNS):
        sub = TM // dil
        res = _dot(h_ref[...], w_ref[:, gi * DIL_GROUP_DIM:(gi + 1) * DIL_GROUP_DIM])
        for hh in range(nqkv):
            blk = res[:, hh * DIL_HEAD_DIM:(hh + 1) * DIL_HEAD_DIM]
            if hh < DIL_HEADS_PER_GROUP:
                blk = blk * DIL_HEAD_DIM ** -0.5
            if dil == 1:
                o_refs[gi][hh, 0, 0] = blk.astype(BF)
            else:
                res_ref[hh] = blk
        if dil > 1:
            for hh in range(nqkv):
                for r in range(dil):
                    o_refs[gi][hh, 0, r] = res_ref[hh, pl.ds(r, sub, stride=dil), :].astype(BF)


def _dil_proj(x2, g, w, batch, seq):
    nseq = seq // TM
    nqkv = 3 * DIL_HEADS_PER_GROUP
    const = lambda i: (0, 0)
    return pl.pallas_call(
        _dil_proj_kernel,
        grid=(batch * nseq,),
        in_specs=[
            pl.BlockSpec((TM, D_MODEL), lambda i: (i, 0)),
            pl.BlockSpec((1, D_MODEL), const),
            pl.BlockSpec((D_MODEL, DIL_GROUPS * DIL_GROUP_DIM), const, pipeline_mode=pl.Buffered(1)),
        ],
        out_specs=[
            pl.BlockSpec((nqkv, 1, dil, TM // dil, DIL_HEAD_DIM), lambda i: (0, i // nseq, 0, i % nseq, 0))
            for _, dil in DIL_PATTERNS
        ],
        out_shape=[
            jax.ShapeDtypeStruct((nqkv, batch, dil, seq // dil, DIL_HEAD_DIM), BF)
            for _, dil in DIL_PATTERNS
        ],
        scratch_shapes=[pltpu.VMEM((TM, D_MODEL), BF), pltpu.VMEM((nqkv, TM, DIL_HEAD_DIM), F32)],
        compiler_params=_params("arbitrary"),
        name="dil_proj",
    )(x2, g, w)


def _flash_kernel(q_ref, k_ref, v_ref, o_ref, *scratch):
    nh = FLASH_HEADS
    m_refs, acc_refs = scratch[:nh], scratch[nh:2 * nh]
    s_refs = (scratch[2 * nh:3 * nh], scratch[3 * nh:4 * nh])
    qi = pl.program_id(2)
    for hh in range(nh):
        m_refs[hh][...] = jnp.full(m_refs[hh].shape, NEG, F32)
        acc_refs[hh][...] = jnp.zeros(acc_refs[hh].shape, F32)

    def key_rows(j):
        return pl.ds(pl.multiple_of(j * TK, TK), TK)

    def scores(j, slot):
        for hh in range(nh):
            s_refs[slot][hh][...] = _dot_nt(q_ref[hh], k_ref[hh, key_rows(j), :])

    def mask_diagonal(slot):
        keep = lax.broadcasted_iota(jnp.int32, (TQ, TK), 1) <= lax.broadcasted_iota(jnp.int32, (TQ, TK), 0)
        for hh in range(nh):
            s_refs[slot][hh][...] = jnp.where(keep, s_refs[slot][hh][...], NEG)

    def update(j, slot):
        for hh in range(nh):
            s = s_refs[slot][hh][...]
            m_old = m_refs[hh][...]
            m_new = jnp.maximum(m_old, jnp.max(s, axis=-1, keepdims=True))
            p = jnp.exp2(s - m_new)
            pv = _dot(p.astype(BF), v_ref[hh, key_rows(j), :])
            acc_refs[hh][...] = jnp.exp2(m_old - m_new) * acc_refs[hh][...] + pv
            m_refs[hh][...] = m_new

    scores(0, 0)

    @pl.when(qi == 0)
    def _():
        mask_diagonal(0)

    def body(j, carry):
        for slot in range(2):
            @pl.when(j % 2 == slot)
            def _(slot=slot):
                scores(j, slot)
                update(j - 1, 1 - slot)

            @pl.when((j % 2 == slot) & (j == qi))
            def _(slot=slot):
                mask_diagonal(slot)
        return carry

    lax.fori_loop(1, qi + 1, body, 0)

    for slot in range(2):
        @pl.when(qi % 2 == slot)
        def _(slot=slot):
            update(qi, slot)

    for hh in range(FLASH_HEADS):
        acc = acc_refs[hh][...]
        o_ref[hh] = (acc[:, :V_HEAD_DIM] * (1.0 / acc[:, V_HEAD_DIM:])).astype(BF)


def _mla_flash(q, k, v, batch, seq):
    nq = seq // TQ
    t = batch * seq
    hb = FLASH_HEADS
    return pl.pallas_call(
        _flash_kernel,
        grid=(batch, MLA_HEADS // hb, nq),
        in_specs=[
            pl.BlockSpec((hb, TQ, QK_SLOT), lambda b, h, i: (h, b * nq + i, 0)),
            pl.BlockSpec((hb, seq, QK_SLOT), lambda b, h, i: (h, b, 0)),
            pl.BlockSpec((hb, seq, V_SLOT), lambda b, h, i: (h, b, 0)),
        ],
        out_specs=pl.BlockSpec((hb, TQ, V_HEAD_DIM), lambda b, h, i: (h, b * nq + i, 0)),
        out_shape=jax.ShapeDtypeStruct((MLA_HEADS, t, V_HEAD_DIM), BF),
        scratch_shapes=([pltpu.VMEM((TQ, 1), F32)] * hb + [pltpu.VMEM((TQ, V_SLOT), F32)] * hb
                        + [pltpu.VMEM((TQ, TK), F32)] * (2 * hb)),
        compiler_params=_params("arbitrary", "arbitrary", "arbitrary"),
        name="mla_flash",
    )(q, k, v)


def _dilated_kernel(slope_ref, q_ref, k_ref, v_ref, o_ref, lse_ref, *, group, dil, nb):
    hs = pl.program_id(1)
    slope = slope_ref[group * DIL_HEADS_PER_GROUP + hs] * float(dil)
    qi = lax.broadcasted_iota(jnp.int32, (DIL_BLOCK, DIL_BLOCK), 0)
    ki = lax.broadcasted_iota(jnp.int32, (DIL_BLOCK, DIL_BLOCK), 1)
    dist_cur = (qi - ki).astype(F32)
    dist_prev = dist_cur + float(DIL_BLOCK)
    w_sub = float(DIL_PATTERNS[group][0] // dil)
    bias_cur = jnp.where((dist_cur >= 0.0) & (dist_cur <= w_sub), -slope * dist_cur, NEG)
    bias_prev = jnp.where(dist_prev <= w_sub, -slope * dist_prev, NEG)
    neg = jnp.full((DIL_BLOCK, DIL_BLOCK), NEG, F32)
    bias_inner = jnp.concatenate([bias_prev, bias_cur], axis=1)
    bias_first = jnp.concatenate([bias_cur, neg], axis=1)
    eye = qi == ki

    def window(n):
        start = max(n - 1, 0) * DIL_BLOCK
        return slice(start, start + 2 * DIL_BLOCK)

    blocks = [(r, n) for r in range(dil) for n in range(nb)]
    for g0 in range(0, len(blocks), DIL_BATCH):
        batch = blocks[g0:g0 + DIL_BATCH]
        scores = [_dot_nt(q_ref[0, 0, r, n * DIL_BLOCK:(n + 1) * DIL_BLOCK, :], k_ref[0, 0, r, window(n), :])
                  + (bias_first if n == 0 else bias_inner) for r, n in batch]
        maxes = [jnp.max(s, axis=-1, keepdims=True) for s in scores]
        probs = [jnp.exp(s - m) for s, m in zip(scores, maxes)]
        sums = [jnp.sum(p, axis=-1, keepdims=True) for p in probs]
        outs = [_dot(p.astype(BF), v_ref[0, 0, r, window(n), :]) for p, (r, n) in zip(probs, batch)]
        for (r, n), o, l in zip(batch, outs, sums):
            o_ref[0, 0, r, n * DIL_BLOCK:(n + 1) * DIL_BLOCK, :] = (o * (1.0 / l)).astype(BF)
        rows = [jnp.sum(jnp.where(eye, m + jnp.log(l), 0.0), axis=0, keepdims=True) for m, l in zip(maxes, sums)]
        lse_ref[0, 0, g0:g0 + DIL_BATCH, :] = jnp.concatenate(rows, axis=0)


def _dilated(qkv, slopes, group, batch, seq):
    _, dil = DIL_PATTERNS[group]
    sub = seq // dil
    nb = sub // DIL_BLOCK
    hg = DIL_HEADS_PER_GROUP
    blk = (1, 1, dil, sub, DIL_HEAD_DIM)

    def head(which):
        return lambda b, h: (which * hg + h, b, 0, 0, 0)

    return pl.pallas_call(
        functools.partial(_dilated_kernel, group=group, dil=dil, nb=nb),
        grid=(batch, hg),
        in_specs=[
            pl.BlockSpec(memory_space=pltpu.SMEM),
            pl.BlockSpec(blk, head(0)),
            pl.BlockSpec(blk, head(1)),
            pl.BlockSpec(blk, head(2)),
        ],
        out_specs=[
            pl.BlockSpec(blk, head(0)),
            pl.BlockSpec((1, 1, dil * nb, DIL_BLOCK), lambda b, h: (h, b, 0, 0)),
        ],
        out_shape=[
            jax.ShapeDtypeStruct((hg, batch, dil, sub, DIL_HEAD_DIM), BF),
            jax.ShapeDtypeStruct((hg, batch, dil * nb, DIL_BLOCK), F32),
        ],
        compiler_params=_params("arbitrary", "arbitrary"),
        name=f"dilated_d{dil}",
    )(slopes, qkv, qkv, qkv)


MERGE_TN = 512
MERGE_TM = 256


def _merge_kernel(x_ref, g_ref, wg_ref, bg_ref, oa_ref, o1_ref, o4_ref, o16_ref, lse_ref,
                  woa_ref, wob_ref, wout_ref, fg_ref, y_ref, h2_ref, h_ref, a_ref, b_ref, mg_ref, nat_ref):
    x = x_ref[...]
    tm = x.shape[0]
    h_ref[...] = _rms(x, g_ref[...]).astype(BF)
    for hd in range(MLA_HEADS):
        a_ref[:, hd * V_HEAD_DIM:(hd + 1) * V_HEAD_DIM] = oa_ref[hd]
    lse = lse_ref[...]
    o_refs = (o1_ref, o4_ref, o16_ref)
    for hs in range(DIL_HEADS_PER_GROUP):
        ls = [lse[:, gi * DIL_HEADS_PER_GROUP + hs:gi * DIL_HEADS_PER_GROUP + hs + 1] for gi in range(DIL_GROUPS)]
        mx = jnp.maximum(jnp.maximum(ls[0], ls[1]), ls[2])
        es = [jnp.exp(v - mx) for v in ls]
        inv = 1.0 / (es[0] + es[1] + es[2])
        comb = (es[0] * inv) * o_refs[0][hs, 0, 0].astype(F32)
        for gi in range(1, DIL_GROUPS):
            dil = DIL_PATTERNS[gi][1]
            for r in range(dil):
                nat_ref[pl.ds(r, tm // dil, stride=dil), :] = o_refs[gi][hs, 0, r].astype(F32)
            comb = comb + (es[gi] * inv) * nat_ref[...]
        b_ref[:, hs * DIL_HEAD_DIM:(hs + 1) * DIL_HEAD_DIM] = comb.astype(BF)
    for c in range(D_MODEL // MERGE_TN):
        cols = slice(c * MERGE_TN, (c + 1) * MERGE_TN)
        cols_b = slice(D_MODEL + c * MERGE_TN, D_MODEL + (c + 1) * MERGE_TN)
        gate_a = jax.nn.sigmoid(_dot(h_ref[...], wg_ref[:, cols]) + bg_ref[:, cols])
        gate_b = jax.nn.sigmoid(_dot(h_ref[...], wg_ref[:, cols_b]) + bg_ref[:, cols_b])
        o_a = _dot(a_ref[...], woa_ref[:, cols])
        o_b = _dot(b_ref[...], wob_ref[:, cols])
        mg_ref[:, cols] = (gate_a * o_a + gate_b * o_b).astype(BF)
    y = x + _dot(mg_ref[...], wout_ref[...])
    y_ref[...] = y
    h2_ref[...] = _rms(y, fg_ref[...]).astype(BF)


def _merge(x2, g, wg, bg, oa, o_dil, lse, woa, wob, wout, fg, seq):
    t = x2.shape[0]
    tm = MERGE_TM
    nseq = seq // tm
    const = lambda i: (0, 0)
    one = pl.Buffered(1)
    dil_specs = [
        pl.BlockSpec((DIL_HEADS_PER_GROUP, 1, dil, tm // dil, DIL_HEAD_DIM),
                     lambda i: (0, i // nseq, 0, i % nseq, 0))
        for _, dil in DIL_PATTERNS
    ]
    return pl.pallas_call(
        _merge_kernel,
        grid=(t // tm,),
        in_specs=[
            pl.BlockSpec((tm, D_MODEL), lambda i: (i, 0)),
            pl.BlockSpec((1, D_MODEL), const),
            pl.BlockSpec((D_MODEL, 2 * D_MODEL), const, pipeline_mode=one),
            pl.BlockSpec((1, 2 * D_MODEL), const),
            pl.BlockSpec((MLA_HEADS, tm, V_HEAD_DIM), lambda i: (0, i, 0)),
            *dil_specs,
            pl.BlockSpec((tm, DIL_HEADS), lambda i: (i, 0)),
            pl.BlockSpec((MLA_HEADS * V_HEAD_DIM, D_MODEL), const, pipeline_mode=one),
            pl.BlockSpec((DIL_HEADS_PER_GROUP * DIL_HEAD_DIM, D_MODEL), const, pipeline_mode=one),
            pl.BlockSpec((D_MODEL, D_MODEL), const, pipeline_mode=one),
            pl.BlockSpec((1, D_MODEL), const),
        ],
        out_specs=[pl.BlockSpec((tm, D_MODEL), lambda i: (i, 0)), pl.BlockSpec((tm, D_MODEL), lambda i: (i, 0))],
        out_shape=[jax.ShapeDtypeStruct((t, D_MODEL), F32), jax.ShapeDtypeStruct((t, D_MODEL), BF)],
        scratch_shapes=[
            pltpu.VMEM((tm, D_MODEL), BF),
            pltpu.VMEM((tm, MLA_HEADS * V_HEAD_DIM), BF),
            pltpu.VMEM((tm, DIL_HEADS_PER_GROUP * DIL_HEAD_DIM), BF),
            pltpu.VMEM((tm, D_MODEL), BF),
            pltpu.VMEM((tm, DIL_HEAD_DIM), F32),
        ],
        compiler_params=_params("arbitrary"),
        name="merge",
    )(x2, g, wg, bg, oa, *o_dil, lse, woa, wob, wout, fg)


def _ffn_up_kernel(h_ref, w_ref, cw_ref, a_ref, u0_ref, u1_ref, *, tiles, tiles_per_seq):
    i = pl.program_id(1)
    tm = h_ref.shape[0]
    u_refs = (u0_ref, u1_ref)

    def project(slot, r0, n):
        rows = slice(SUBLANE + r0, SUBLANE + r0 + FFN_PIECE)
        cols = slice(n * FFN_NSPLIT, (n + 1) * FFN_NSPLIT)
        u_refs[slot][rows, cols] = _dot(h_ref[r0:r0 + FFN_PIECE, :], w_ref[:, cols])

    def activate(slot, r1):
        u_ref = u_refs[slot]
        groups = FFN_ROWS // SUBLANE

        def conv(cols):
            window = u_ref[r1:r1 + FFN_ROWS + SUBLANE, cols]
            out = None
            for tap in range(CONV_WIDTH):
                back = CONV_WIDTH - 1 - tap
                shifted = window if back == 0 else pltpu.roll(window, back, 0)
                x = shifted[SUBLANE:].reshape(groups, SUBLANE, FF_CHUNK)
                term = cw_ref[tap * SUBLANE:(tap + 1) * SUBLANE, cols][None] * x
                out = term if out is None else out + term
            out = out + cw_ref[CONV_WIDTH * SUBLANE:(CONV_WIDTH + 1) * SUBLANE, cols][None]
            return out.reshape(FFN_ROWS, FF_CHUNK)

        up = conv(slice(0, FF_CHUNK))
        gate = conv(slice(FF_CHUNK, 2 * FF_CHUNK))
        a_ref[r1:r1 + FFN_ROWS, :] = (jax.nn.silu(gate) * up).astype(BF)

    def run(project_slot=None, activate_slot=None):
        if project_slot is not None:
            seq_start = (i % tiles_per_seq) == 0
            halo = u_refs[1 - project_slot][tm:]
            u_refs[project_slot][:SUBLANE] = jnp.where(seq_start, 0.0, halo)
        nsplit = 2 * FF_CHUNK // FFN_NSPLIT
        per_dot = FFN_PIECE // FFN_ROWS // nsplit
        for r0 in range(0, tm, FFN_PIECE):
            for n in range(nsplit):
                if project_slot is not None:
                    project(project_slot, r0, n)
                if activate_slot is not None:
                    for k in range(per_dot):
                        activate(activate_slot, r0 + (n * per_dot + k) * FFN_ROWS)

    @pl.when(i == 0)
    def _():
        u0_ref[:SUBLANE] = jnp.zeros((SUBLANE, 2 * FF_CHUNK), F32)
        for r0 in range(0, tm, FFN_PIECE):
            for n in range(2 * FF_CHUNK // FFN_NSPLIT):
                project(0, r0, n)

    for slot in range(2):
        @pl.when((i > 0) & (i < tiles) & (i % 2 == slot))
        def _(slot=slot):
            run(project_slot=slot, activate_slot=1 - slot)

    @pl.when(i == tiles)
    def _():
        run(activate_slot=(tiles - 1) % 2)


def _ffn_down_kernel(a_ref, wd_ref, x_ref, fg_ref, y_ref):
    y_ref[...] = _rms(x_ref[...] + _dot(a_ref[...], wd_ref[...]), fg_ref[...])


def _ffn_up(h2, wu, cw, seq):
    t = h2.shape[0]
    tm = FFN_TM
    tiles = t // tm
    return pl.pallas_call(
        functools.partial(_ffn_up_kernel, tiles=tiles, tiles_per_seq=seq // tm),
        grid=(FF_CHUNKS, tiles + 1),
        in_specs=[
            pl.BlockSpec((tm, D_MODEL), lambda c, i: (jnp.minimum(i, tiles - 1), 0)),
            pl.BlockSpec((D_MODEL, 2 * FF_CHUNK), lambda c, i: (0, c)),
            pl.BlockSpec(((CONV_WIDTH + 1) * SUBLANE, 2 * FF_CHUNK), lambda c, i: (0, c)),
        ],
        out_specs=pl.BlockSpec((tm, FF_CHUNK), lambda c, i: (jnp.maximum(i - 1, 0), c)),
        out_shape=jax.ShapeDtypeStruct((t, D_FF_PAD), BF),
        scratch_shapes=[
            pltpu.VMEM((tm + SUBLANE, 2 * FF_CHUNK), F32),
            pltpu.VMEM((tm + SUBLANE, 2 * FF_CHUNK), F32),
        ],
        compiler_params=_params("arbitrary", "arbitrary"),
        name="ffn_up",
    )(h2, wu, cw)


def _ffn_down(act, wd, x1, fg):
    t = x1.shape[0]
    tm = FFN_DOWN_TM
    return pl.pallas_call(
        _ffn_down_kernel,
        grid=(t // tm,),
        in_specs=[
            pl.BlockSpec((tm, D_FF_PAD), lambda i: (i, 0)),
            pl.BlockSpec((D_FF_PAD, D_MODEL), lambda i: (0, 0), pipeline_mode=pl.Buffered(1)),
            pl.BlockSpec((tm, D_MODEL), lambda i: (i, 0)),
            pl.BlockSpec((1, D_MODEL), lambda i: (0, 0)),
        ],
        out_specs=pl.BlockSpec((tm, D_MODEL), lambda i: (i, 0)),
        out_shape=jax.ShapeDtypeStruct((t, D_MODEL), F32),
        compiler_params=_params("arbitrary"),
        name="ffn_down",
    )(act, wd, x1, fg)


def _rotate_half_cols(w):
    half = w.shape[-1] // 2
    return jnp.concatenate([-w[..., half:], w[..., :half]], axis=-1)


PREP_PIECE = 512
PREP_ROWS = 256


def _relayout_kernel(src_ref, *out_refs, plans):
    ncols = src_ref.shape[1]
    for out_ref, plan in zip(out_refs, plans):
        col = 0
        for start, width in plan:
            for off in range(0, width, PREP_PIECE):
                w = min(PREP_PIECE, width - off)
                dst = slice(col + off, col + off + w)
                if start is None:
                    out_ref[:, dst] = jnp.zeros((out_ref.shape[0], w), out_ref.dtype)
                    continue
                lead = (start + off) % LANE
                lo = start + off - lead
                hi = min(-(-(start + off + w) // LANE) * LANE, ncols)
                out_ref[:, dst] = src_ref[:, lo:hi][:, lead:lead + w].astype(out_ref.dtype)
            col += width


def _relayout(src, plans, dtype, row_tile, name):
    rows, ncols = src.shape
    widths = [sum(w for _, w in plan) for plan in plans]
    assert rows % row_tile == 0 and all(w % LANE == 0 for plan in plans for _, w in plan)
    outs = pl.pallas_call(
        functools.partial(_relayout_kernel, plans=plans),
        grid=(rows // row_tile,),
        in_specs=[pl.BlockSpec((row_tile, ncols), lambda i: (i, 0))],
        out_specs=[pl.BlockSpec((row_tile, w), lambda i: (i, 0)) for w in widths],
        out_shape=[jax.ShapeDtypeStruct((rows, w), dtype) for w in widths],
        compiler_params=_params("arbitrary"),
        name=name,
    )(src)
    return outs


def _transpose_rows_kernel(src_hbm, out_ref, buf_ref, sem, *, row_start, chunks):
    j = pl.program_id(0)

    def fetch(jj, slot):
        rows = pl.ds(pl.multiple_of(row_start(jj), SUBLANE), PREP_PIECE)
        return pltpu.make_async_copy(src_hbm.at[rows, :], buf_ref.at[slot], sem.at[slot])

    @pl.when(j == 0)
    def _():
        fetch(j, 0).start()

    @pl.when(j + 1 < chunks)
    def _():
        fetch(j + 1, (j + 1) % 2).start()

    fetch(j, j % 2).wait()
    out_ref[...] = buf_ref[j % 2].T.astype(out_ref.dtype)


def _transpose_rows(src, row_start, chunks, name):
    ncols = src.shape[1]
    return pl.pallas_call(
        functools.partial(_transpose_rows_kernel, row_start=row_start, chunks=chunks),
        grid=(chunks,),
        in_specs=[pl.BlockSpec(memory_space=pl.ANY)],
        out_specs=pl.BlockSpec((ncols, PREP_PIECE), lambda j: (0, j)),
        out_shape=jax.ShapeDtypeStruct((ncols, chunks * PREP_PIECE), BF),
        scratch_shapes=[pltpu.VMEM((2, PREP_PIECE, ncols), F32), pltpu.SemaphoreType.DMA((2,))],
        compiler_params=_params("arbitrary"),
        name=name,
    )(src)


def _ff_chunk_plan():
    plan = []
    for c in range(FF_CHUNKS):
        lo, hi = c * FF_CHUNK, min((c + 1) * FF_CHUNK, D_FF)
        for half in (0, D_FF):
            plan.append((half + lo, hi - lo))
            if hi - lo < FF_CHUNK:
                plan.append((None, FF_CHUNK - (hi - lo)))
    return plan


def kernel(x, attn_norm_g, w_in, b_gate, q_norm_g, w_uq, kv_norm_g, w_ukv, w_o_mla, w_o_dil,
           w_out, ffn_norm_g, w_up, conv_w, conv_b, w_down, final_norm_g):
    batch, seq, _ = x.shape
    assert w_in.shape[0] == 1, "single-layer block"
    assert seq % TM == 0 and seq % FFN_TM == 0 and seq % TQ == 0 and TQ == TK and seq % (16 * DIL_BLOCK) == 0
    assert FF_CHUNKS >= 3
    t = batch * seq
    x2 = x.reshape(t, D_MODEL)

    dqkv = DIL_HEADS * DIL_HEAD_DIM
    o = np.cumsum((0, Q_LORA_RANK, KV_LORA_RANK, QK_ROPE_DIM, dqkv, dqkv, dqkv, D_MODEL, D_MODEL))
    wt = w_in[0].T
    wt_kpe = wt[o[2]:o[3]]
    wt_rot = jnp.concatenate([-wt_kpe[QK_ROPE_DIM // 2:], wt_kpe[:QK_ROPE_DIM // 2]], axis=0)
    wlat = jnp.concatenate([wt[o[0]:o[3]], wt_rot], axis=0).T.astype(BF)
    gw = DIL_HEADS_PER_GROUP * DIL_HEAD_DIM
    assert gw == PREP_PIECE and D_MODEL % PREP_PIECE == 0
    dil_start = lambda j: int(o[3]) + (j % 3) * (DIL_HEADS * DIL_HEAD_DIM) + (j // 3) * gw
    wdil = _transpose_rows(wt, dil_start, 3 * DIL_GROUPS, "relayout_w_dil")
    wgate = _transpose_rows(wt, lambda j: int(o[6]) + j * PREP_PIECE, 2 * D_MODEL // PREP_PIECE, "relayout_w_gate")
    wq = w_uq[0].astype(BF).reshape(Q_LORA_RANK, MLA_HEADS, QK_NOPE_DIM + QK_ROPE_DIM)
    wq_pe = wq[..., QK_NOPE_DIM:]
    wq = jnp.concatenate([wq[..., :QK_NOPE_DIM], wq_pe, _rotate_half_cols(wq_pe)], axis=-1)
    wq = wq.reshape(Q_LORA_RANK, MLA_HEADS * QK_SLOT)
    wkv = w_ukv[0].astype(BF)
    (wu,) = _relayout(w_up[0], [_ff_chunk_plan()], BF, PREP_ROWS, "relayout_w_up")
    cw = jnp.repeat(jnp.concatenate([conv_w[0], conv_b], axis=0), SUBLANE, axis=0)
    (cw,) = _relayout(cw, [_ff_chunk_plan()], F32, cw.shape[0], "relayout_conv")
    wd =jnp.concatenate([w_down[0].astype(BF), jnp.zeros((D_FF_PAD - D_FF, D_MODEL), BF)], axis=0)

    pos = jnp.arange(seq, dtype=F32)
    inv_freq = ROPE_THETA ** (-jnp.arange(0, QK_ROPE_DIM, 2, dtype=F32) / QK_ROPE_DIM)
    ang = pos[:, None] * inv_freq[None, :]
    cs = jnp.concatenate([jnp.cos(ang), jnp.cos(ang), jnp.sin(ang), jnp.sin(ang)], axis=1)
    slopes = 2.0 ** (-ALIBI_MAX_BIAS * jnp.arange(1, DIL_HEADS + 1, dtype=F32) / DIL_HEADS)

    q, k, v = _mla_prep(x2, attn_norm_g, wlat, q_norm_g, kv_norm_g, wq, wkv, cs, seq)
    o_mla = _mla_flash(q, k, v, batch, seq)
    qkv_dil = _dil_proj(x2, attn_norm_g, wdil, batch, seq)
    o_dil, lse_dil = [], []
    for gi, (_, dil) in enumerate(DIL_PATTERNS):
        o_g, lse_g = _dilated(qkv_dil[gi], slopes, gi, batch, seq)
        nb = seq // dil // DIL_BLOCK
        o_dil.append(o_g)
        lse_g = lse_g.reshape(DIL_HEADS_PER_GROUP, batch, dil, nb, DIL_BLOCK)
        lse_dil.append(lse_g.transpose(1, 3, 4, 2, 0).reshape(t, DIL_HEADS_PER_GROUP))
    lse = jnp.concatenate(lse_dil, axis=1)
    x1, h2 = _merge(x2, attn_norm_g, wgate, b_gate, o_mla, o_dil, lse,
                    w_o_mla[0].astype(BF), w_o_dil[0].astype(BF), w_out[0].astype(BF), ffn_norm_g, seq)

    act = _ffn_up(h2, wu, cw, seq)
    y = _ffn_down(act, wd, x1, final_norm_g.reshape(1, D_MODEL))
    return y.reshape(batch, seq, D_MODEL)
```

```python
import functools
import math

import numpy as np
import jax
import jax.numpy as jnp
from jax import lax
from jax.experimental import pallas as pl
from jax.experimental.pallas import tpu as pltpu

D_MODEL = 2048
MLA_HEADS = 8
QK_NOPE_DIM = 128
QK_ROPE_DIM = 64
V_HEAD_DIM = 128
Q_LORA_RANK = 512
KV_LORA_RANK = 256
ROPE_THETA = 10000.0
DIL_PATTERNS = ((128, 1), (512, 4), (2048, 16))
DIL_GROUPS = 3
DIL_HEADS_PER_GROUP = 4
DIL_HEADS = DIL_GROUPS * DIL_HEADS_PER_GROUP
DIL_HEAD_DIM = 128
DIL_BLOCK = 128
ALIBI_MAX_BIAS = 8.0
D_FF = 5504
CONV_WIDTH = 3
NORM_EPS = 1e-6

LANE = 128
SUBLANE = 8
VMEM_LIMIT = 56 * 1024 * 1024

LAT_DIM = Q_LORA_RANK + KV_LORA_RANK + 2 * QK_ROPE_DIM
QK_SLOT = 2 * LANE
V_SLOT = 2 * LANE
DIL_GROUP_DIM = 3 * DIL_HEADS_PER_GROUP * DIL_HEAD_DIM
FF_CHUNK = 512
FFN_TM = 1024
FFN_DOWN_TM = 256
FFN_PIECE = 256
FFN_NSPLIT = 1024
FFN_ROWS = 32
D_FF_PAD = -(-D_FF // FF_CHUNK) * FF_CHUNK
FF_CHUNKS = D_FF_PAD // FF_CHUNK
NEG = -1e30

TM = 512
TQ = 512
TK = 512
FLASH_HEADS = 4
DIL_BATCH = 8

BF = jnp.bfloat16
F32 = jnp.float32


def _params(*sem):
    return pltpu.CompilerParams(dimension_semantics=sem, vmem_limit_bytes=VMEM_LIMIT)


def _rms(x, g):
    return x * lax.rsqrt(jnp.mean(x * x, axis=-1, keepdims=True) + NORM_EPS) * g


def _dot(a, b):
    return jnp.dot(a, b, preferred_element_type=F32)


def _dot_nt(a, b):
    return lax.dot_general(a, b, (((1,), (1,)), ((), ())), preferred_element_type=F32)


def _rope_pair(t):
    lane = lax.broadcasted_iota(jnp.int32, t.shape, 1)
    return jnp.where(lane < QK_ROPE_DIM, t + pltpu.roll(t, QK_ROPE_DIM, 1), 0.0)


def _mla_prep_kernel(x_ref, g_ref, wlat_ref, qg_ref, kvg_ref, wq_ref, wkv_ref, cs_ref,
                     q_ref, k_ref, v_ref):
    h = _rms(x_ref[...], g_ref[...]).astype(BF)
    lat = _dot(h, wlat_ref[...])
    cq = _rms(lat[:, :Q_LORA_RANK], qg_ref[...]).astype(BF)
    ckv = _rms(lat[:, Q_LORA_RANK:Q_LORA_RANK + KV_LORA_RANK], kvg_ref[...]).astype(BF)
    q = _dot(cq, wq_ref[...])
    kv = _dot(ckv, wkv_ref[...])
    cs = cs_ref[...]
    scale = (QK_NOPE_DIM + QK_ROPE_DIM) ** -0.5 * math.log2(math.e)
    k_rope = _rope_pair(lat[:, Q_LORA_RANK + KV_LORA_RANK:] * cs).astype(BF)
    ones = jnp.ones((x_ref.shape[0], V_SLOT - V_HEAD_DIM), BF)
    for hd in range(MLA_HEADS):
        qs = q[:, hd * QK_SLOT:(hd + 1) * QK_SLOT]
        q_ref[hd, :, :LANE] = (qs[:, :LANE] * scale).astype(BF)
        q_ref[hd, :, LANE:] = _rope_pair(qs[:, LANE:] * (cs * scale)).astype(BF)
        k_ref[hd, :, :LANE] = kv[:, hd * 256:hd * 256 + LANE].astype(BF)
        k_ref[hd, :, LANE:] = k_rope
        v_ref[hd, :, :V_HEAD_DIM] = kv[:, hd * 256 + LANE:(hd + 1) * 256].astype(BF)
        v_ref[hd, :, V_HEAD_DIM:] = ones


def _mla_prep(x2, g, wlat, qg, kvg, wq, wkv, cs, seq):
    t = x2.shape[0]
    nseq = seq // TM
    const = lambda i: (0, 0)
    return pl.pallas_call(
        _mla_prep_kernel,
        grid=(t // TM,),
        in_specs=[
            pl.BlockSpec((TM, D_MODEL), lambda i: (i, 0)),
            pl.BlockSpec((1, D_MODEL), const),
            pl.BlockSpec((D_MODEL, LAT_DIM), const),
            pl.BlockSpec((1, Q_LORA_RANK), const),
            pl.BlockSpec((1, KV_LORA_RANK), const),
            pl.BlockSpec((Q_LORA_RANK, MLA_HEADS * QK_SLOT), const),
            pl.BlockSpec((KV_LORA_RANK, MLA_HEADS * 256), const),
            pl.BlockSpec((TM, LANE), lambda i: (i % nseq, 0)),
        ],
        out_specs=[
            pl.BlockSpec((MLA_HEADS, TM, QK_SLOT), lambda i: (0, i, 0)),
            pl.BlockSpec((MLA_HEADS, TM, QK_SLOT), lambda i: (0, i, 0)),
            pl.BlockSpec((MLA_HEADS, TM, V_SLOT), lambda i: (0, i, 0)),
        ],
        out_shape=[
            jax.ShapeDtypeStruct((MLA_HEADS, t, QK_SLOT), BF),
            jax.ShapeDtypeStruct((MLA_HEADS, t, QK_SLOT), BF),
            jax.ShapeDtypeStruct((MLA_HEADS, t, V_SLOT), BF),
        ],
        compiler_params=_params("arbitrary"),
        name="mla_prep",
    )(x2, g, wlat, qg, kvg, wq, wkv, cs)


def _dil_proj_kernel(x_ref, g_ref, w_ref, *rest):
    o_refs, h_ref, res_ref = rest[:DIL_GROUPS], rest[DIL_GROUPS], rest[DIL_GROUPS + 1]
    nqkv = 3 * DIL_HEADS_PER_GROUP
    h_ref[...] = _rms(x_ref[...], g_ref[...]).astype(BF)
    for gi, (_, dil) in enumerate(DIL_PATTERNS):
        sub = TM // dil
        res = _dot(h_ref[...], w_ref[:, gi * DIL_GROUP_DIM:(gi + 1) * DIL_GROUP_DIM])
        for hh in range(nqkv):
            blk = res[:, hh * DIL_HEAD_DIM:(hh + 1) * DIL_HEAD_DIM]
            if hh < DIL_HEADS_PER_GROUP:
                blk = blk * DIL_HEAD_DIM ** -0.5
            if dil == 1:
                o_refs[gi][hh, 0, 0] = blk.astype(BF)
            else:
                res_ref[hh] = blk
        if dil > 1:
            for hh in range(nqkv):
                for r in range(dil):
                    o_refs[gi][hh, 0, r] = res_ref[hh, pl.ds(r, sub, stride=dil), :].astype(BF)


def _dil_proj(x2, g, w, batch, seq):
    nseq = seq // TM
    nqkv = 3 * DIL_HEADS_PER_GROUP
    const = lambda i: (0, 0)
    return pl.pallas_call(
        _dil_proj_kernel,
        grid=(batch * nseq,),
        in_specs=[
            pl.BlockSpec((TM, D_MODEL), lambda i: (i, 0)),
            pl.BlockSpec((1, D_MODEL), const),
            pl.BlockSpec((D_MODEL, DIL_GROUPS * DIL_GROUP_DIM), const, pipeline_mode=pl.Buffered(1)),
        ],
        out_specs=[
            pl.BlockSpec((nqkv, 1, dil, TM // dil, DIL_HEAD_DIM), lambda i: (0, i // nseq, 0, i % nseq, 0))
            for _, dil in DIL_PATTERNS
        ],
        out_shape=[
            jax.ShapeDtypeStruct((nqkv, batch, dil, seq // dil, DIL_HEAD_DIM), BF)
            for _, dil in DIL_PATTERNS
        ],
        scratch_shapes=[pltpu.VMEM((TM, D_MODEL), BF), pltpu.VMEM((nqkv, TM, DIL_HEAD_DIM), F32)],
        compiler_params=_params("arbitrary"),
        name="dil_proj",
    )(x2, g, w)


def _flash_kernel(q_ref, k_ref, v_ref, o_ref, *scratch):
    nh = FLASH_HEADS
    m_refs, acc_refs = scratch[:nh], scratch[nh:2 * nh]
    s_refs = (scratch[2 * nh:3 * nh], scratch[3 * nh:4 * nh])
    qi = pl.program_id(2)
    for hh in range(nh):
        m_refs[hh][...] = jnp.full(m_refs[hh].shape, NEG, F32)
        acc_refs[hh][...] = jnp.zeros(acc_refs[hh].shape, F32)

    def key_rows(j):
        return pl.ds(pl.multiple_of(j * TK, TK), TK)

    def scores(j, slot):
        for hh in range(nh):
            s_refs[slot][hh][...] = _dot_nt(q_ref[hh], k_ref[hh, key_rows(j), :])

    def mask_diagonal(slot):
        keep = lax.broadcasted_iota(jnp.int32, (TQ, TK), 1) <= lax.broadcasted_iota(jnp.int32, (TQ, TK), 0)
        for hh in range(nh):
            s_refs[slot][hh][...] = jnp.where(keep, s_refs[slot][hh][...], NEG)

    def update(j, slot):
        for hh in range(nh):
            s = s_refs[slot][hh][...]
            m_old = m_refs[hh][...]
            m_new = jnp.maximum(m_old, jnp.max(s, axis=-1, keepdims=True))
            p = jnp.exp2(s - m_new)
            pv = _dot(p.astype(BF), v_ref[hh, key_rows(j), :])
            acc_refs[hh][...] = jnp.exp2(m_old - m_new) * acc_refs[hh][...] + pv
            m_refs[hh][...] = m_new

    scores(0, 0)

    @pl.when(qi == 0)
    def _():
        mask_diagonal(0)

    def body(j, carry):
        for slot in range(2):
            @pl.when(j % 2 == slot)
            def _(slot=slot):
                scores(j, slot)
                update(j - 1, 1 - slot)

            @pl.when((j % 2 == slot) & (j == qi))
            def _(slot=slot):
                mask_diagonal(slot)
        return carry

    lax.fori_loop(1, qi + 1, body, 0)

    for slot in range(2):
        @pl.when(qi % 2 == slot)
        def _(slot=slot):
            update(qi, slot)

    for hh in range(FLASH_HEADS):
        acc = acc_refs[hh][...]
        o_ref[hh] = (acc[:, :V_HEAD_DIM] * (1.0 / acc[:, V_HEAD_DIM:])).astype(BF)


def _mla_flash(q, k, v, batch, seq):
    nq = seq // TQ
    t = batch * seq
    hb = FLASH_HEADS
    return pl.pallas_call(
        _flash_kernel,
        grid=(batch, MLA_HEADS // hb, nq),
        in_specs=[
            pl.BlockSpec((hb, TQ, QK_SLOT), lambda b, h, i: (h, b * nq + i, 0)),
            pl.BlockSpec((hb, seq, QK_SLOT), lambda b, h, i: (h, b, 0)),
            pl.BlockSpec((hb, seq, V_SLOT), lambda b, h, i: (h, b, 0)),
        ],
        out_specs=pl.BlockSpec((hb, TQ, V_HEAD_DIM), lambda b, h, i: (h, b * nq + i, 0)),
        out_shape=jax.ShapeDtypeStruct((MLA_HEADS, t, V_HEAD_DIM), BF),
        scratch_shapes=([pltpu.VMEM((TQ, 1), F32)] * hb + [pltpu.VMEM((TQ, V_SLOT), F32)] * hb
                        + [pltpu.VMEM((TQ, TK), F32)] * (2 * hb)),
        compiler_params=_params("arbitrary", "arbitrary", "arbitrary"),
        name="mla_flash",
    )(q, k, v)


def _dilated_kernel(slope_ref, q_ref, k_ref, v_ref, o_ref, lse_ref, *, group, dil, nb):
    hs = pl.program_id(1)
    slope = slope_ref[group * DIL_HEADS_PER_GROUP + hs] * float(dil)
    qi = lax.broadcasted_iota(jnp.int32, (DIL_BLOCK, DIL_BLOCK), 0)
    ki = lax.broadcasted_iota(jnp.int32, (DIL_BLOCK, DIL_BLOCK), 1)
    dist_cur = (qi - ki).astype(F32)
    dist_prev = dist_cur + float(DIL_BLOCK)
    w_sub = float(DIL_PATTERNS[group][0] // dil)
    bias_cur = jnp.where((dist_cur >= 0.0) & (dist_cur <= w_sub), -slope * dist_cur, NEG)
    bias_prev = jnp.where(dist_prev <= w_sub, -slope * dist_prev, NEG)
    neg = jnp.full((DIL_BLOCK, DIL_BLOCK), NEG, F32)
    bias_inner = jnp.concatenate([bias_prev, bias_cur], axis=1)
    bias_first = jnp.concatenate([bias_cur, neg], axis=1)
    eye = qi == ki

    def window(n):
        start = max(n - 1, 0) * DIL_BLOCK
        return slice(start, start + 2 * DIL_BLOCK)

    blocks = [(r, n) for r in range(dil) for n in range(nb)]
    for g0 in range(0, len(blocks), DIL_BATCH):
        batch = blocks[g0:g0 + DIL_BATCH]
        scores = [_dot_nt(q_ref[0, 0, r, n * DIL_BLOCK:(n + 1) * DIL_BLOCK, :], k_ref[0, 0, r, window(n), :])
                  + (bias_first if n == 0 else bias_inner) for r, n in batch]
        maxes = [jnp.max(s, axis=-1, keepdims=True) for s in scores]
        probs = [jnp.exp(s - m) for s, m in zip(scores, maxes)]
        sums = [jnp.sum(p, axis=-1, keepdims=True) for p in probs]
        outs = [_dot(p.astype(BF), v_ref[0, 0, r, window(n), :]) for p, (r, n) in zip(probs, batch)]
        for (r, n), o, l in zip(batch, outs, sums):
            o_ref[0, 0, r, n * DIL_BLOCK:(n + 1) * DIL_BLOCK, :] = (o * (1.0 / l)).astype(BF)
        rows = [jnp.sum(jnp.where(eye, m + jnp.log(l), 0.0), axis=0, keepdims=True) for m, l in zip(maxes, sums)]
        lse_ref[0, 0, g0:g0 + DIL_BATCH, :] = jnp.concatenate(rows, axis=0)


def _dilated(qkv, slopes, group, batch, seq):
    _, dil = DIL_PATTERNS[group]
    sub = seq // dil
    nb = sub // DIL_BLOCK
    hg = DIL_HEADS_PER_GROUP
    blk = (1, 1, dil, sub, DIL_HEAD_DIM)

    def head(which):
        return lambda b, h: (which * hg + h, b, 0, 0, 0)

    return pl.pallas_call(
        functools.partial(_dilated_kernel, group=group, dil=dil, nb=nb),
        grid=(batch, hg),
        in_specs=[
            pl.BlockSpec(memory_space=pltpu.SMEM),
            pl.BlockSpec(blk, head(0)),
            pl.BlockSpec(blk, head(1)),
            pl.BlockSpec(blk, head(2)),
        ],
        out_specs=[
            pl.BlockSpec(blk, head(0)),
            pl.BlockSpec((1, 1, dil * nb, DIL_BLOCK), lambda b, h: (h, b, 0, 0)),
        ],
        out_shape=[
            jax.ShapeDtypeStruct((hg, batch, dil, sub, DIL_HEAD_DIM), BF),
            jax.ShapeDtypeStruct((hg, batch, dil * nb, DIL_BLOCK), F32),
        ],
        compiler_params=_params("arbitrary", "arbitrary"),
        name=f"dilated_d{dil}",
    )(slopes, qkv, qkv, qkv)


MERGE_TN = 512
MERGE_TM = 256


def _merge_kernel(x_ref, g_ref, wg_ref, bg_ref, oa_ref, o1_ref, o4_ref, o16_ref, lse_ref,
                  woa_ref, wob_ref, wout_ref, fg_ref, y_ref, h2_ref, h_ref, a_ref, b_ref, mg_ref, nat_ref):
    x = x_ref[...]
    tm = x.shape[0]
    h_ref[...] = _rms(x, g_ref[...]).astype(BF)
    for hd in range(MLA_HEADS):
        a_ref[:, hd * V_HEAD_DIM:(hd + 1) * V_HEAD_DIM] = oa_ref[hd]
    lse = lse_ref[...]
    o_refs = (o1_ref, o4_ref, o16_ref)
    for hs in range(DIL_HEADS_PER_GROUP):
        ls = [lse[:, gi * DIL_HEADS_PER_GROUP + hs:gi * DIL_HEADS_PER_GROUP + hs + 1] for gi in range(DIL_GROUPS)]
        mx = jnp.maximum(jnp.maximum(ls[0], ls[1]), ls[2])
        es = [jnp.exp(v - mx) for v in ls]
        inv = 1.0 / (es[0] + es[1] + es[2])
        comb = (es[0] * inv) * o_refs[0][hs, 0, 0].astype(F32)
        for gi in range(1, DIL_GROUPS):
            dil = DIL_PATTERNS[gi][1]
            for r in range(dil):
                nat_ref[pl.ds(r, tm // dil, stride=dil), :] = o_refs[gi][hs, 0, r].astype(F32)
            comb = comb + (es[gi] * inv) * nat_ref[...]
        b_ref[:, hs * DIL_HEAD_DIM:(hs + 1) * DIL_HEAD_DIM] = comb.astype(BF)
    for c in range(D_MODEL // MERGE_TN):
        cols = slice(c * MERGE_TN, (c + 1) * MERGE_TN)
        cols_b = slice(D_MODEL + c * MERGE_TN, D_MODEL + (c + 1) * MERGE_TN)
        gate_a = jax.nn.sigmoid(_dot(h_ref[...], wg_ref[:, cols]) + bg_ref[:, cols])
        gate_b = jax.nn.sigmoid(_dot(h_ref[...], wg_ref[:, cols_b]) + bg_ref[:, cols_b])
        o_a = _dot(a_ref[...], woa_ref[:, cols])
        o_b = _dot(b_ref[...], wob_ref[:, cols])
        mg_ref[:, cols] = (gate_a * o_a + gate_b * o_b).astype(BF)
    y = x + _dot(mg_ref[...], wout_ref[...])
    y_ref[...] = y
    h2_ref[...] = _rms(y, fg_ref[...]).astype(BF)


def _merge(x2, g, wg, bg, oa, o_dil, lse, woa, wob, wout, fg, seq):
    t = x2.shape[0]
    tm = MERGE_TM
    nseq = seq // tm
    const = lambda i: (0, 0)
    one = pl.Buffered(1)
    dil_specs = [
        pl.BlockSpec((DIL_HEADS_PER_GROUP, 1, dil, tm // dil, DIL_HEAD_DIM),
                     lambda i: (0, i // nseq, 0, i % nseq, 0))
        for _, dil in DIL_PATTERNS
    ]
    return pl.pallas_call(
        _merge_kernel,
        grid=(t // tm,),
        in_specs=[
            pl.BlockSpec((tm, D_MODEL), lambda i: (i, 0)),
            pl.BlockSpec((1, D_MODEL), const),
            pl.BlockSpec((D_MODEL, 2 * D_MODEL), const, pipeline_mode=one),
            pl.BlockSpec((1, 2 * D_MODEL), const),
            pl.BlockSpec((MLA_HEADS, tm, V_HEAD_DIM), lambda i: (0, i, 0)),
            *dil_specs,
            pl.BlockSpec((tm, DIL_HEADS), lambda i: (i, 0)),
            pl.BlockSpec((MLA_HEADS * V_HEAD_DIM, D_MODEL), const, pipeline_mode=one),
            pl.BlockSpec((DIL_HEADS_PER_GROUP * DIL_HEAD_DIM, D_MODEL), const, pipeline_mode=one),
            pl.BlockSpec((D_MODEL, D_MODEL), const, pipeline_mode=one),
            pl.BlockSpec((1, D_MODEL), const),
        ],
        out_specs=[pl.BlockSpec((tm, D_MODEL), lambda i: (i, 0)), pl.BlockSpec((tm, D_MODEL), lambda i: (i, 0))],
        out_shape=[jax.ShapeDtypeStruct((t, D_MODEL), F32), jax.ShapeDtypeStruct((t, D_MODEL), BF)],
        scratch_shapes=[
            pltpu.VMEM((tm, D_MODEL), BF),
            pltpu.VMEM((tm, MLA_HEADS * V_HEAD_DIM), BF),
            pltpu.VMEM((tm, DIL_HEADS_PER_GROUP * DIL_HEAD_DIM), BF),
            pltpu.VMEM((tm, D_MODEL), BF),
            pltpu.VMEM((tm, DIL_HEAD_DIM), F32),
        ],
        compiler_params=_params("arbitrary"),
        name="merge",
    )(x2, g, wg, bg, oa, *o_dil, lse, woa, wob, wout, fg)


def _ffn_up_kernel(h_ref, w_ref, cw_ref, a_ref, u0_ref, u1_ref, *, tiles, tiles_per_seq):
    i = pl.program_id(1)
    tm = h_ref.shape[0]
    u_refs = (u0_ref, u1_ref)

    def project(slot, r0, n):
        rows = slice(SUBLANE + r0, SUBLANE + r0 + FFN_PIECE)
        cols = slice(n * FFN_NSPLIT, (n + 1) * FFN_NSPLIT)
        u_refs[slot][rows, cols] = _dot(h_ref[r0:r0 + FFN_PIECE, :], w_ref[:, cols])

    def activate(slot, r1):
        u_ref = u_refs[slot]
        groups = FFN_ROWS // SUBLANE

        def conv(cols):
            window = u_ref[r1:r1 + FFN_ROWS + SUBLANE, cols]
            out = None
            for tap in range(CONV_WIDTH):
                back = CONV_WIDTH - 1 - tap
                shifted = window if back == 0 else pltpu.roll(window, back, 0)
                x = shifted[SUBLANE:].reshape(groups, SUBLANE, FF_CHUNK)
                term = cw_ref[tap * SUBLANE:(tap + 1) * SUBLANE, cols][None] * x
                out = term if out is None else out + term
            out = out + cw_ref[CONV_WIDTH * SUBLANE:(CONV_WIDTH + 1) * SUBLANE, cols][None]
            return out.reshape(FFN_ROWS, FF_CHUNK)

        up = conv(slice(0, FF_CHUNK))
        gate = conv(slice(FF_CHUNK, 2 * FF_CHUNK))
        a_ref[r1:r1 + FFN_ROWS, :] = (jax.nn.silu(gate) * up).astype(BF)

    def run(project_slot=None, activate_slot=None):
        if project_slot is not None:
            seq_start = (i % tiles_per_seq) == 0
            halo = u_refs[1 - project_slot][tm:]
            u_refs[project_slot][:SUBLANE] = jnp.where(seq_start, 0.0, halo)
        nsplit = 2 * FF_CHUNK // FFN_NSPLIT
        per_dot = FFN_PIECE // FFN_ROWS // nsplit
        for r0 in range(0, tm, FFN_PIECE):
            for n in range(nsplit):
                if project_slot is not None:
                    project(project_slot, r0, n)
                if activate_slot is not None:
                    for k in range(per_dot):
                        activate(activate_slot, r0 + (n * per_dot + k) * FFN_ROWS)

    @pl.when(i == 0)
    def _():
        u0_ref[:SUBLANE] = jnp.zeros((SUBLANE, 2 * FF_CHUNK), F32)
        for r0 in range(0, tm, FFN_PIECE):
            for n in range(2 * FF_CHUNK // FFN_NSPLIT):
                project(0, r0, n)

    for slot in range(2):
        @pl.when((i > 0) & (i < tiles) & (i % 2 == slot))
        def _(slot=slot):
            run(project_slot=slot, activate_slot=1 - slot)

    @pl.when(i == tiles)
    def _():
        run(activate_slot=(tiles - 1) % 2)


def _ffn_down_kernel(a_ref, wd_ref, x_ref, fg_ref, y_ref):
    y_ref[...] = _rms(x_ref[...] + _dot(a_ref[...], wd_ref[...]), fg_ref[...])


def _ffn_up(h2, wu, cw, seq):
    t = h2.shape[0]
    tm = FFN_TM
    tiles = t // tm
    return pl.pallas_call(
        functools.partial(_ffn_up_kernel, tiles=tiles, tiles_per_seq=seq // tm),
        grid=(FF_CHUNKS, tiles + 1),
        in_specs=[
            pl.BlockSpec((tm, D_MODEL), lambda c, i: (jnp.minimum(i, tiles - 1), 0)),
            pl.BlockSpec((D_MODEL, 2 * FF_CHUNK), lambda c, i: (0, c)),
            pl.BlockSpec(((CONV_WIDTH + 1) * SUBLANE, 2 * FF_CHUNK), lambda c, i: (0, c)),
        ],
        out_specs=pl.BlockSpec((tm, FF_CHUNK), lambda c, i: (jnp.maximum(i - 1, 0), c)),
        out_shape=jax.ShapeDtypeStruct((t, D_FF_PAD), BF),
        scratch_shapes=[
            pltpu.VMEM((tm + SUBLANE, 2 * FF_CHUNK), F32),
            pltpu.VMEM((tm + SUBLANE, 2 * FF_CHUNK), F32),
        ],
        compiler_params=_params("arbitrary", "arbitrary"),
        name="ffn_up",
    )(h2, wu, cw)


def _ffn_down(act, wd, x1, fg):
    t = x1.shape[0]
    tm = FFN_DOWN_TM
    return pl.pallas_call(
        _ffn_down_kernel,
        grid=(t // tm,),
        in_specs=[
            pl.BlockSpec((tm, D_FF_PAD), lambda i: (i, 0)),
            pl.BlockSpec((D_FF_PAD, D_MODEL), lambda i: (0, 0), pipeline_mode=pl.Buffered(1)),
            pl.BlockSpec((tm, D_MODEL), lambda i: (i, 0)),
            pl.BlockSpec((1, D_MODEL), lambda i: (0, 0)),
        ],
        out_specs=pl.BlockSpec((tm, D_MODEL), lambda i: (i, 0)),
        out_shape=jax.ShapeDtypeStruct((t, D_MODEL), F32),
        compiler_params=_params("arbitrary"),
        name="ffn_down",
    )(act, wd, x1, fg)


def _rotate_half_cols(w):
    half = w.shape[-1] // 2
    return jnp.concatenate([-w[..., half:], w[..., :half]], axis=-1)


PREP_PIECE = 512
PREP_ROWS = 256


def _relayout_kernel(src_ref, *out_refs, plans):
    ncols = src_ref.shape[1]
    for out_ref, plan in zip(out_refs, plans):
        col = 0
        for start, width in plan:
            for off in range(0, width, PREP_PIECE):
                w = min(PREP_PIECE, width - off)
                dst = slice(col + off, col + off + w)
                if start is None:
                    out_ref[:, dst] = jnp.zeros((out_ref.shape[0], w), out_ref.dtype)
                    continue
                lead = (start + off) % LANE
                lo = start + off - lead
                hi = min(-(-(start + off + w) // LANE) * LANE, ncols)
                out_ref[:, dst] = src_ref[:, lo:hi][:, lead:lead + w].astype(out_ref.dtype)
            col += width


def _relayout(src, plans, dtype, row_tile, name):
    rows, ncols = src.shape
    widths = [sum(w for _, w in plan) for plan in plans]
    assert rows % row_tile == 0 and all(w % LANE == 0 for plan in plans for _, w in plan)
    outs = pl.pallas_call(
        functools.partial(_relayout_kernel, plans=plans),
        grid=(rows // row_tile,),
        in_specs=[pl.BlockSpec((row_tile, ncols), lambda i: (i, 0))],
        out_specs=[pl.BlockSpec((row_tile, w), lambda i: (i, 0)) for w in widths],
        out_shape=[jax.ShapeDtypeStruct((rows, w), dtype) for w in widths],
        compiler_params=_params("arbitrary"),
        name=name,
    )(src)
    return outs


def _transpose_rows_kernel(src_hbm, out_ref, buf_ref, sem, *, row_start, chunks):
    j = pl.program_id(0)

    def fetch(jj, slot):
        rows = pl.ds(pl.multiple_of(row_start(jj), SUBLANE), PREP_PIECE)
        return pltpu.make_async_copy(src_hbm.at[rows, :], buf_ref.at[slot], sem.at[slot])

    @pl.when(j == 0)
    def _():
        fetch(j, 0).start()

    @pl.when(j + 1 < chunks)
    def _():
        fetch(j + 1, (j + 1) % 2).start()

    fetch(j, j % 2).wait()
    out_ref[...] = buf_ref[j % 2].T.astype(out_ref.dtype)


def _transpose_rows(src, row_start, chunks, name):
    ncols = src.shape[1]
    return pl.pallas_call(
        functools.partial(_transpose_rows_kernel, row_start=row_start, chunks=chunks),
        grid=(chunks,),
        in_specs=[pl.BlockSpec(memory_space=pl.ANY)],
        out_specs=pl.BlockSpec((ncols, PREP_PIECE), lambda j: (0, j)),
        out_shape=jax.ShapeDtypeStruct((ncols, chunks * PREP_PIECE), BF),
        scratch_shapes=[pltpu.VMEM((2, PREP_PIECE, ncols), F32), pltpu.SemaphoreType.DMA((2,))],
        compiler_params=_params("arbitrary"),
        name=name,
    )(src)


def _ff_chunk_plan():
    plan = []
    for c in range(FF_CHUNKS):
        lo, hi = c * FF_CHUNK, min((c + 1) * FF_CHUNK, D_FF)
        for half in (0, D_FF):
            plan.append((half + lo, hi - lo))
            if hi - lo < FF_CHUNK:
                plan.append((None, FF_CHUNK - (hi - lo)))
    return plan


def kernel(x, attn_norm_g, w_in, b_gate, q_norm_g, w_uq, kv_norm_g, w_ukv, w_o_mla, w_o_dil,
           w_out, ffn_norm_g, w_up, conv_w, conv_b, w_down, final_norm_g):
    batch, seq, _ = x.shape
    assert w_in.shape[0] == 1, "single-layer block"
    assert seq % TM == 0 and seq % FFN_TM == 0 and seq % TQ == 0 and TQ == TK and seq % (16 * DIL_BLOCK) == 0
    assert FF_CHUNKS >= 3
    t = batch * seq
    x2 = x.reshape(t, D_MODEL)

    dqkv = DIL_HEADS * DIL_HEAD_DIM
    o = np.cumsum((0, Q_LORA_RANK, KV_LORA_RANK, QK_ROPE_DIM, dqkv, dqkv, dqkv, D_MODEL, D_MODEL))
    wt = w_in[0].T
    w_head = _transpose_rows(wt, lambda j: j * PREP_PIECE, -(-int(o[3]) // PREP_PIECE), "relayout_w_lat")
    wlat = jnp.concatenate([w_head[:, o[0]:o[3]], _rotate_half_cols(w_head[:, o[2]:o[3]])], axis=1)
    gw = DIL_HEADS_PER_GROUP * DIL_HEAD_DIM
    assert gw == PREP_PIECE and D_MODEL % PREP_PIECE == 0
    dil_start = lambda j: int(o[3]) + (j % 3) * (DIL_HEADS * DIL_HEAD_DIM) + (j // 3) * gw
    wdil = _transpose_rows(wt, dil_start, 3 * DIL_GROUPS, "relayout_w_dil")
    wgate = _transpose_rows(wt, lambda j: int(o[6]) + j * PREP_PIECE, 2 * D_MODEL // PREP_PIECE, "relayout_w_gate")
    wq = w_uq[0].astype(BF).reshape(Q_LORA_RANK, MLA_HEADS, QK_NOPE_DIM + QK_ROPE_DIM)
    wq_pe = wq[..., QK_NOPE_DIM:]
    wq = jnp.concatenate([wq[..., :QK_NOPE_DIM], wq_pe, _rotate_half_cols(wq_pe)], axis=-1)
    wq = wq.reshape(Q_LORA_RANK, MLA_HEADS * QK_SLOT)
    wkv = w_ukv[0].astype(BF)
    (wu,) = _relayout(w_up[0], [_ff_chunk_plan()], BF, PREP_ROWS, "relayout_w_up")
    cw = jnp.repeat(jnp.concatenate([conv_w[0], conv_b], axis=0), SUBLANE, axis=0)
    (cw,) = _relayout(cw, [_ff_chunk_plan()], F32, cw.shape[0], "relayout_conv")
    wd =jnp.concatenate([w_down[0].astype(BF), jnp.zeros((D_FF_PAD - D_FF, D_MODEL), BF)], axis=0)

    pos = jnp.arange(seq, dtype=F32)
    inv_freq = ROPE_THETA ** (-jnp.arange(0, QK_ROPE_DIM, 2, dtype=F32) / QK_ROPE_DIM)
    ang = pos[:, None] * inv_freq[None, :]
    cs = jnp.concatenate([jnp.cos(ang), jnp.cos(ang), jnp.sin(ang), jnp.sin(ang)], axis=1)
    slopes = 2.0 ** (-ALIBI_MAX_BIAS * jnp.arange(1, DIL_HEADS + 1, dtype=F32) / DIL_HEADS)

    q, k, v = _mla_prep(x2, attn_norm_g, wlat, q_norm_g, kv_norm_g, wq, wkv, cs, seq)
    o_mla = _mla_flash(q, k, v, batch, seq)
    qkv_dil = _dil_proj(x2, attn_norm_g, wdil, batch, seq)
    o_dil, lse_dil = [], []
    for gi, (_, dil) in enumerate(DIL_PATTERNS):
        o_g, lse_g = _dilated(qkv_dil[gi], slopes, gi, batch, seq)
        nb = seq // dil // DIL_BLOCK
        o_dil.append(o_g)
        lse_g = lse_g.reshape(DIL_HEADS_PER_GROUP, batch, dil, nb, DIL_BLOCK)
        lse_dil.append(lse_g.transpose(1, 3, 4, 2, 0).reshape(t, DIL_HEADS_PER_GROUP))
    lse = jnp.concatenate(lse_dil, axis=1)
    x1, h2 = _merge(x2, attn_norm_g, wgate, b_gate, o_mla, o_dil, lse,
                    w_o_mla[0].astype(BF), w_o_dil[0].astype(BF), w_out[0].astype(BF), ffn_norm_g, seq)

    act = _ffn_up(h2, wu, cw, seq)
    y = _ffn_down(act, wd, x1, final_norm_g.reshape(1, D_MODEL))
    return y.reshape(batch, seq, D_MODEL)
```

```python
import functools
import math

import numpy as np
import jax
import jax.numpy as jnp
from jax import lax
from jax.experimental import pallas as pl
from jax.experimental.pallas import tpu as pltpu

D_MODEL = 2048
MLA_HEADS = 8
QK_NOPE_DIM = 128
QK_ROPE_DIM = 64
V_HEAD_DIM = 128
Q_LORA_RANK = 512
KV_LORA_RANK = 256
ROPE_THETA = 10000.0
DIL_PATTERNS = ((128, 1), (512, 4), (2048, 16))
DIL_GROUPS = 3
DIL_HEADS_PER_GROUP = 4
DIL_HEADS = DIL_GROUPS * DIL_HEADS_PER_GROUP
DIL_HEAD_DIM = 128
DIL_BLOCK = 128
ALIBI_MAX_BIAS = 8.0
D_FF = 5504
CONV_WIDTH = 3
NORM_EPS = 1e-6

LANE = 128
SUBLANE = 8
VMEM_LIMIT = 56 * 1024 * 1024

LAT_DIM = Q_LORA_RANK + KV_LORA_RANK + 2 * QK_ROPE_DIM
QK_SLOT = 2 * LANE
V_SLOT = 2 * LANE
DIL_GROUP_DIM = 3 * DIL_HEADS_PER_GROUP * DIL_HEAD_DIM
FF_CHUNK = 512
FFN_TM = 1024
FFN_DOWN_TM = 256
FFN_PIECE = 256
FFN_NSPLIT = 1024
FFN_KSLICE = 256
FFN_ROWS = 32
D_FF_PAD = -(-D_FF // FF_CHUNK) * FF_CHUNK
FF_CHUNKS = D_FF_PAD // FF_CHUNK
NEG = -1e30

TM = 512
TQ = 512
TK = 512
FLASH_HEADS = 4
DIL_BATCH = 8

BF = jnp.bfloat16
F32 = jnp.float32


def _params(*sem):
    return pltpu.CompilerParams(dimension_semantics=sem, vmem_limit_bytes=VMEM_LIMIT)


def _rms(x, g):
    return x * lax.rsqrt(jnp.mean(x * x, axis=-1, keepdims=True) + NORM_EPS) * g


def _dot(a, b):
    return jnp.dot(a, b, preferred_element_type=F32)


def _dot_nt(a, b):
    return lax.dot_general(a, b, (((1,), (1,)), ((), ())), preferred_element_type=F32)


def _rope_pair(t):
    lane = lax.broadcasted_iota(jnp.int32, t.shape, 1)
    return jnp.where(lane < QK_ROPE_DIM, t + pltpu.roll(t, QK_ROPE_DIM, 1), 0.0)


def _mla_prep_kernel(x_ref, g_ref, wlat_ref, qg_ref, kvg_ref, wq_ref, wkv_ref, cs_ref,
                     q_ref, k_ref, v_ref):
    h = _rms(x_ref[...], g_ref[...]).astype(BF)
    lat = _dot(h, wlat_ref[...])
    cq = _rms(lat[:, :Q_LORA_RANK], qg_ref[...]).astype(BF)
    ckv = _rms(lat[:, Q_LORA_RANK:Q_LORA_RANK + KV_LORA_RANK], kvg_ref[...]).astype(BF)
    q = _dot(cq, wq_ref[...])
    kv = _dot(ckv, wkv_ref[...])
    cs = cs_ref[...]
    scale = (QK_NOPE_DIM + QK_ROPE_DIM) ** -0.5 * math.log2(math.e)
    k_rope = _rope_pair(lat[:, Q_LORA_RANK + KV_LORA_RANK:] * cs).astype(BF)
    ones = jnp.ones((x_ref.shape[0], V_SLOT - V_HEAD_DIM), BF)
    for hd in range(MLA_HEADS):
        qs = q[:, hd * QK_SLOT:(hd + 1) * QK_SLOT]
        q_ref[hd, :, :LANE] = (qs[:, :LANE] * scale).astype(BF)
        q_ref[hd, :, LANE:] = _rope_pair(qs[:, LANE:] * (cs * scale)).astype(BF)
        k_ref[hd, :, :LANE] = kv[:, hd * 256:hd * 256 + LANE].astype(BF)
        k_ref[hd, :, LANE:] = k_rope
        v_ref[hd, :, :V_HEAD_DIM] = kv[:, hd * 256 + LANE:(hd + 1) * 256].astype(BF)
        v_ref[hd, :, V_HEAD_DIM:] = ones


def _mla_prep(x2, g, wlat, qg, kvg, wq, wkv, cs, seq):
    t = x2.shape[0]
    nseq = seq // TM
    const = lambda i: (0, 0)
    return pl.pallas_call(
        _mla_prep_kernel,
        grid=(t // TM,),
        in_specs=[
            pl.BlockSpec((TM, D_MODEL), lambda i: (i, 0)),
            pl.BlockSpec((1, D_MODEL), const),
            pl.BlockSpec((D_MODEL, LAT_DIM), const),
            pl.BlockSpec((1, Q_LORA_RANK), const),
            pl.BlockSpec((1, KV_LORA_RANK), const),
            pl.BlockSpec((Q_LORA_RANK, MLA_HEADS * QK_SLOT), const),
            pl.BlockSpec((KV_LORA_RANK, MLA_HEADS * 256), const),
            pl.BlockSpec((TM, LANE), lambda i: (i % nseq, 0)),
        ],
        out_specs=[
            pl.BlockSpec((MLA_HEADS, TM, QK_SLOT), lambda i: (0, i, 0)),
            pl.BlockSpec((MLA_HEADS, TM, QK_SLOT), lambda i: (0, i, 0)),
            pl.BlockSpec((MLA_HEADS, TM, V_SLOT), lambda i: (0, i, 0)),
        ],
        out_shape=[
            jax.ShapeDtypeStruct((MLA_HEADS, t, QK_SLOT), BF),
            jax.ShapeDtypeStruct((MLA_HEADS, t, QK_SLOT), BF),
            jax.ShapeDtypeStruct((MLA_HEADS, t, V_SLOT), BF),
        ],
        compiler_params=_params("arbitrary"),
        name="mla_prep",
    )(x2, g, wlat, qg, kvg, wq, wkv, cs)


def _dil_proj_kernel(x_ref, g_ref, w_ref, *rest):
    o_refs, h_ref, res_ref = rest[:DIL_GROUPS], rest[DIL_GROUPS], rest[DIL_GROUPS + 1]
    nqkv = 3 * DIL_HEADS_PER_GROUP
    h_ref[...] = _rms(x_ref[...], g_ref[...]).astype(BF)
    for gi, (_, dil) in enumerate(DIL_PATTERNS):
        sub = TM // dil
        res = _dot(h_ref[...], w_ref[:, gi * DIL_GROUP_DIM:(gi + 1) * DIL_GROUP_DIM])
        for hh in range(nqkv):
            blk = res[:, hh * DIL_HEAD_DIM:(hh + 1) * DIL_HEAD_DIM]
            if hh < DIL_HEADS_PER_GROUP:
                blk = blk * DIL_HEAD_DIM ** -0.5
            if dil == 1:
                o_refs[gi][hh, 0, 0] = blk.astype(BF)
            else:
                res_ref[hh] = blk
        if dil > 1:
            for hh in range(nqkv):
                for r in range(dil):
                    o_refs[gi][hh, 0, r] = res_ref[hh, pl.ds(r, sub, stride=dil), :].astype(BF)


def _dil_proj(x2, g, w, batch, seq):
    nseq = seq // TM
    nqkv = 3 * DIL_HEADS_PER_GROUP
    const = lambda i: (0, 0)
    return pl.pallas_call(
        _dil_proj_kernel,
        grid=(batch * nseq,),
        in_specs=[
            pl.BlockSpec((TM, D_MODEL), lambda i: (i, 0)),
            pl.BlockSpec((1, D_MODEL), const),
            pl.BlockSpec((D_MODEL, DIL_GROUPS * DIL_GROUP_DIM), const, pipeline_mode=pl.Buffered(1)),
        ],
        out_specs=[
            pl.BlockSpec((nqkv, 1, dil, TM // dil, DIL_HEAD_DIM), lambda i: (0, i // nseq, 0, i % nseq, 0))
            for _, dil in DIL_PATTERNS
        ],
        out_shape=[
            jax.ShapeDtypeStruct((nqkv, batch, dil, seq // dil, DIL_HEAD_DIM), BF)
            for _, dil in DIL_PATTERNS
        ],
        scratch_shapes=[pltpu.VMEM((TM, D_MODEL), BF), pltpu.VMEM((nqkv, TM, DIL_HEAD_DIM), F32)],
        compiler_params=_params("arbitrary"),
        name="dil_proj",
    )(x2, g, w)


def _flash_kernel(q_ref, k_ref, v_ref, o_ref, *scratch):
    nh = FLASH_HEADS
    m_refs, acc_refs = scratch[:nh], scratch[nh:2 * nh]
    s_refs = (scratch[2 * nh:3 * nh], scratch[3 * nh:4 * nh])
    qi = pl.program_id(2)
    for hh in range(nh):
        m_refs[hh][...] = jnp.full(m_refs[hh].shape, NEG, F32)
        acc_refs[hh][...] = jnp.zeros(acc_refs[hh].shape, F32)

    def key_rows(j):
        return pl.ds(pl.multiple_of(j * TK, TK), TK)

    def scores(j, slot):
        for hh in range(nh):
            s_refs[slot][hh][...] = _dot_nt(q_ref[hh], k_ref[hh, key_rows(j), :])

    def mask_diagonal(slot):
        keep = lax.broadcasted_iota(jnp.int32, (TQ, TK), 1) <= lax.broadcasted_iota(jnp.int32, (TQ, TK), 0)
        for hh in range(nh):
            s_refs[slot][hh][...] = jnp.where(keep, s_refs[slot][hh][...], NEG)

    def update(j, slot):
        for hh in range(nh):
            s = s_refs[slot][hh][...]
            m_old = m_refs[hh][...]
            m_new = jnp.maximum(m_old, jnp.max(s, axis=-1, keepdims=True))
            p = jnp.exp2(s - m_new)
            pv = _dot(p.astype(BF), v_ref[hh, key_rows(j), :])
            acc_refs[hh][...] = jnp.exp2(m_old - m_new) * acc_refs[hh][...] + pv
            m_refs[hh][...] = m_new

    scores(0, 0)

    @pl.when(qi == 0)
    def _():
        mask_diagonal(0)

    def body(j, carry):
        for slot in range(2):
            @pl.when(j % 2 == slot)
            def _(slot=slot):
                scores(j, slot)
                update(j - 1, 1 - slot)

            @pl.when((j % 2 == slot) & (j == qi))
            def _(slot=slot):
                mask_diagonal(slot)
        return carry

    lax.fori_loop(1, qi + 1, body, 0)

    for slot in range(2):
        @pl.when(qi % 2 == slot)
        def _(slot=slot):
            update(qi, slot)

    for hh in range(FLASH_HEADS):
        acc = acc_refs[hh][...]
        o_ref[hh] = (acc[:, :V_HEAD_DIM] * (1.0 / acc[:, V_HEAD_DIM:])).astype(BF)


def _mla_flash(q, k, v, batch, seq):
    nq = seq // TQ
    t = batch * seq
    hb = FLASH_HEADS
    return pl.pallas_call(
        _flash_kernel,
        grid=(batch, MLA_HEADS // hb, nq),
        in_specs=[
            pl.BlockSpec((hb, TQ, QK_SLOT), lambda b, h, i: (h, b * nq + i, 0)),
            pl.BlockSpec((hb, seq, QK_SLOT), lambda b, h, i: (h, b, 0)),
            pl.BlockSpec((hb, seq, V_SLOT), lambda b, h, i: (h, b, 0)),
        ],
        out_specs=pl.BlockSpec((hb, TQ, V_HEAD_DIM), lambda b, h, i: (h, b * nq + i, 0)),
        out_shape=jax.ShapeDtypeStruct((MLA_HEADS, t, V_HEAD_DIM), BF),
        scratch_shapes=([pltpu.VMEM((TQ, 1), F32)] * hb + [pltpu.VMEM((TQ, V_SLOT), F32)] * hb
                        + [pltpu.VMEM((TQ, TK), F32)] * (2 * hb)),
        compiler_params=_params("arbitrary", "arbitrary", "arbitrary"),
        name="mla_flash",
    )(q, k, v)


def _dilated_kernel(slope_ref, q_ref, k_ref, v_ref, o_ref, lse_ref, *, group, dil, nb):
    hs = pl.program_id(1)
    slope = slope_ref[group * DIL_HEADS_PER_GROUP + hs] * float(dil)
    qi = lax.broadcasted_iota(jnp.int32, (DIL_BLOCK, DIL_BLOCK), 0)
    ki = lax.broadcasted_iota(jnp.int32, (DIL_BLOCK, DIL_BLOCK), 1)
    dist_cur = (qi - ki).astype(F32)
    dist_prev = dist_cur + float(DIL_BLOCK)
    w_sub = float(DIL_PATTERNS[group][0] // dil)
    bias_cur = jnp.where((dist_cur >= 0.0) & (dist_cur <= w_sub), -slope * dist_cur, NEG)
    bias_prev = jnp.where(dist_prev <= w_sub, -slope * dist_prev, NEG)
    neg = jnp.full((DIL_BLOCK, DIL_BLOCK), NEG, F32)
    bias_inner = jnp.concatenate([bias_prev, bias_cur], axis=1)
    bias_first = jnp.concatenate([bias_cur, neg], axis=1)
    eye = qi == ki

    def window(n):
        start = max(n - 1, 0) * DIL_BLOCK
        return slice(start, start + 2 * DIL_BLOCK)

    blocks = [(r, n) for r in range(dil) for n in range(nb)]
    for g0 in range(0, len(blocks), DIL_BATCH):
        batch = blocks[g0:g0 + DIL_BATCH]
        scores = [_dot_nt(q_ref[0, 0, r, n * DIL_BLOCK:(n + 1) * DIL_BLOCK, :], k_ref[0, 0, r, window(n), :])
                  + (bias_first if n == 0 else bias_inner) for r, n in batch]
        maxes = [jnp.max(s, axis=-1, keepdims=True) for s in scores]
        probs = [jnp.exp(s - m) for s, m in zip(scores, maxes)]
        sums = [jnp.sum(p, axis=-1, keepdims=True) for p in probs]
        outs = [_dot(p.astype(BF), v_ref[0, 0, r, window(n), :]) for p, (r, n) in zip(probs, batch)]
        for (r, n), o, l in zip(batch, outs, sums):
            o_ref[0, 0, r, n * DIL_BLOCK:(n + 1) * DIL_BLOCK, :] = (o * (1.0 / l)).astype(BF)
        rows = [jnp.sum(jnp.where(eye, m + jnp.log(l), 0.0), axis=0, keepdims=True) for m, l in zip(maxes, sums)]
        lse_ref[0, 0, g0:g0 + DIL_BATCH, :] = jnp.concatenate(rows, axis=0)


def _dilated(qkv, slopes, group, batch, seq):
    _, dil = DIL_PATTERNS[group]
    sub = seq // dil
    nb = sub // DIL_BLOCK
    hg = DIL_HEADS_PER_GROUP
    blk = (1, 1, dil, sub, DIL_HEAD_DIM)

    def head(which):
        return lambda b, h: (which * hg + h, b, 0, 0, 0)

    return pl.pallas_call(
        functools.partial(_dilated_kernel, group=group, dil=dil, nb=nb),
        grid=(batch, hg),
        in_specs=[
            pl.BlockSpec(memory_space=pltpu.SMEM),
            pl.BlockSpec(blk, head(0)),
            pl.BlockSpec(blk, head(1)),
            pl.BlockSpec(blk, head(2)),
        ],
        out_specs=[
            pl.BlockSpec(blk, head(0)),
            pl.BlockSpec((1, 1, dil * nb, DIL_BLOCK), lambda b, h: (h, b, 0, 0)),
        ],
        out_shape=[
            jax.ShapeDtypeStruct((hg, batch, dil, sub, DIL_HEAD_DIM), BF),
            jax.ShapeDtypeStruct((hg, batch, dil * nb, DIL_BLOCK), F32),
        ],
        compiler_params=_params("arbitrary", "arbitrary"),
        name=f"dilated_d{dil}",
    )(slopes, qkv, qkv, qkv)


MERGE_TN = 512
MERGE_TM = 256


def _merge_kernel(x_ref, g_ref, wg_ref, bg_ref, oa_ref, o1_ref, o4_ref, o16_ref, lse_ref,
                  woa_ref, wob_ref, wout_ref, fg_ref, y_ref, h2_ref, h_ref, a_ref, b_ref, mg_ref, nat_ref):
    x = x_ref[...]
    tm = x.shape[0]
    h_ref[...] = _rms(x, g_ref[...]).astype(BF)
    for hd in range(MLA_HEADS):
        a_ref[:, hd * V_HEAD_DIM:(hd + 1) * V_HEAD_DIM] = oa_ref[hd]
    lse = lse_ref[...]
    o_refs = (o1_ref, o4_ref, o16_ref)
    for hs in range(DIL_HEADS_PER_GROUP):
        ls = [lse[:, gi * DIL_HEADS_PER_GROUP + hs:gi * DIL_HEADS_PER_GROUP + hs + 1] for gi in range(DIL_GROUPS)]
        mx = jnp.maximum(jnp.maximum(ls[0], ls[1]), ls[2])
        es = [jnp.exp(v - mx) for v in ls]
        inv = 1.0 / (es[0] + es[1] + es[2])
        comb = (es[0] * inv) * o_refs[0][hs, 0, 0].astype(F32)
        for gi in range(1, DIL_GROUPS):
            dil = DIL_PATTERNS[gi][1]
            for r in range(dil):
                nat_ref[pl.ds(r, tm // dil, stride=dil), :] = o_refs[gi][hs, 0, r].astype(F32)
            comb = comb + (es[gi] * inv) * nat_ref[...]
        b_ref[:, hs * DIL_HEAD_DIM:(hs + 1) * DIL_HEAD_DIM] = comb.astype(BF)
    for c in range(D_MODEL // MERGE_TN):
        cols = slice(c * MERGE_TN, (c + 1) * MERGE_TN)
        cols_b = slice(D_MODEL + c * MERGE_TN, D_MODEL + (c + 1) * MERGE_TN)
        gate_a = jax.nn.sigmoid(_dot(h_ref[...], wg_ref[:, cols]) + bg_ref[:, cols])
        gate_b = jax.nn.sigmoid(_dot(h_ref[...], wg_ref[:, cols_b]) + bg_ref[:, cols_b])
        o_a = _dot(a_ref[...], woa_ref[:, cols])
        o_b = _dot(b_ref[...], wob_ref[:, cols])
        mg_ref[:, cols] = (gate_a * o_a + gate_b * o_b).astype(BF)
    y = x + _dot(mg_ref[...], wout_ref[...])
    y_ref[...] = y
    h2_ref[...] = _rms(y, fg_ref[...]).astype(BF)


def _merge(x2, g, wg, bg, oa, o_dil, lse, woa, wob, wout, fg, seq):
    t = x2.shape[0]
    tm = MERGE_TM
    nseq = seq // tm
    const = lambda i: (0, 0)
    one = pl.Buffered(1)
    dil_specs = [
        pl.BlockSpec((DIL_HEADS_PER_GROUP, 1, dil, tm // dil, DIL_HEAD_DIM),
                     lambda i: (0, i // nseq, 0, i % nseq, 0))
        for _, dil in DIL_PATTERNS
    ]
    return pl.pallas_call(
        _merge_kernel,
        grid=(t // tm,),
        in_specs=[
            pl.BlockSpec((tm, D_MODEL), lambda i: (i, 0)),
            pl.BlockSpec((1, D_MODEL), const),
            pl.BlockSpec((D_MODEL, 2 * D_MODEL), const, pipeline_mode=one),
            pl.BlockSpec((1, 2 * D_MODEL), const),
            pl.BlockSpec((MLA_HEADS, tm, V_HEAD_DIM), lambda i: (0, i, 0)),
            *dil_specs,
            pl.BlockSpec((tm, DIL_HEADS), lambda i: (i, 0)),
            pl.BlockSpec((MLA_HEADS * V_HEAD_DIM, D_MODEL), const, pipeline_mode=one),
            pl.BlockSpec((DIL_HEADS_PER_GROUP * DIL_HEAD_DIM, D_MODEL), const, pipeline_mode=one),
            pl.BlockSpec((D_MODEL, D_MODEL), const, pipeline_mode=one),
            pl.BlockSpec((1, D_MODEL), const),
        ],
        out_specs=[pl.BlockSpec((tm, D_MODEL), lambda i: (i, 0)), pl.BlockSpec((tm, D_MODEL), lambda i: (i, 0))],
        out_shape=[jax.ShapeDtypeStruct((t, D_MODEL), F32), jax.ShapeDtypeStruct((t, D_MODEL), BF)],
        scratch_shapes=[
            pltpu.VMEM((tm, D_MODEL), BF),
            pltpu.VMEM((tm, MLA_HEADS * V_HEAD_DIM), BF),
            pltpu.VMEM((tm, DIL_HEADS_PER_GROUP * DIL_HEAD_DIM), BF),
            pltpu.VMEM((tm, D_MODEL), BF),
            pltpu.VMEM((tm, DIL_HEAD_DIM), F32),
        ],
        compiler_params=_params("arbitrary"),
        name="merge",
    )(x2, g, wg, bg, oa, *o_dil, lse, woa, wob, wout, fg)


def _ffn_up_kernel(h_ref, w_ref, cw_ref, a_ref, u0_ref, u1_ref, *, tiles, tiles_per_seq):
    i = pl.program_id(1)
    tm = h_ref.shape[0]
    u_refs = (u0_ref, u1_ref)

    def project(slot, r0, n):
        rows = slice(SUBLANE + r0, SUBLANE + r0 + FFN_PIECE)
        cols = slice(n * FFN_NSPLIT, (n + 1) * FFN_NSPLIT)
        u_refs[slot][rows, cols] = _dot(h_ref[r0:r0 + FFN_PIECE, :], w_ref[:, cols])

    def activate(slot, r1):
        u_ref = u_refs[slot]
        groups = FFN_ROWS // SUBLANE

        def conv(cols):
            window = u_ref[r1:r1 + FFN_ROWS + SUBLANE, cols]
            out = None
            for tap in range(CONV_WIDTH):
                back = CONV_WIDTH - 1 - tap
                shifted = window if back == 0 else pltpu.roll(window, back, 0)
                x = shifted[SUBLANE:].reshape(groups, SUBLANE, FF_CHUNK)
                term = cw_ref[tap * SUBLANE:(tap + 1) * SUBLANE, cols][None] * x
                out = term if out is None else out + term
            out = out + cw_ref[CONV_WIDTH * SUBLANE:(CONV_WIDTH + 1) * SUBLANE, cols][None]
            return out.reshape(FFN_ROWS, FF_CHUNK)

        up = conv(slice(0, FF_CHUNK))
        gate = conv(slice(FF_CHUNK, 2 * FF_CHUNK))
        a_ref[r1:r1 + FFN_ROWS, :] = (jax.nn.silu(gate) * up).astype(BF)

    def run(project_slot=None, activate_slot=None):
        if project_slot is not None:
            seq_start = (i % tiles_per_seq) == 0
            halo = u_refs[1 - project_slot][tm:]
            u_refs[project_slot][:SUBLANE] = jnp.where(seq_start, 0.0, halo)
        kslices = D_MODEL // FFN_KSLICE
        per_slice = FFN_PIECE // FFN_ROWS // kslices
        for r0 in range(0, tm, FFN_PIECE):
            acc = None
            for ks in range(kslices):
                if project_slot is not None:
                    kk = slice(ks * FFN_KSLICE, (ks + 1) * FFN_KSLICE)
                    part = _dot(h_ref[r0:r0 + FFN_PIECE, kk], w_ref[kk, :])
                    acc = part if acc is None else acc + part
                if activate_slot is not None:
                    for k in range(per_slice):
                        activate(activate_slot, r0 + (ks * per_slice + k) * FFN_ROWS)
            if project_slot is not None:
                u_refs[project_slot][SUBLANE + r0:SUBLANE + r0 + FFN_PIECE, :] = acc

    @pl.when(i == 0)
    def _():
        u0_ref[:SUBLANE] = jnp.zeros((SUBLANE, 2 * FF_CHUNK), F32)
        for r0 in range(0, tm, FFN_PIECE):
            for n in range(2 * FF_CHUNK // FFN_NSPLIT):
                project(0, r0, n)

    for slot in range(2):
        @pl.when((i > 0) & (i < tiles) & (i % 2 == slot))
        def _(slot=slot):
            run(project_slot=slot, activate_slot=1 - slot)

    @pl.when(i == tiles)
    def _():
        run(activate_slot=(tiles - 1) % 2)


def _ffn_down_kernel(a_ref, wd_ref, x_ref, fg_ref, y_ref):
    y_ref[...] = _rms(x_ref[...] + _dot(a_ref[...], wd_ref[...]), fg_ref[...])


def _ffn_up(h2, wu, cw, seq):
    t = h2.shape[0]
    tm = FFN_TM
    tiles = t // tm
    return pl.pallas_call(
        functools.partial(_ffn_up_kernel, tiles=tiles, tiles_per_seq=seq // tm),
        grid=(FF_CHUNKS, tiles + 1),
        in_specs=[
            pl.BlockSpec((tm, D_MODEL), lambda c, i: (jnp.minimum(i, tiles - 1), 0)),
            pl.BlockSpec((D_MODEL, 2 * FF_CHUNK), lambda c, i: (0, c)),
            pl.BlockSpec(((CONV_WIDTH + 1) * SUBLANE, 2 * FF_CHUNK), lambda c, i: (0, c)),
        ],
        out_specs=pl.BlockSpec((tm, FF_CHUNK), lambda c, i: (jnp.maximum(i - 1, 0), c)),
        out_shape=jax.ShapeDtypeStruct((t, D_FF_PAD), BF),
        scratch_shapes=[
            pltpu.VMEM((tm + SUBLANE, 2 * FF_CHUNK), F32),
            pltpu.VMEM((tm + SUBLANE, 2 * FF_CHUNK), F32),
        ],
        compiler_params=_params("arbitrary", "arbitrary"),
        name="ffn_up",
    )(h2, wu, cw)


def _ffn_down(act, wd, x1, fg):
    t = x1.shape[0]
    tm = FFN_DOWN_TM
    return pl.pallas_call(
        _ffn_down_kernel,
        grid=(t // tm,),
        in_specs=[
            pl.BlockSpec((tm, D_FF_PAD), lambda i: (i, 0)),
            pl.BlockSpec((D_FF_PAD, D_MODEL), lambda i: (0, 0), pipeline_mode=pl.Buffered(1)),
            pl.BlockSpec((tm, D_MODEL), lambda i: (i, 0)),
            pl.BlockSpec((1, D_MODEL), lambda i: (0, 0)),
        ],
        out_specs=pl.BlockSpec((tm, D_MODEL), lambda i: (i, 0)),
        out_shape=jax.ShapeDtypeStruct((t, D_MODEL), F32),
        compiler_params=_params("arbitrary"),
        name="ffn_down",
    )(act, wd, x1, fg)


def _rotate_half_cols(w):
    half = w.shape[-1] // 2
    return jnp.concatenate([-w[..., half:], w[..., :half]], axis=-1)


PREP_PIECE = 512
PREP_ROWS = 256


def _relayout_kernel(src_ref, *out_refs, plans):
    ncols = src_ref.shape[1]
    for out_ref, plan in zip(out_refs, plans):
        col = 0
        for start, width in plan:
            for off in range(0, width, PREP_PIECE):
                w = min(PREP_PIECE, width - off)
                dst = slice(col + off, col + off + w)
                if start is None:
                    out_ref[:, dst] = jnp.zeros((out_ref.shape[0], w), out_ref.dtype)
                    continue
                lead = (start + off) % LANE
                lo = start + off - lead
                hi = min(-(-(start + off + w) // LANE) * LANE, ncols)
                out_ref[:, dst] = src_ref[:, lo:hi][:, lead:lead + w].astype(out_ref.dtype)
            col += width


def _relayout(src, plans, dtype, row_tile, name):
    rows, ncols = src.shape
    widths = [sum(w for _, w in plan) for plan in plans]
    assert rows % row_tile == 0 and all(w % LANE == 0 for plan in plans for _, w in plan)
    outs = pl.pallas_call(
        functools.partial(_relayout_kernel, plans=plans),
        grid=(rows // row_tile,),
        in_specs=[pl.BlockSpec((row_tile, ncols), lambda i: (i, 0))],
        out_specs=[pl.BlockSpec((row_tile, w), lambda i: (i, 0)) for w in widths],
        out_shape=[jax.ShapeDtypeStruct((rows, w), dtype) for w in widths],
        compiler_params=_params("arbitrary"),
        name=name,
    )(src)
    return outs


def _transpose_rows_kernel(src_hbm, out_ref, buf_ref, sem, *, row_start, chunks):
    j = pl.program_id(0)

    def fetch(jj, slot):
        rows = pl.ds(pl.multiple_of(row_start(jj), SUBLANE), PREP_PIECE)
        return pltpu.make_async_copy(src_hbm.at[rows, :], buf_ref.at[slot], sem.at[slot])

    @pl.when(j == 0)
    def _():
        fetch(j, 0).start()

    @pl.when(j + 1 < chunks)
    def _():
        fetch(j + 1, (j + 1) % 2).start()

    fetch(j, j % 2).wait()
    out_ref[...] = buf_ref[j % 2].T.astype(out_ref.dtype)


def _transpose_rows(src, row_start, chunks, name):
    ncols = src.shape[1]
    return pl.pallas_call(
        functools.partial(_transpose_rows_kernel, row_start=row_start, chunks=chunks),
        grid=(chunks,),
        in_specs=[pl.BlockSpec(memory_space=pl.ANY)],
        out_specs=pl.BlockSpec((ncols, PREP_PIECE), lambda j: (0, j)),
        out_shape=jax.ShapeDtypeStruct((ncols, chunks * PREP_PIECE), BF),
        scratch_shapes=[pltpu.VMEM((2, PREP_PIECE, ncols), F32), pltpu.SemaphoreType.DMA((2,))],
        compiler_params=_params("arbitrary"),
        name=name,
    )(src)


def _ff_chunk_plan():
    plan = []
    for c in range(FF_CHUNKS):
        lo, hi = c * FF_CHUNK, min((c + 1) * FF_CHUNK, D_FF)
        for half in (0, D_FF):
            plan.append((half + lo, hi - lo))
            if hi - lo < FF_CHUNK:
                plan.append((None, FF_CHUNK - (hi - lo)))
    return plan


def kernel(x, attn_norm_g, w_in, b_gate, q_norm_g, w_uq, kv_norm_g, w_ukv, w_o_mla, w_o_dil,
           w_out, ffn_norm_g, w_up, conv_w, conv_b, w_down, final_norm_g):
    batch, seq, _ = x.shape
    assert w_in.shape[0] == 1, "single-layer block"
    assert seq % TM == 0 and seq % FFN_TM == 0 and seq % TQ == 0 and TQ == TK and seq % (16 * DIL_BLOCK) == 0
    assert FF_CHUNKS >= 3
    t = batch * seq
    x2 = x.reshape(t, D_MODEL)

    dqkv = DIL_HEADS * DIL_HEAD_DIM
    o = np.cumsum((0, Q_LORA_RANK, KV_LORA_RANK, QK_ROPE_DIM, dqkv, dqkv, dqkv, D_MODEL, D_MODEL))
    wt = w_in[0].T
    w_head = _transpose_rows(wt, lambda j: j * PREP_PIECE, -(-int(o[3]) // PREP_PIECE), "relayout_w_lat")
    wlat = jnp.concatenate([w_head[:, o[0]:o[3]], _rotate_half_cols(w_head[:, o[2]:o[3]])], axis=1)
    gw = DIL_HEADS_PER_GROUP * DIL_HEAD_DIM
    assert gw == PREP_PIECE and D_MODEL % PREP_PIECE == 0
    dil_start = lambda j: int(o[3]) + (j % 3) * (DIL_HEADS * DIL_HEAD_DIM) + (j // 3) * gw
    wdil = _transpose_rows(wt, dil_start, 3 * DIL_GROUPS, "relayout_w_dil")
    wgate = _transpose_rows(wt, lambda j: int(o[6]) + j * PREP_PIECE, 2 * D_MODEL // PREP_PIECE, "relayout_w_gate")
    wq = w_uq[0].astype(BF).reshape(Q_LORA_RANK, MLA_HEADS, QK_NOPE_DIM + QK_ROPE_DIM)
    wq_pe = wq[..., QK_NOPE_DIM:]
    wq = jnp.concatenate([wq[..., :QK_NOPE_DIM], wq_pe, _rotate_half_cols(wq_pe)], axis=-1)
    wq = wq.reshape(Q_LORA_RANK, MLA_HEADS * QK_SLOT)
    wkv = w_ukv[0].astype(BF)
    (wu,) = _relayout(w_up[0], [_ff_chunk_plan()], BF, PREP_ROWS, "relayout_w_up")
    cw = jnp.repeat(jnp.concatenate([conv_w[0], conv_b], axis=0), SUBLANE, axis=0)
    (cw,) = _relayout(cw, [_ff_chunk_plan()], F32, cw.shape[0], "relayout_conv")
    wd =jnp.concatenate([w_down[0].astype(BF), jnp.zeros((D_FF_PAD - D_FF, D_MODEL), BF)], axis=0)

    pos = jnp.arange(seq, dtype=F32)
    inv_freq = ROPE_THETA ** (-jnp.arange(0, QK_ROPE_DIM, 2, dtype=F32) / QK_ROPE_DIM)
    ang = pos[:, None] * inv_freq[None, :]
    cs = jnp.concatenate([jnp.cos(ang), jnp.cos(ang), jnp.sin(ang), jnp.sin(ang)], axis=1)
    slopes = 2.0 ** (-ALIBI_MAX_BIAS * jnp.arange(1, DIL_HEADS + 1, dtype=F32) / DIL_HEADS)

    q, k, v = _mla_prep(x2, attn_norm_g, wlat, q_norm_g, kv_norm_g, wq, wkv, cs, seq)
    o_mla = _mla_flash(q, k, v, batch, seq)
    qkv_dil = _dil_proj(x2, attn_norm_g, wdil, batch, seq)
    o_dil, lse_dil = [], []
    for gi, (_, dil) in enumerate(DIL_PATTERNS):
        o_g, lse_g = _dilated(qkv_dil[gi], slopes, gi, batch, seq)
        nb = seq // dil // DIL_BLOCK
        o_dil.append(o_g)
        lse_g = lse_g.reshape(DIL_HEADS_PER_GROUP, batch, dil, nb, DIL_BLOCK)
        lse_dil.append(lse_g.transpose(1, 3, 4, 2, 0).reshape(t, DIL_HEADS_PER_GROUP))
    lse = jnp.concatenate(lse_dil, axis=1)
    x1, h2 = _merge(x2, attn_norm_g, wgate, b_gate, o_mla, o_dil, lse,
                    w_o_mla[0].astype(BF), w_o_dil[0].astype(BF), w_out[0].astype(BF), ffn_norm_g, seq)

    act = _ffn_up(h2, wu, cw, seq)
    y = _ffn_down(act, wd, x1, final_norm_g.reshape(1, D_MODEL))
    return y.reshape(batch, seq, D_MODEL)
```

```python
import functools
import math

import numpy as np
import jax
import jax.numpy as jnp
from jax import lax
from jax.experimental import pallas as pl
from jax.experimental.pallas import tpu as pltpu

D_MODEL = 2048
MLA_HEADS = 8
QK_NOPE_DIM = 128
QK_ROPE_DIM = 64
V_HEAD_DIM = 128
Q_LORA_RANK = 512
KV_LORA_RANK = 256
ROPE_THETA = 10000.0
DIL_PATTERNS = ((128, 1), (512, 4), (2048, 16))
DIL_GROUPS = 3
DIL_HEADS_PER_GROUP = 4
DIL_HEADS = DIL_GROUPS * DIL_HEADS_PER_GROUP
DIL_HEAD_DIM = 128
DIL_BLOCK = 128
ALIBI_MAX_BIAS = 8.0
D_FF = 5504
CONV_WIDTH = 3
NORM_EPS = 1e-6

LANE = 128
SUBLANE = 8
VMEM_LIMIT = 56 * 1024 * 1024

LAT_DIM = Q_LORA_RANK + KV_LORA_RANK + 2 * QK_ROPE_DIM
QK_SLOT = 2 * LANE
V_SLOT = 2 * LANE
DIL_GROUP_DIM = 3 * DIL_HEADS_PER_GROUP * DIL_HEAD_DIM
FF_CHUNK = 512
FFN_TM = 1024
FFN_DOWN_TM = 256
FFN_PIECE = 256
FFN_KSLICE = 256
FFN_ROWS = 32
D_FF_PAD = -(-D_FF // FF_CHUNK) * FF_CHUNK
FF_CHUNKS = D_FF_PAD // FF_CHUNK
NEG = -1e30

TM = 512
TQ = 512
TK = 512
FLASH_HEADS = 4
DIL_BATCH = 8

BF = jnp.bfloat16
F32 = jnp.float32


def _params(*sem):
    return pltpu.CompilerParams(dimension_semantics=sem, vmem_limit_bytes=VMEM_LIMIT)


def _rms(x, g):
    return x * lax.rsqrt(jnp.mean(x * x, axis=-1, keepdims=True) + NORM_EPS) * g


def _dot(a, b):
    return jnp.dot(a, b, preferred_element_type=F32)


def _dot_nt(a, b):
    return lax.dot_general(a, b, (((1,), (1,)), ((), ())), preferred_element_type=F32)


def _rope_pair(t):
    lane = lax.broadcasted_iota(jnp.int32, t.shape, 1)
    return jnp.where(lane < QK_ROPE_DIM, t + pltpu.roll(t, QK_ROPE_DIM, 1), 0.0)


def _mla_prep_kernel(x_ref, g_ref, wlat_ref, qg_ref, kvg_ref, wq_ref, wkv_ref, cs_ref,
                     q_ref, k_ref, v_ref):
    h = _rms(x_ref[...], g_ref[...]).astype(BF)
    lat = _dot(h, wlat_ref[...])
    cq = _rms(lat[:, :Q_LORA_RANK], qg_ref[...]).astype(BF)
    ckv = _rms(lat[:, Q_LORA_RANK:Q_LORA_RANK + KV_LORA_RANK], kvg_ref[...]).astype(BF)
    q = _dot(cq, wq_ref[...])
    kv = _dot(ckv, wkv_ref[...])
    cs = cs_ref[...]
    scale = (QK_NOPE_DIM + QK_ROPE_DIM) ** -0.5 * math.log2(math.e)
    k_rope = _rope_pair(lat[:, Q_LORA_RANK + KV_LORA_RANK:] * cs).astype(BF)
    ones = jnp.ones((x_ref.shape[0], V_SLOT - V_HEAD_DIM), BF)
    for hd in range(MLA_HEADS):
        qs = q[:, hd * QK_SLOT:(hd + 1) * QK_SLOT]
        q_ref[hd, :, :LANE] = (qs[:, :LANE] * scale).astype(BF)
        q_ref[hd, :, LANE:] = _rope_pair(qs[:, LANE:] * (cs * scale)).astype(BF)
        k_ref[hd, :, :LANE] = kv[:, hd * 256:hd * 256 + LANE].astype(BF)
        k_ref[hd, :, LANE:] = k_rope
        v_ref[hd, :, :V_HEAD_DIM] = kv[:, hd * 256 + LANE:(hd + 1) * 256].astype(BF)
        v_ref[hd, :, V_HEAD_DIM:] = ones


def _mla_prep(x2, g, wlat, qg, kvg, wq, wkv, cs, seq):
    t = x2.shape[0]
    nseq = seq // TM
    const = lambda i: (0, 0)
    return pl.pallas_call(
        _mla_prep_kernel,
        grid=(t // TM,),
        in_specs=[
            pl.BlockSpec((TM, D_MODEL), lambda i: (i, 0)),
            pl.BlockSpec((1, D_MODEL), const),
            pl.BlockSpec((D_MODEL, LAT_DIM), const),
            pl.BlockSpec((1, Q_LORA_RANK), const),
            pl.BlockSpec((1, KV_LORA_RANK), const),
            pl.BlockSpec((Q_LORA_RANK, MLA_HEADS * QK_SLOT), const),
            pl.BlockSpec((KV_LORA_RANK, MLA_HEADS * 256), const),
            pl.BlockSpec((TM, LANE), lambda i: (i % nseq, 0)),
        ],
        out_specs=[
            pl.BlockSpec((MLA_HEADS, TM, QK_SLOT), lambda i: (0, i, 0)),
            pl.BlockSpec((MLA_HEADS, TM, QK_SLOT), lambda i: (0, i, 0)),
            pl.BlockSpec((MLA_HEADS, TM, V_SLOT), lambda i: (0, i, 0)),
        ],
        out_shape=[
            jax.ShapeDtypeStruct((MLA_HEADS, t, QK_SLOT), BF),
            jax.ShapeDtypeStruct((MLA_HEADS, t, QK_SLOT), BF),
            jax.ShapeDtypeStruct((MLA_HEADS, t, V_SLOT), BF),
        ],
        compiler_params=_params("arbitrary"),
        name="mla_prep",
    )(x2, g, wlat, qg, kvg, wq, wkv, cs)


def _dil_proj_kernel(x_ref, g_ref, w_ref, *rest):
    o_refs, h_ref, res_ref = rest[:DIL_GROUPS], rest[DIL_GROUPS], rest[DIL_GROUPS + 1]
    nqkv = 3 * DIL_HEADS_PER_GROUP
    h_ref[...] = _rms(x_ref[...], g_ref[...]).astype(BF)
    for gi, (_, dil) in enumerate(DIL_PATTERNS):
        sub = TM // dil
        res = _dot(h_ref[...], w_ref[:, gi * DIL_GROUP_DIM:(gi + 1) * DIL_GROUP_DIM])
        for hh in range(nqkv):
            blk = res[:, hh * DIL_HEAD_DIM:(hh + 1) * DIL_HEAD_DIM]
            if hh < DIL_HEADS_PER_GROUP:
                blk = blk * DIL_HEAD_DIM ** -0.5
            if dil == 1:
                o_refs[gi][hh, 0, 0] = blk.astype(BF)
            else:
                res_ref[hh] = blk
        if dil > 1:
            for hh in range(nqkv):
                for r in range(dil):
                    o_refs[gi][hh, 0, r] = res_ref[hh, pl.ds(r, sub, stride=dil), :].astype(BF)


def _dil_proj(x2, g, w, batch, seq):
    nseq = seq // TM
    nqkv = 3 * DIL_HEADS_PER_GROUP
    const = lambda i: (0, 0)
    return pl.pallas_call(
        _dil_proj_kernel,
        grid=(batch * nseq,),
        in_specs=[
            pl.BlockSpec((TM, D_MODEL), lambda i: (i, 0)),
            pl.BlockSpec((1, D_MODEL), const),
            pl.BlockSpec((D_MODEL, DIL_GROUPS * DIL_GROUP_DIM), const, pipeline_mode=pl.Buffered(1)),
        ],
        out_specs=[
            pl.BlockSpec((nqkv, 1, dil, TM // dil, DIL_HEAD_DIM), lambda i: (0, i // nseq, 0, i % nseq, 0))
            for _, dil in DIL_PATTERNS
        ],
        out_shape=[
            jax.ShapeDtypeStruct((nqkv, batch, dil, seq // dil, DIL_HEAD_DIM), BF)
            for _, dil in DIL_PATTERNS
        ],
        scratch_shapes=[pltpu.VMEM((TM, D_MODEL), BF), pltpu.VMEM((nqkv, TM, DIL_HEAD_DIM), F32)],
        compiler_params=_params("arbitrary"),
        name="dil_proj",
    )(x2, g, w)


def _flash_kernel(q_ref, k_ref, v_ref, o_ref, *scratch):
    nh = FLASH_HEADS
    m_refs, acc_refs = scratch[:nh], scratch[nh:2 * nh]
    s_refs = (scratch[2 * nh:3 * nh], scratch[3 * nh:4 * nh])
    qi = pl.program_id(2)
    for hh in range(nh):
        m_refs[hh][...] = jnp.full(m_refs[hh].shape, NEG, F32)
        acc_refs[hh][...] = jnp.zeros(acc_refs[hh].shape, F32)

    def key_rows(j):
        return pl.ds(pl.multiple_of(j * TK, TK), TK)

    def scores(j, slot):
        for hh in range(nh):
            s_refs[slot][hh][...] = _dot_nt(q_ref[hh], k_ref[hh, key_rows(j), :])

    def mask_diagonal(slot):
        keep = lax.broadcasted_iota(jnp.int32, (TQ, TK), 1) <= lax.broadcasted_iota(jnp.int32, (TQ, TK), 0)
        for hh in range(nh):
            s_refs[slot][hh][...] = jnp.where(keep, s_refs[slot][hh][...], NEG)

    def update(j, slot):
        for hh in range(nh):
            s = s_refs[slot][hh][...]
            m_old = m_refs[hh][...]
            m_new = jnp.maximum(m_old, jnp.max(s, axis=-1, keepdims=True))
            p = jnp.exp2(s - m_new)
            pv = _dot(p.astype(BF), v_ref[hh, key_rows(j), :])
            acc_refs[hh][...] = jnp.exp2(m_old - m_new) * acc_refs[hh][...] + pv
            m_refs[hh][...] = m_new

    scores(0, 0)

    @pl.when(qi == 0)
    def _():
        mask_diagonal(0)

    def body(j, carry):
        for slot in range(2):
            @pl.when(j % 2 == slot)
            def _(slot=slot):
                scores(j, slot)
                update(j - 1, 1 - slot)

            @pl.when((j % 2 == slot) & (j == qi))
            def _(slot=slot):
                mask_diagonal(slot)
        return carry

    lax.fori_loop(1, qi + 1, body, 0)

    for slot in range(2):
        @pl.when(qi % 2 == slot)
        def _(slot=slot):
            update(qi, slot)

    for hh in range(FLASH_HEADS):
        acc = acc_refs[hh][...]
        o_ref[hh] = (acc[:, :V_HEAD_DIM] * (1.0 / acc[:, V_HEAD_DIM:])).astype(BF)


def _mla_flash(q, k, v, batch, seq):
    nq = seq // TQ
    t = batch * seq
    hb = FLASH_HEADS
    return pl.pallas_call(
        _flash_kernel,
        grid=(batch, MLA_HEADS // hb, nq),
        in_specs=[
            pl.BlockSpec((hb, TQ, QK_SLOT), lambda b, h, i: (h, b * nq + i, 0)),
            pl.BlockSpec((hb, seq, QK_SLOT), lambda b, h, i: (h, b, 0)),
            pl.BlockSpec((hb, seq, V_SLOT), lambda b, h, i: (h, b, 0)),
        ],
        out_specs=pl.BlockSpec((hb, TQ, V_HEAD_DIM), lambda b, h, i: (h, b * nq + i, 0)),
        out_shape=jax.ShapeDtypeStruct((MLA_HEADS, t, V_HEAD_DIM), BF),
        scratch_shapes=([pltpu.VMEM((TQ, 1), F32)] * hb + [pltpu.VMEM((TQ, V_SLOT), F32)] * hb
                        + [pltpu.VMEM((TQ, TK), F32)] * (2 * hb)),
        compiler_params=_params("arbitrary", "arbitrary", "arbitrary"),
        name="mla_flash",
    )(q, k, v)


def _dilated_kernel(slope_ref, q_ref, k_ref, v_ref, o_ref, lse_ref, *, group, dil, nb):
    hs = pl.program_id(1)
    slope = slope_ref[group * DIL_HEADS_PER_GROUP + hs] * float(dil)
    qi = lax.broadcasted_iota(jnp.int32, (DIL_BLOCK, DIL_BLOCK), 0)
    ki = lax.broadcasted_iota(jnp.int32, (DIL_BLOCK, DIL_BLOCK), 1)
    dist_cur = (qi - ki).astype(F32)
    dist_prev = dist_cur + float(DIL_BLOCK)
    w_sub = float(DIL_PATTERNS[group][0] // dil)
    bias_cur = jnp.where((dist_cur >= 0.0) & (dist_cur <= w_sub), -slope * dist_cur, NEG)
    bias_prev = jnp.where(dist_prev <= w_sub, -slope * dist_prev, NEG)
    neg = jnp.full((DIL_BLOCK, DIL_BLOCK), NEG, F32)
    bias_inner = jnp.concatenate([bias_prev, bias_cur], axis=1)
    bias_first = jnp.concatenate([bias_cur, neg], axis=1)
    eye = qi == ki

    def window(n):
        start = max(n - 1, 0) * DIL_BLOCK
        return slice(start, start + 2 * DIL_BLOCK)

    blocks = [(r, n) for r in range(dil) for n in range(nb)]
    for g0 in range(0, len(blocks), DIL_BATCH):
        batch = blocks[g0:g0 + DIL_BATCH]
        scores = [_dot_nt(q_ref[0, 0, r, n * DIL_BLOCK:(n + 1) * DIL_BLOCK, :], k_ref[0, 0, r, window(n), :])
                  + (bias_first if n == 0 else bias_inner) for r, n in batch]
        maxes = [jnp.max(s, axis=-1, keepdims=True) for s in scores]
        probs = [jnp.exp(s - m) for s, m in zip(scores, maxes)]
        sums = [jnp.sum(p, axis=-1, keepdims=True) for p in probs]
        outs = [_dot(p.astype(BF), v_ref[0, 0, r, window(n), :]) for p, (r, n) in zip(probs, batch)]
        for (r, n), o, l in zip(batch, outs, sums):
            o_ref[0, 0, r, n * DIL_BLOCK:(n + 1) * DIL_BLOCK, :] = (o * (1.0 / l)).astype(BF)
        rows = [jnp.sum(jnp.where(eye, m + jnp.log(l), 0.0), axis=0, keepdims=True) for m, l in zip(maxes, sums)]
        lse_ref[0, 0, g0:g0 + DIL_BATCH, :] = jnp.concatenate(rows, axis=0)


def _dilated(qkv, slopes, group, batch, seq):
    _, dil = DIL_PATTERNS[group]
    sub = seq // dil
    nb = sub // DIL_BLOCK
    hg = DIL_HEADS_PER_GROUP
    blk = (1, 1, dil, sub, DIL_HEAD_DIM)

    def head(which):
        return lambda b, h: (which * hg + h, b, 0, 0, 0)

    return pl.pallas_call(
        functools.partial(_dilated_kernel, group=group, dil=dil, nb=nb),
        grid=(batch, hg),
        in_specs=[
            pl.BlockSpec(memory_space=pltpu.SMEM),
            pl.BlockSpec(blk, head(0)),
            pl.BlockSpec(blk, head(1)),
            pl.BlockSpec(blk, head(2)),
        ],
        out_specs=[
            pl.BlockSpec(blk, head(0)),
            pl.BlockSpec((1, 1, dil * nb, DIL_BLOCK), lambda b, h: (h, b, 0, 0)),
        ],
        out_shape=[
            jax.ShapeDtypeStruct((hg, batch, dil, sub, DIL_HEAD_DIM), BF),
            jax.ShapeDtypeStruct((hg, batch, dil * nb, DIL_BLOCK), F32),
        ],
        compiler_params=_params("arbitrary", "arbitrary"),
        name=f"dilated_d{dil}",
    )(slopes, qkv, qkv, qkv)


MERGE_TN = 512
MERGE_TM = 256


def _merge_kernel(x_ref, g_ref, wg_ref, bg_ref, oa_ref, o1_ref, o4_ref, o16_ref, lse_ref,
                  woa_ref, wob_ref, wout_ref, fg_ref, y_ref, h2_ref, h_ref, a_ref, b_ref, mg_ref, nat_ref):
    x = x_ref[...]
    tm = x.shape[0]
    h_ref[...] = _rms(x, g_ref[...]).astype(BF)
    for hd in range(MLA_HEADS):
        a_ref[:, hd * V_HEAD_DIM:(hd + 1) * V_HEAD_DIM] = oa_ref[hd]
    lse = lse_ref[...]
    o_refs = (o1_ref, o4_ref, o16_ref)
    for hs in range(DIL_HEADS_PER_GROUP):
        ls = [lse[:, gi * DIL_HEADS_PER_GROUP + hs:gi * DIL_HEADS_PER_GROUP + hs + 1] for gi in range(DIL_GROUPS)]
        mx = jnp.maximum(jnp.maximum(ls[0], ls[1]), ls[2])
        es = [jnp.exp(v - mx) for v in ls]
        inv = 1.0 / (es[0] + es[1] + es[2])
        comb = (es[0] * inv) * o_refs[0][hs, 0, 0].astype(F32)
        for gi in range(1, DIL_GROUPS):
            dil = DIL_PATTERNS[gi][1]
            for r in range(dil):
                nat_ref[pl.ds(r, tm // dil, stride=dil), :] = o_refs[gi][hs, 0, r].astype(F32)
            comb = comb + (es[gi] * inv) * nat_ref[...]
        b_ref[:, hs * DIL_HEAD_DIM:(hs + 1) * DIL_HEAD_DIM] = comb.astype(BF)
    for c in range(D_MODEL // MERGE_TN):
        cols = slice(c * MERGE_TN, (c + 1) * MERGE_TN)
        cols_b = slice(D_MODEL + c * MERGE_TN, D_MODEL + (c + 1) * MERGE_TN)
        gate_a = jax.nn.sigmoid(_dot(h_ref[...], wg_ref[:, cols]) + bg_ref[:, cols])
        gate_b = jax.nn.sigmoid(_dot(h_ref[...], wg_ref[:, cols_b]) + bg_ref[:, cols_b])
        o_a = _dot(a_ref[...], woa_ref[:, cols])
        o_b = _dot(b_ref[...], wob_ref[:, cols])
        mg_ref[:, cols] = (gate_a * o_a + gate_b * o_b).astype(BF)
    y = x + _dot(mg_ref[...], wout_ref[...])
    y_ref[...] = y
    h2_ref[...] = _rms(y, fg_ref[...]).astype(BF)


def _merge(x2, g, wg, bg, oa, o_dil, lse, woa, wob, wout, fg, seq):
    t = x2.shape[0]
    tm = MERGE_TM
    nseq = seq // tm
    const = lambda i: (0, 0)
    one = pl.Buffered(1)
    dil_specs = [
        pl.BlockSpec((DIL_HEADS_PER_GROUP, 1, dil, tm // dil, DIL_HEAD_DIM),
                     lambda i: (0, i // nseq, 0, i % nseq, 0))
        for _, dil in DIL_PATTERNS
    ]
    return pl.pallas_call(
        _merge_kernel,
        grid=(t // tm,),
        in_specs=[
            pl.BlockSpec((tm, D_MODEL), lambda i: (i, 0)),
            pl.BlockSpec((1, D_MODEL), const),
            pl.BlockSpec((D_MODEL, 2 * D_MODEL), const, pipeline_mode=one),
            pl.BlockSpec((1, 2 * D_MODEL), const),
            pl.BlockSpec((MLA_HEADS, tm, V_HEAD_DIM), lambda i: (0, i, 0)),
            *dil_specs,
            pl.BlockSpec((tm, DIL_HEADS), lambda i: (i, 0)),
            pl.BlockSpec((MLA_HEADS * V_HEAD_DIM, D_MODEL), const, pipeline_mode=one),
            pl.BlockSpec((DIL_HEADS_PER_GROUP * DIL_HEAD_DIM, D_MODEL), const, pipeline_mode=one),
            pl.BlockSpec((D_MODEL, D_MODEL), const, pipeline_mode=one),
            pl.BlockSpec((1, D_MODEL), const),
        ],
        out_specs=[pl.BlockSpec((tm, D_MODEL), lambda i: (i, 0)), pl.BlockSpec((tm, D_MODEL), lambda i: (i, 0))],
        out_shape=[jax.ShapeDtypeStruct((t, D_MODEL), F32), jax.ShapeDtypeStruct((t, D_MODEL), BF)],
        scratch_shapes=[
            pltpu.VMEM((tm, D_MODEL), BF),
            pltpu.VMEM((tm, MLA_HEADS * V_HEAD_DIM), BF),
            pltpu.VMEM((tm, DIL_HEADS_PER_GROUP * DIL_HEAD_DIM), BF),
            pltpu.VMEM((tm, D_MODEL), BF),
            pltpu.VMEM((tm, DIL_HEAD_DIM), F32),
        ],
        compiler_params=_params("arbitrary"),
        name="merge",
    )(x2, g, wg, bg, oa, *o_dil, lse, woa, wob, wout, fg)


def _ffn_up_kernel(h_ref, w_ref, cw_ref, a_ref, u0_ref, u1_ref, *, steps, tiles, tiles_per_seq):
    s = pl.program_id(0)
    i = s % tiles
    tm = h_ref.shape[0]
    u_refs = (u0_ref, u1_ref)

    def activate(slot, r1):
        u_ref = u_refs[slot]
        groups = FFN_ROWS // SUBLANE

        def conv(cols):
            window = u_ref[r1:r1 + FFN_ROWS + SUBLANE, cols]
            out = None
            for tap in range(CONV_WIDTH):
                back = CONV_WIDTH - 1 - tap
                shifted = window if back == 0 else pltpu.roll(window, back, 0)
                x = shifted[SUBLANE:].reshape(groups, SUBLANE, FF_CHUNK)
                term = cw_ref[tap * SUBLANE:(tap + 1) * SUBLANE, cols][None] * x
                out = term if out is None else out + term
            out = out + cw_ref[CONV_WIDTH * SUBLANE:(CONV_WIDTH + 1) * SUBLANE, cols][None]
            return out.reshape(FFN_ROWS, FF_CHUNK)

        up = conv(slice(0, FF_CHUNK))
        gate = conv(slice(FF_CHUNK, 2 * FF_CHUNK))
        a_ref[r1:r1 + FFN_ROWS, :] = (jax.nn.silu(gate) * up).astype(BF)

    def run(project_slot=None, activate_slot=None, first_tile=False):
        if first_tile:
            u_refs[project_slot][:SUBLANE] = jnp.zeros((SUBLANE, 2 * FF_CHUNK), F32)
        elif project_slot is not None:
            seq_start = (i % tiles_per_seq) == 0
            halo = u_refs[1 - project_slot][tm:]
            u_refs[project_slot][:SUBLANE] = jnp.where(seq_start, 0.0, halo)
        kslices = D_MODEL // FFN_KSLICE
        per_slice = FFN_PIECE // FFN_ROWS // kslices
        for r0 in range(0, tm, FFN_PIECE):
            acc = None
            for ks in range(kslices):
                if project_slot is not None:
                    kk = slice(ks * FFN_KSLICE, (ks + 1) * FFN_KSLICE)
                    part = _dot(h_ref[r0:r0 + FFN_PIECE, kk], w_ref[kk, :])
                    acc = part if acc is None else acc + part
                if activate_slot is not None:
                    for k in range(per_slice):
                        activate(activate_slot, r0 + (ks * per_slice + k) * FFN_ROWS)
            if project_slot is not None:
                u_refs[project_slot][SUBLANE + r0:SUBLANE + r0 + FFN_PIECE, :] = acc

    @pl.when(s == 0)
    def _():
        run(project_slot=0, first_tile=True)

    for slot in range(2):
        @pl.when((s > 0) & (s < steps) & (s % 2 == slot))
        def _(slot=slot):
            run(project_slot=slot, activate_slot=1 - slot)

    @pl.when(s == steps)
    def _():
        run(activate_slot=(steps - 1) % 2)


def _ffn_down_kernel(a_ref, wd_ref, x_ref, fg_ref, y_ref):
    y_ref[...] = _rms(x_ref[...] + _dot(a_ref[...], wd_ref[...]), fg_ref[...])


def _ffn_up(h2, wu, cw, seq):
    t = h2.shape[0]
    tm = FFN_TM
    tiles = t // tm
    steps = FF_CHUNKS * tiles
    projected = lambda s: jnp.minimum(s, steps - 1)
    activated = lambda s: jnp.maximum(s - 1, 0)
    return pl.pallas_call(
        functools.partial(_ffn_up_kernel, steps=steps, tiles=tiles, tiles_per_seq=seq // tm),
        grid=(steps + 1,),
        in_specs=[
            pl.BlockSpec((tm, D_MODEL), lambda s: (projected(s) % tiles, 0)),
            pl.BlockSpec((D_MODEL, 2 * FF_CHUNK), lambda s: (0, projected(s) // tiles)),
            pl.BlockSpec(((CONV_WIDTH + 1) * SUBLANE, 2 * FF_CHUNK), lambda s: (0, activated(s) // tiles)),
        ],
        out_specs=pl.BlockSpec((tm, FF_CHUNK), lambda s: (activated(s) % tiles, activated(s) // tiles)),
        out_shape=jax.ShapeDtypeStruct((t, D_FF_PAD), BF),
        scratch_shapes=[
            pltpu.VMEM((tm + SUBLANE, 2 * FF_CHUNK), F32),
            pltpu.VMEM((tm + SUBLANE, 2 * FF_CHUNK), F32),
        ],
        compiler_params=_params("arbitrary"),
        name="ffn_up",
    )(h2, wu, cw)


def _ffn_down(act, wd, x1, fg):
    t = x1.shape[0]
    tm = FFN_DOWN_TM
    return pl.pallas_call(
        _ffn_down_kernel,
        grid=(t // tm,),
        in_specs=[
            pl.BlockSpec((tm, D_FF_PAD), lambda i: (i, 0)),
            pl.BlockSpec((D_FF_PAD, D_MODEL), lambda i: (0, 0), pipeline_mode=pl.Buffered(1)),
            pl.BlockSpec((tm, D_MODEL), lambda i: (i, 0)),
            pl.BlockSpec((1, D_MODEL), lambda i: (0, 0)),
        ],
        out_specs=pl.BlockSpec((tm, D_MODEL), lambda i: (i, 0)),
        out_shape=jax.ShapeDtypeStruct((t, D_MODEL), F32),
        compiler_params=_params("arbitrary"),
        name="ffn_down",
    )(act, wd, x1, fg)


def _rotate_half_cols(w):
    half = w.shape[-1] // 2
    return jnp.concatenate([-w[..., half:], w[..., :half]], axis=-1)


PREP_PIECE = 512
PREP_ROWS = 256


def _relayout_kernel(src_ref, *out_refs, plans):
    ncols = src_ref.shape[1]
    for out_ref, plan in zip(out_refs, plans):
        col = 0
        for start, width in plan:
            for off in range(0, width, PREP_PIECE):
                w = min(PREP_PIECE, width - off)
                dst = slice(col + off, col + off + w)
                if start is None:
                    out_ref[:, dst] = jnp.zeros((out_ref.shape[0], w), out_ref.dtype)
                    continue
                lead = (start + off) % LANE
                lo = start + off - lead
                hi = min(-(-(start + off + w) // LANE) * LANE, ncols)
                out_ref[:, dst] = src_ref[:, lo:hi][:, lead:lead + w].astype(out_ref.dtype)
            col += width


def _relayout(src, plans, dtype, row_tile, name):
    rows, ncols = src.shape
    widths = [sum(w for _, w in plan) for plan in plans]
    assert rows % row_tile == 0 and all(w % LANE == 0 for plan in plans for _, w in plan)
    outs = pl.pallas_call(
        functools.partial(_relayout_kernel, plans=plans),
        grid=(rows // row_tile,),
        in_specs=[pl.BlockSpec((row_tile, ncols), lambda i: (i, 0))],
        out_specs=[pl.BlockSpec((row_tile, w), lambda i: (i, 0)) for w in widths],
        out_shape=[jax.ShapeDtypeStruct((rows, w), dtype) for w in widths],
        compiler_params=_params("arbitrary"),
        name=name,
    )(src)
    return outs


def _transpose_rows_kernel(src_hbm, out_ref, buf_ref, sem, *, row_start, chunks):
    j = pl.program_id(0)

    def fetch(jj, slot):
        rows = pl.ds(pl.multiple_of(row_start(jj), SUBLANE), PREP_PIECE)
        return pltpu.make_async_copy(src_hbm.at[rows, :], buf_ref.at[slot], sem.at[slot])

    @pl.when(j == 0)
    def _():
        fetch(j, 0).start()

    @pl.when(j + 1 < chunks)
    def _():
        fetch(j + 1, (j + 1) % 2).start()

    fetch(j, j % 2).wait()
    out_ref[...] = buf_ref[j % 2].T.astype(out_ref.dtype)


def _transpose_rows(src, row_start, chunks, name):
    ncols = src.shape[1]
    return pl.pallas_call(
        functools.partial(_transpose_rows_kernel, row_start=row_start, chunks=chunks),
        grid=(chunks,),
        in_specs=[pl.BlockSpec(memory_space=pl.ANY)],
        out_specs=pl.BlockSpec((ncols, PREP_PIECE), lambda j: (0, j)),
        out_shape=jax.ShapeDtypeStruct((ncols, chunks * PREP_PIECE), BF),
        scratch_shapes=[pltpu.VMEM((2, PREP_PIECE, ncols), F32), pltpu.SemaphoreType.DMA((2,))],
        compiler_params=_params("arbitrary"),
        name=name,
    )(src)


def _ff_chunk_plan():
    plan = []
    for c in range(FF_CHUNKS):
        lo, hi = c * FF_CHUNK, min((c + 1) * FF_CHUNK, D_FF)
        for half in (0, D_FF):
            plan.append((half + lo, hi - lo))
            if hi - lo < FF_CHUNK:
                plan.append((None, FF_CHUNK - (hi - lo)))
    return plan


def kernel(x, attn_norm_g, w_in, b_gate, q_norm_g, w_uq, kv_norm_g, w_ukv, w_o_mla, w_o_dil,
           w_out, ffn_norm_g, w_up, conv_w, conv_b, w_down, final_norm_g):
    batch, seq, _ = x.shape
    assert w_in.shape[0] == 1, "single-layer block"
    assert seq % TM == 0 and seq % FFN_TM == 0 and seq % TQ == 0 and TQ == TK and seq % (16 * DIL_BLOCK) == 0
    t = batch * seq
    x2 = x.reshape(t, D_MODEL)

    dqkv = DIL_HEADS * DIL_HEAD_DIM
    o = np.cumsum((0, Q_LORA_RANK, KV_LORA_RANK, QK_ROPE_DIM, dqkv, dqkv, dqkv, D_MODEL, D_MODEL))
    wt = w_in[0].T
    w_head = _transpose_rows(wt, lambda j: j * PREP_PIECE, -(-int(o[3]) // PREP_PIECE), "relayout_w_lat")
    wlat = jnp.concatenate([w_head[:, o[0]:o[3]], _rotate_half_cols(w_head[:, o[2]:o[3]])], axis=1)
    gw = DIL_HEADS_PER_GROUP * DIL_HEAD_DIM
    assert gw == PREP_PIECE and D_MODEL % PREP_PIECE == 0
    dil_start = lambda j: int(o[3]) + (j % 3) * (DIL_HEADS * DIL_HEAD_DIM) + (j // 3) * gw
    wdil = _transpose_rows(wt, dil_start, 3 * DIL_GROUPS, "relayout_w_dil")
    wgate = _transpose_rows(wt, lambda j: int(o[6]) + j * PREP_PIECE, 2 * D_MODEL // PREP_PIECE, "relayout_w_gate")
    wq = w_uq[0].astype(BF).reshape(Q_LORA_RANK, MLA_HEADS, QK_NOPE_DIM + QK_ROPE_DIM)
    wq_pe = wq[..., QK_NOPE_DIM:]
    wq = jnp.concatenate([wq[..., :QK_NOPE_DIM], wq_pe, _rotate_half_cols(wq_pe)], axis=-1)
    wq = wq.reshape(Q_LORA_RANK, MLA_HEADS * QK_SLOT)
    wkv = w_ukv[0].astype(BF)
    (wu,) = _relayout(w_up[0], [_ff_chunk_plan()], BF, PREP_ROWS, "relayout_w_up")
    cw = jnp.repeat(jnp.concatenate([conv_w[0], conv_b], axis=0), SUBLANE, axis=0)
    (cw,) = _relayout(cw, [_ff_chunk_plan()], F32, cw.shape[0], "relayout_conv")
    wd =jnp.concatenate([w_down[0].astype(BF), jnp.zeros((D_FF_PAD - D_FF, D_MODEL), BF)], axis=0)

    pos = jnp.arange(seq, dtype=F32)
    inv_freq = ROPE_THETA ** (-jnp.arange(0, QK_ROPE_DIM, 2, dtype=F32) / QK_ROPE_DIM)
    ang = pos[:, None] * inv_freq[None, :]
    cs = jnp.concatenate([jnp.cos(ang), jnp.cos(ang), jnp.sin(ang), jnp.sin(ang)], axis=1)
    slopes = 2.0 ** (-ALIBI_MAX_BIAS * jnp.arange(1, DIL_HEADS + 1, dtype=F32) / DIL_HEADS)

    q, k, v = _mla_prep(x2, attn_norm_g, wlat, q_norm_g, kv_norm_g, wq, wkv, cs, seq)
    o_mla = _mla_flash(q, k, v, batch, seq)
    qkv_dil = _dil_proj(x2, attn_norm_g, wdil, batch, seq)
    o_dil, lse_dil = [], []
    for gi, (_, dil) in enumerate(DIL_PATTERNS):
        o_g, lse_g = _dilated(qkv_dil[gi], slopes, gi, batch, seq)
        nb = seq // dil // DIL_BLOCK
        o_dil.append(o_g)
        lse_g = lse_g.reshape(DIL_HEADS_PER_GROUP, batch, dil, nb, DIL_BLOCK)
        lse_dil.append(lse_g.transpose(1, 3, 4, 2, 0).reshape(t, DIL_HEADS_PER_GROUP))
    lse = jnp.concatenate(lse_dil, axis=1)
    x1, h2 = _merge(x2, attn_norm_g, wgate, b_gate, o_mla, o_dil, lse,
                    w_o_mla[0].astype(BF), w_o_dil[0].astype(BF), w_out[0].astype(BF), ffn_norm_g, seq)

    act = _ffn_up(h2, wu, cw, seq)
    y = _ffn_down(act, wd, x1, final_norm_g.reshape(1, D_MODEL))
    return y.reshape(batch, seq, D_MODEL)
```

```python
import functools
import math

import numpy as np
import jax
import jax.numpy as jnp
from jax import lax
from jax.experimental import pallas as pl
from jax.experimental.pallas import tpu as pltpu

D_MODEL = 2048
MLA_HEADS = 8
QK_NOPE_DIM = 128
QK_ROPE_DIM = 64
V_HEAD_DIM = 128
Q_LORA_RANK = 512
KV_LORA_RANK = 256
ROPE_THETA = 10000.0
DIL_PATTERNS = ((128, 1), (512, 4), (2048, 16))
DIL_GROUPS = 3
DIL_HEADS_PER_GROUP = 4
DIL_HEADS = DIL_GROUPS * DIL_HEADS_PER_GROUP
DIL_HEAD_DIM = 128
DIL_BLOCK = 128
ALIBI_MAX_BIAS = 8.0
D_FF = 5504
CONV_WIDTH = 3
NORM_EPS = 1e-6

LANE = 128
SUBLANE = 8
VMEM_LIMIT = 56 * 1024 * 1024

LAT_DIM = Q_LORA_RANK + KV_LORA_RANK + 2 * QK_ROPE_DIM
QK_SLOT = 2 * LANE
V_SLOT = 2 * LANE
DIL_GROUP_DIM = 3 * DIL_HEADS_PER_GROUP * DIL_HEAD_DIM
FF_CHUNK = 512
FFN_TM = 2048
FFN_DOWN_TM = 512
FFN_PIECE = 256
FFN_KSLICE = 256
FFN_ROWS = 32
D_FF_PAD = -(-D_FF // FF_CHUNK) * FF_CHUNK
FF_CHUNKS = D_FF_PAD // FF_CHUNK
NEG = -1e30

TM = 512
TQ = 512
TK = 512
FLASH_HEADS = 4
DIL_BATCH = 8

BF = jnp.bfloat16
F32 = jnp.float32


def _params(*sem):
    return pltpu.CompilerParams(dimension_semantics=sem, vmem_limit_bytes=VMEM_LIMIT)


def _rms(x, g):
    return x * lax.rsqrt(jnp.mean(x * x, axis=-1, keepdims=True) + NORM_EPS) * g


def _dot(a, b):
    return jnp.dot(a, b, preferred_element_type=F32)


def _dot_nt(a, b):
    return lax.dot_general(a, b, (((1,), (1,)), ((), ())), preferred_element_type=F32)


def _rope_pair(t):
    lane = lax.broadcasted_iota(jnp.int32, t.shape, 1)
    return jnp.where(lane < QK_ROPE_DIM, t + pltpu.roll(t, QK_ROPE_DIM, 1), 0.0)


def _mla_prep_kernel(x_ref, g_ref, wlat_ref, qg_ref, kvg_ref, wq_ref, wkv_ref, cs_ref,
                     q_ref, k_ref, v_ref):
    h = _rms(x_ref[...], g_ref[...]).astype(BF)
    lat = _dot(h, wlat_ref[...])
    cq = _rms(lat[:, :Q_LORA_RANK], qg_ref[...]).astype(BF)
    ckv = _rms(lat[:, Q_LORA_RANK:Q_LORA_RANK + KV_LORA_RANK], kvg_ref[...]).astype(BF)
    q = _dot(cq, wq_ref[...])
    kv = _dot(ckv, wkv_ref[...])
    cs = cs_ref[...]
    scale = (QK_NOPE_DIM + QK_ROPE_DIM) ** -0.5 * math.log2(math.e)
    k_rope = _rope_pair(lat[:, Q_LORA_RANK + KV_LORA_RANK:] * cs).astype(BF)
    ones = jnp.ones((x_ref.shape[0], V_SLOT - V_HEAD_DIM), BF)
    for hd in range(MLA_HEADS):
        qs = q[:, hd * QK_SLOT:(hd + 1) * QK_SLOT]
        q_ref[hd, :, :LANE] = (qs[:, :LANE] * scale).astype(BF)
        q_ref[hd, :, LANE:] = _rope_pair(qs[:, LANE:] * (cs * scale)).astype(BF)
        k_ref[hd, :, :LANE] = kv[:, hd * 256:hd * 256 + LANE].astype(BF)
        k_ref[hd, :, LANE:] = k_rope
        v_ref[hd, :, :V_HEAD_DIM] = kv[:, hd * 256 + LANE:(hd + 1) * 256].astype(BF)
        v_ref[hd, :, V_HEAD_DIM:] = ones


def _mla_prep(x2, g, wlat, qg, kvg, wq, wkv, cs, seq):
    t = x2.shape[0]
    nseq = seq // TM
    const = lambda i: (0, 0)
    return pl.pallas_call(
        _mla_prep_kernel,
        grid=(t // TM,),
        in_specs=[
            pl.BlockSpec((TM, D_MODEL), lambda i: (i, 0)),
            pl.BlockSpec((1, D_MODEL), const),
            pl.BlockSpec((D_MODEL, LAT_DIM), const),
            pl.BlockSpec((1, Q_LORA_RANK), const),
            pl.BlockSpec((1, KV_LORA_RANK), const),
            pl.BlockSpec((Q_LORA_RANK, MLA_HEADS * QK_SLOT), const),
            pl.BlockSpec((KV_LORA_RANK, MLA_HEADS * 256), const),
            pl.BlockSpec((TM, LANE), lambda i: (i % nseq, 0)),
        ],
        out_specs=[
            pl.BlockSpec((MLA_HEADS, TM, QK_SLOT), lambda i: (0, i, 0)),
            pl.BlockSpec((MLA_HEADS, TM, QK_SLOT), lambda i: (0, i, 0)),
            pl.BlockSpec((MLA_HEADS, TM, V_SLOT), lambda i: (0, i, 0)),
        ],
        out_shape=[
            jax.ShapeDtypeStruct((MLA_HEADS, t, QK_SLOT), BF),
            jax.ShapeDtypeStruct((MLA_HEADS, t, QK_SLOT), BF),
            jax.ShapeDtypeStruct((MLA_HEADS, t, V_SLOT), BF),
        ],
        compiler_params=_params("arbitrary"),
        name="mla_prep",
    )(x2, g, wlat, qg, kvg, wq, wkv, cs)


def _dil_proj_kernel(x_ref, g_ref, w_ref, *rest):
    o_refs, h_ref, res_ref = rest[:DIL_GROUPS], rest[DIL_GROUPS], rest[DIL_GROUPS + 1]
    nqkv = 3 * DIL_HEADS_PER_GROUP
    h_ref[...] = _rms(x_ref[...], g_ref[...]).astype(BF)
    for gi, (_, dil) in enumerate(DIL_PATTERNS):
        sub = TM // dil
        res = _dot(h_ref[...], w_ref[:, gi * DIL_GROUP_DIM:(gi + 1) * DIL_GROUP_DIM])
        for hh in range(nqkv):
            blk = res[:, hh * DIL_HEAD_DIM:(hh + 1) * DIL_HEAD_DIM]
            if hh < DIL_HEADS_PER_GROUP:
                blk = blk * DIL_HEAD_DIM ** -0.5
            if dil == 1:
                o_refs[gi][hh, 0, 0] = blk.astype(BF)
            else:
                res_ref[hh] = blk
        if dil > 1:
            for hh in range(nqkv):
                for r in range(dil):
                    o_refs[gi][hh, 0, r] = res_ref[hh, pl.ds(r, sub, stride=dil), :].astype(BF)


def _dil_proj(x2, g, w, batch, seq):
    nseq = seq // TM
    nqkv = 3 * DIL_HEADS_PER_GROUP
    const = lambda i: (0, 0)
    return pl.pallas_call(
        _dil_proj_kernel,
        grid=(batch * nseq,),
        in_specs=[
            pl.BlockSpec((TM, D_MODEL), lambda i: (i, 0)),
            pl.BlockSpec((1, D_MODEL), const),
            pl.BlockSpec((D_MODEL, DIL_GROUPS * DIL_GROUP_DIM), const, pipeline_mode=pl.Buffered(1)),
        ],
        out_specs=[
            pl.BlockSpec((nqkv, 1, dil, TM // dil, DIL_HEAD_DIM), lambda i: (0, i // nseq, 0, i % nseq, 0))
            for _, dil in DIL_PATTERNS
        ],
        out_shape=[
            jax.ShapeDtypeStruct((nqkv, batch, dil, seq // dil, DIL_HEAD_DIM), BF)
            for _, dil in DIL_PATTERNS
        ],
        scratch_shapes=[pltpu.VMEM((TM, D_MODEL), BF), pltpu.VMEM((nqkv, TM, DIL_HEAD_DIM), F32)],
        compiler_params=_params("arbitrary"),
        name="dil_proj",
    )(x2, g, w)


def _flash_kernel(q_ref, k_ref, v_ref, o_ref, *scratch):
    nh = FLASH_HEADS
    m_refs, acc_refs = scratch[:nh], scratch[nh:2 * nh]
    s_refs = (scratch[2 * nh:3 * nh], scratch[3 * nh:4 * nh])
    qi = pl.program_id(2)
    for hh in range(nh):
        m_refs[hh][...] = jnp.full(m_refs[hh].shape, NEG, F32)
        acc_refs[hh][...] = jnp.zeros(acc_refs[hh].shape, F32)

    def key_rows(j):
        return pl.ds(pl.multiple_of(j * TK, TK), TK)

    def scores(j, slot):
        for hh in range(nh):
            s_refs[slot][hh][...] = _dot_nt(q_ref[hh], k_ref[hh, key_rows(j), :])

    def mask_diagonal(slot):
        keep = lax.broadcasted_iota(jnp.int32, (TQ, TK), 1) <= lax.broadcasted_iota(jnp.int32, (TQ, TK), 0)
        for hh in range(nh):
            s_refs[slot][hh][...] = jnp.where(keep, s_refs[slot][hh][...], NEG)

    def update(j, slot):
        for hh in range(nh):
            s = s_refs[slot][hh][...]
            m_old = m_refs[hh][...]
            m_new = jnp.maximum(m_old, jnp.max(s, axis=-1, keepdims=True))
            p = jnp.exp2(s - m_new)
            pv = _dot(p.astype(BF), v_ref[hh, key_rows(j), :])
            acc_refs[hh][...] = jnp.exp2(m_old - m_new) * acc_refs[hh][...] + pv
            m_refs[hh][...] = m_new

    scores(0, 0)

    @pl.when(qi == 0)
    def _():
        mask_diagonal(0)

    def body(j, carry):
        for slot in range(2):
            @pl.when(j % 2 == slot)
            def _(slot=slot):
                scores(j, slot)
                update(j - 1, 1 - slot)

            @pl.when((j % 2 == slot) & (j == qi))
            def _(slot=slot):
                mask_diagonal(slot)
        return carry

    lax.fori_loop(1, qi + 1, body, 0)

    for slot in range(2):
        @pl.when(qi % 2 == slot)
        def _(slot=slot):
            update(qi, slot)

    for hh in range(FLASH_HEADS):
        acc = acc_refs[hh][...]
        o_ref[hh] = (acc[:, :V_HEAD_DIM] * (1.0 / acc[:, V_HEAD_DIM:])).astype(BF)


def _mla_flash(q, k, v, batch, seq):
    nq = seq // TQ
    t = batch * seq
    hb = FLASH_HEADS
    return pl.pallas_call(
        _flash_kernel,
        grid=(batch, MLA_HEADS // hb, nq),
        in_specs=[
            pl.BlockSpec((hb, TQ, QK_SLOT), lambda b, h, i: (h, b * nq + i, 0)),
            pl.BlockSpec((hb, seq, QK_SLOT), lambda b, h, i: (h, b, 0)),
            pl.BlockSpec((hb, seq, V_SLOT), lambda b, h, i: (h, b, 0)),
        ],
        out_specs=pl.BlockSpec((hb, TQ, V_HEAD_DIM), lambda b, h, i: (h, b * nq + i, 0)),
        out_shape=jax.ShapeDtypeStruct((MLA_HEADS, t, V_HEAD_DIM), BF),
        scratch_shapes=([pltpu.VMEM((TQ, 1), F32)] * hb + [pltpu.VMEM((TQ, V_SLOT), F32)] * hb
                        + [pltpu.VMEM((TQ, TK), F32)] * (2 * hb)),
        compiler_params=_params("arbitrary", "arbitrary", "arbitrary"),
        name="mla_flash",
    )(q, k, v)


def _dilated_kernel(slope_ref, q_ref, k_ref, v_ref, o_ref, lse_ref, *, group, dil, nb):
    hs = pl.program_id(1)
    slope = slope_ref[group * DIL_HEADS_PER_GROUP + hs] * float(dil)
    qi = lax.broadcasted_iota(jnp.int32, (DIL_BLOCK, DIL_BLOCK), 0)
    ki = lax.broadcasted_iota(jnp.int32, (DIL_BLOCK, DIL_BLOCK), 1)
    dist_cur = (qi - ki).astype(F32)
    dist_prev = dist_cur + float(DIL_BLOCK)
    w_sub = float(DIL_PATTERNS[group][0] // dil)
    bias_cur = jnp.where((dist_cur >= 0.0) & (dist_cur <= w_sub), -slope * dist_cur, NEG)
    bias_prev = jnp.where(dist_prev <= w_sub, -slope * dist_prev, NEG)
    neg = jnp.full((DIL_BLOCK, DIL_BLOCK), NEG, F32)
    bias_inner = jnp.concatenate([bias_prev, bias_cur], axis=1)
    bias_first = jnp.concatenate([bias_cur, neg], axis=1)
    eye = qi == ki

    def window(n):
        start = max(n - 1, 0) * DIL_BLOCK
        return slice(start, start + 2 * DIL_BLOCK)

    blocks = [(r, n) for r in range(dil) for n in range(nb)]
    for g0 in range(0, len(blocks), DIL_BATCH):
        batch = blocks[g0:g0 + DIL_BATCH]
        scores = [_dot_nt(q_ref[0, 0, r, n * DIL_BLOCK:(n + 1) * DIL_BLOCK, :], k_ref[0, 0, r, window(n), :])
                  + (bias_first if n == 0 else bias_inner) for r, n in batch]
        maxes = [jnp.max(s, axis=-1, keepdims=True) for s in scores]
        probs = [jnp.exp(s - m) for s, m in zip(scores, maxes)]
        sums = [jnp.sum(p, axis=-1, keepdims=True) for p in probs]
        outs = [_dot(p.astype(BF), v_ref[0, 0, r, window(n), :]) for p, (r, n) in zip(probs, batch)]
        for (r, n), o, l in zip(batch, outs, sums):
            o_ref[0, 0, r, n * DIL_BLOCK:(n + 1) * DIL_BLOCK, :] = (o * (1.0 / l)).astype(BF)
        rows = [jnp.sum(jnp.where(eye, m + jnp.log(l), 0.0), axis=0, keepdims=True) for m, l in zip(maxes, sums)]
        lse_ref[0, 0, g0:g0 + DIL_BATCH, :] = jnp.concatenate(rows, axis=0)


def _dilated(qkv, slopes, group, batch, seq):
    _, dil = DIL_PATTERNS[group]
    sub = seq // dil
    nb = sub // DIL_BLOCK
    hg = DIL_HEADS_PER_GROUP
    blk = (1, 1, dil, sub, DIL_HEAD_DIM)

    def head(which):
        return lambda b, h: (which * hg + h, b, 0, 0, 0)

    return pl.pallas_call(
        functools.partial(_dilated_kernel, group=group, dil=dil, nb=nb),
        grid=(batch, hg),
        in_specs=[
            pl.BlockSpec(memory_space=pltpu.SMEM),
            pl.BlockSpec(blk, head(0)),
            pl.BlockSpec(blk, head(1)),
            pl.BlockSpec(blk, head(2)),
        ],
        out_specs=[
            pl.BlockSpec(blk, head(0)),
            pl.BlockSpec((1, 1, dil * nb, DIL_BLOCK), lambda b, h: (h, b, 0, 0)),
        ],
        out_shape=[
            jax.ShapeDtypeStruct((hg, batch, dil, sub, DIL_HEAD_DIM), BF),
            jax.ShapeDtypeStruct((hg, batch, dil * nb, DIL_BLOCK), F32),
        ],
        compiler_params=_params("arbitrary", "arbitrary"),
        name=f"dilated_d{dil}",
    )(slopes, qkv, qkv, qkv)


MERGE_TN = 512
MERGE_TM = 256


def _merge_kernel(x_ref, g_ref, wg_ref, bg_ref, oa_ref, o1_ref, o4_ref, o16_ref, lse_ref,
                  woa_ref, wob_ref, wout_ref, fg_ref, y_ref, h2_ref, h_ref, a_ref, b_ref, mg_ref, nat_ref):
    x = x_ref[...]
    tm = x.shape[0]
    h_ref[...] = _rms(x, g_ref[...]).astype(BF)
    for hd in range(MLA_HEADS):
        a_ref[:, hd * V_HEAD_DIM:(hd + 1) * V_HEAD_DIM] = oa_ref[hd]
    lse = lse_ref[...]
    o_refs = (o1_ref, o4_ref, o16_ref)
    for hs in range(DIL_HEADS_PER_GROUP):
        ls = [lse[:, gi * DIL_HEADS_PER_GROUP + hs:gi * DIL_HEADS_PER_GROUP + hs + 1] for gi in range(DIL_GROUPS)]
        mx = jnp.maximum(jnp.maximum(ls[0], ls[1]), ls[2])
        es = [jnp.exp(v - mx) for v in ls]
        inv = 1.0 / (es[0] + es[1] + es[2])
        comb = (es[0] * inv) * o_refs[0][hs, 0, 0].astype(F32)
        for gi in range(1, DIL_GROUPS):
            dil = DIL_PATTERNS[gi][1]
            for r in range(dil):
                nat_ref[pl.ds(r, tm // dil, stride=dil), :] = o_refs[gi][hs, 0, r].astype(F32)
            comb = comb + (es[gi] * inv) * nat_ref[...]
        b_ref[:, hs * DIL_HEAD_DIM:(hs + 1) * DIL_HEAD_DIM] = comb.astype(BF)
    for c in range(D_MODEL // MERGE_TN):
        cols = slice(c * MERGE_TN, (c + 1) * MERGE_TN)
        cols_b = slice(D_MODEL + c * MERGE_TN, D_MODEL + (c + 1) * MERGE_TN)
        gate_a = jax.nn.sigmoid(_dot(h_ref[...], wg_ref[:, cols]) + bg_ref[:, cols])
        gate_b = jax.nn.sigmoid(_dot(h_ref[...], wg_ref[:, cols_b]) + bg_ref[:, cols_b])
        o_a = _dot(a_ref[...], woa_ref[:, cols])
        o_b = _dot(b_ref[...], wob_ref[:, cols])
        mg_ref[:, cols] = (gate_a * o_a + gate_b * o_b).astype(BF)
    y = x + _dot(mg_ref[...], wout_ref[...])
    y_ref[...] = y
    h2_ref[...] = _rms(y, fg_ref[...]).astype(BF)


def _merge(x2, g, wg, bg, oa, o_dil, lse, woa, wob, wout, fg, seq):
    t = x2.shape[0]
    tm = MERGE_TM
    nseq = seq // tm
    const = lambda i: (0, 0)
    one = pl.Buffered(1)
    dil_specs = [
        pl.BlockSpec((DIL_HEADS_PER_GROUP, 1, dil, tm // dil, DIL_HEAD_DIM),
                     lambda i: (0, i // nseq, 0, i % nseq, 0))
        for _, dil in DIL_PATTERNS
    ]
    return pl.pallas_call(
        _merge_kernel,
        grid=(t // tm,),
        in_specs=[
            pl.BlockSpec((tm, D_MODEL), lambda i: (i, 0)),
            pl.BlockSpec((1, D_MODEL), const),
            pl.BlockSpec((D_MODEL, 2 * D_MODEL), const, pipeline_mode=one),
            pl.BlockSpec((1, 2 * D_MODEL), const),
            pl.BlockSpec((MLA_HEADS, tm, V_HEAD_DIM), lambda i: (0, i, 0)),
            *dil_specs,
            pl.BlockSpec((tm, DIL_HEADS), lambda i: (i, 0)),
            pl.BlockSpec((MLA_HEADS * V_HEAD_DIM, D_MODEL), const, pipeline_mode=one),
            pl.BlockSpec((DIL_HEADS_PER_GROUP * DIL_HEAD_DIM, D_MODEL), const, pipeline_mode=one),
            pl.BlockSpec((D_MODEL, D_MODEL), const, pipeline_mode=one),
            pl.BlockSpec((1, D_MODEL), const),
        ],
        out_specs=[pl.BlockSpec((tm, D_MODEL), lambda i: (i, 0)), pl.BlockSpec((tm, D_MODEL), lambda i: (i, 0))],
        out_shape=[jax.ShapeDtypeStruct((t, D_MODEL), F32), jax.ShapeDtypeStruct((t, D_MODEL), BF)],
        scratch_shapes=[
            pltpu.VMEM((tm, D_MODEL), BF),
            pltpu.VMEM((tm, MLA_HEADS * V_HEAD_DIM), BF),
            pltpu.VMEM((tm, DIL_HEADS_PER_GROUP * DIL_HEAD_DIM), BF),
            pltpu.VMEM((tm, D_MODEL), BF),
            pltpu.VMEM((tm, DIL_HEAD_DIM), F32),
        ],
        compiler_params=_params("arbitrary"),
        name="merge",
    )(x2, g, wg, bg, oa, *o_dil, lse, woa, wob, wout, fg)


def _ffn_up_kernel(h_ref, w_ref, cw_ref, a_ref, u0_ref, u1_ref, *, steps, tiles, tiles_per_seq):
    s = pl.program_id(0)
    i = s % tiles
    tm = h_ref.shape[0]
    u_refs = (u0_ref, u1_ref)

    def activate(slot, r1):
        u_ref = u_refs[slot]
        groups = FFN_ROWS // SUBLANE

        def conv(cols):
            window = u_ref[r1:r1 + FFN_ROWS + SUBLANE, cols]
            out = None
            for tap in range(CONV_WIDTH):
                back = CONV_WIDTH - 1 - tap
                shifted = window if back == 0 else pltpu.roll(window, back, 0)
                x = shifted[SUBLANE:].reshape(groups, SUBLANE, FF_CHUNK)
                term = cw_ref[tap * SUBLANE:(tap + 1) * SUBLANE, cols][None] * x
                out = term if out is None else out + term
            out = out + cw_ref[CONV_WIDTH * SUBLANE:(CONV_WIDTH + 1) * SUBLANE, cols][None]
            return out.reshape(FFN_ROWS, FF_CHUNK)

        up = conv(slice(0, FF_CHUNK))
        gate = conv(slice(FF_CHUNK, 2 * FF_CHUNK))
        a_ref[r1:r1 + FFN_ROWS, :] = (jax.nn.silu(gate) * up).astype(BF)

    def run(project_slot=None, activate_slot=None, first_tile=False):
        if first_tile:
            u_refs[project_slot][:SUBLANE] = jnp.zeros((SUBLANE, 2 * FF_CHUNK), F32)
        elif project_slot is not None:
            seq_start = (i % tiles_per_seq) == 0
            halo = u_refs[1 - project_slot][tm:]
            u_refs[project_slot][:SUBLANE] = jnp.where(seq_start, 0.0, halo)
        kslices = D_MODEL // FFN_KSLICE
        per_slice = FFN_PIECE // FFN_ROWS // kslices
        for r0 in range(0, tm, FFN_PIECE):
            acc = None
            for ks in range(kslices):
                if project_slot is not None:
                    kk = slice(ks * FFN_KSLICE, (ks + 1) * FFN_KSLICE)
                    part = _dot(h_ref[r0:r0 + FFN_PIECE, kk], w_ref[kk, :])
                    acc = part if acc is None else acc + part
                if activate_slot is not None:
                    for k in range(per_slice):
                        activate(activate_slot, r0 + (ks * per_slice + k) * FFN_ROWS)
            if project_slot is not None:
                u_refs[project_slot][SUBLANE + r0:SUBLANE + r0 + FFN_PIECE, :] = acc

    @pl.when(s == 0)
    def _():
        run(project_slot=0, first_tile=True)

    for slot in range(2):
        @pl.when((s > 0) & (s < steps) & (s % 2 == slot))
        def _(slot=slot):
            run(project_slot=slot, activate_slot=1 - slot)

    @pl.when(s == steps)
    def _():
        run(activate_slot=(steps - 1) % 2)


def _ffn_down_kernel(a_ref, wd_ref, x_ref, fg_ref, y_ref):
    y_ref[...] = _rms(x_ref[...] + _dot(a_ref[...], wd_ref[...]), fg_ref[...])


def _ffn_up(h2, wu, cw, seq):
    t = h2.shape[0]
    tm = FFN_TM
    tiles = t // tm
    steps = FF_CHUNKS * tiles
    projected = lambda s: jnp.minimum(s, steps - 1)
    activated = lambda s: jnp.maximum(s - 1, 0)
    return pl.pallas_call(
        functools.partial(_ffn_up_kernel, steps=steps, tiles=tiles, tiles_per_seq=seq // tm),
        grid=(steps + 1,),
        in_specs=[
            pl.BlockSpec((tm, D_MODEL), lambda s: (projected(s) % tiles, 0)),
            pl.BlockSpec((D_MODEL, 2 * FF_CHUNK), lambda s: (0, projected(s) // tiles)),
            pl.BlockSpec(((CONV_WIDTH + 1) * SUBLANE, 2 * FF_CHUNK), lambda s: (0, activated(s) // tiles)),
        ],
        out_specs=pl.BlockSpec((tm, FF_CHUNK), lambda s: (activated(s) % tiles, activated(s) // tiles)),
        out_shape=jax.ShapeDtypeStruct((t, D_FF_PAD), BF),
        scratch_shapes=[
            pltpu.VMEM((tm + SUBLANE, 2 * FF_CHUNK), F32),
            pltpu.VMEM((tm + SUBLANE, 2 * FF_CHUNK), F32),
        ],
        compiler_params=_params("arbitrary"),
        name="ffn_up",
    )(h2, wu, cw)


def _ffn_down(act, wd, x1, fg):
    t = x1.shape[0]
    tm = FFN_DOWN_TM
    return pl.pallas_call(
        _ffn_down_kernel,
        grid=(t // tm,),
        in_specs=[
            pl.BlockSpec((tm, D_FF_PAD), lambda i: (i, 0)),
            pl.BlockSpec((D_FF_PAD, D_MODEL), lambda i: (0, 0), pipeline_mode=pl.Buffered(1)),
            pl.BlockSpec((tm, D_MODEL), lambda i: (i, 0)),
            pl.BlockSpec((1, D_MODEL), lambda i: (0, 0)),
        ],
        out_specs=pl.BlockSpec((tm, D_MODEL), lambda i: (i, 0)),
        out_shape=jax.ShapeDtypeStruct((t, D_MODEL), F32),
        compiler_params=_params("arbitrary"),
        name="ffn_down",
    )(act, wd, x1, fg)


def _rotate_half_cols(w):
    half = w.shape[-1] // 2
    return jnp.concatenate([-w[..., half:], w[..., :half]], axis=-1)


PREP_PIECE = 512
PREP_ROWS = 256


def _relayout_kernel(src_ref, *out_refs, plans):
    ncols = src_ref.shape[1]
    for out_ref, plan in zip(out_refs, plans):
        col = 0
        for start, width in plan:
            for off in range(0, width, PREP_PIECE):
                w = min(PREP_PIECE, width - off)
                dst = slice(col + off, col + off + w)
                if start is None:
                    out_ref[:, dst] = jnp.zeros((out_ref.shape[0], w), out_ref.dtype)
                    continue
                lead = (start + off) % LANE
                lo = start + off - lead
                hi = min(-(-(start + off + w) // LANE) * LANE, ncols)
                out_ref[:, dst] = src_ref[:, lo:hi][:, lead:lead + w].astype(out_ref.dtype)
            col += width


def _relayout(src, plans, dtype, row_tile, name):
    rows, ncols = src.shape
    widths = [sum(w for _, w in plan) for plan in plans]
    assert rows % row_tile == 0 and all(w % LANE == 0 for plan in plans for _, w in plan)
    outs = pl.pallas_call(
        functools.partial(_relayout_kernel, plans=plans),
        grid=(rows // row_tile,),
        in_specs=[pl.BlockSpec((row_tile, ncols), lambda i: (i, 0))],
        out_specs=[pl.BlockSpec((row_tile, w), lambda i: (i, 0)) for w in widths],
        out_shape=[jax.ShapeDtypeStruct((rows, w), dtype) for w in widths],
        compiler_params=_params("arbitrary"),
        name=name,
    )(src)
    return outs


def _transpose_rows_kernel(src_hbm, out_ref, buf_ref, sem, *, row_start, chunks):
    j = pl.program_id(0)

    def fetch(jj, slot):
        rows = pl.ds(pl.multiple_of(row_start(jj), SUBLANE), PREP_PIECE)
        return pltpu.make_async_copy(src_hbm.at[rows, :], buf_ref.at[slot], sem.at[slot])

    @pl.when(j == 0)
    def _():
        fetch(j, 0).start()

    @pl.when(j + 1 < chunks)
    def _():
        fetch(j + 1, (j + 1) % 2).start()

    fetch(j, j % 2).wait()
    out_ref[...] = buf_ref[j % 2].T.astype(out_ref.dtype)


def _transpose_rows(src, row_start, chunks, name):
    ncols = src.shape[1]
    return pl.pallas_call(
        functools.partial(_transpose_rows_kernel, row_start=row_start, chunks=chunks),
        grid=(chunks,),
        in_specs=[pl.BlockSpec(memory_space=pl.ANY)],
        out_specs=pl.BlockSpec((ncols, PREP_PIECE), lambda j: (0, j)),
        out_shape=jax.ShapeDtypeStruct((ncols, chunks * PREP_PIECE), BF),
        scratch_shapes=[pltpu.VMEM((2, PREP_PIECE, ncols), F32), pltpu.SemaphoreType.DMA((2,))],
        compiler_params=_params("arbitrary"),
        name=name,
    )(src)


def _ff_chunk_plan():
    plan = []
    for c in range(FF_CHUNKS):
        lo, hi = c * FF_CHUNK, min((c + 1) * FF_CHUNK, D_FF)
        for half in (0, D_FF):
            plan.append((half + lo, hi - lo))
            if hi - lo < FF_CHUNK:
                plan.append((None, FF_CHUNK - (hi - lo)))
    return plan


def kernel(x, attn_norm_g, w_in, b_gate, q_norm_g, w_uq, kv_norm_g, w_ukv, w_o_mla, w_o_dil,
           w_out, ffn_norm_g, w_up, conv_w, conv_b, w_down, final_norm_g):
    batch, seq, _ = x.shape
    assert w_in.shape[0] == 1, "single-layer block"
    assert seq % TM == 0 and seq % FFN_TM == 0 and seq % TQ == 0 and TQ == TK and seq % (16 * DIL_BLOCK) == 0
    t = batch * seq
    x2 = x.reshape(t, D_MODEL)

    dqkv = DIL_HEADS * DIL_HEAD_DIM
    o = np.cumsum((0, Q_LORA_RANK, KV_LORA_RANK, QK_ROPE_DIM, dqkv, dqkv, dqkv, D_MODEL, D_MODEL))
    wt = w_in[0].T
    w_head = _transpose_rows(wt, lambda j: j * PREP_PIECE, -(-int(o[3]) // PREP_PIECE), "relayout_w_lat")
    wlat = jnp.concatenate([w_head[:, o[0]:o[3]], _rotate_half_cols(w_head[:, o[2]:o[3]])], axis=1)
    gw = DIL_HEADS_PER_GROUP * DIL_HEAD_DIM
    assert gw == PREP_PIECE and D_MODEL % PREP_PIECE == 0
    dil_start = lambda j: int(o[3]) + (j % 3) * (DIL_HEADS * DIL_HEAD_DIM) + (j // 3) * gw
    wdil = _transpose_rows(wt, dil_start, 3 * DIL_GROUPS, "relayout_w_dil")
    wgate = _transpose_rows(wt, lambda j: int(o[6]) + j * PREP_PIECE, 2 * D_MODEL // PREP_PIECE, "relayout_w_gate")
    wq = w_uq[0].astype(BF).reshape(Q_LORA_RANK, MLA_HEADS, QK_NOPE_DIM + QK_ROPE_DIM)
    wq_pe = wq[..., QK_NOPE_DIM:]
    wq = jnp.concatenate([wq[..., :QK_NOPE_DIM], wq_pe, _rotate_half_cols(wq_pe)], axis=-1)
    wq = wq.reshape(Q_LORA_RANK, MLA_HEADS * QK_SLOT)
    wkv = w_ukv[0].astype(BF)
    (wu,) = _relayout(w_up[0], [_ff_chunk_plan()], BF, PREP_ROWS, "relayout_w_up")
    cw = jnp.repeat(jnp.concatenate([conv_w[0], conv_b], axis=0), SUBLANE, axis=0)
    (cw,) = _relayout(cw, [_ff_chunk_plan()], F32, cw.shape[0], "relayout_conv")
    wd =jnp.concatenate([w_down[0].astype(BF), jnp.zeros((D_FF_PAD - D_FF, D_MODEL), BF)], axis=0)

    pos = jnp.arange(seq, dtype=F32)
    inv_freq = ROPE_THETA ** (-jnp.arange(0, QK_ROPE_DIM, 2, dtype=F32) / QK_ROPE_DIM)
    ang = pos[:, None] * inv_freq[None, :]
    cs = jnp.concatenate([jnp.cos(ang), jnp.cos(ang), jnp.sin(ang), jnp.sin(ang)], axis=1)
    slopes = 2.0 ** (-ALIBI_MAX_BIAS * jnp.arange(1, DIL_HEADS + 1, dtype=F32) / DIL_HEADS)

    q, k, v = _mla_prep(x2, attn_norm_g, wlat, q_norm_g, kv_norm_g, wq, wkv, cs, seq)
    o_mla = _mla_flash(q, k, v, batch, seq)
    qkv_dil = _dil_proj(x2, attn_norm_g, wdil, batch, seq)
    o_dil, lse_dil = [], []
    for gi, (_, dil) in enumerate(DIL_PATTERNS):
        o_g, lse_g = _dilated(qkv_dil[gi], slopes, gi, batch, seq)
        nb = seq // dil // DIL_BLOCK
        o_dil.append(o_g)
        lse_g = lse_g.reshape(DIL_HEADS_PER_GROUP, batch, dil, nb, DIL_BLOCK)
        lse_dil.append(lse_g.transpose(1, 3, 4, 2, 0).reshape(t, DIL_HEADS_PER_GROUP))
    lse = jnp.concatenate(lse_dil, axis=1)
    x1, h2 = _merge(x2, attn_norm_g, wgate, b_gate, o_mla, o_dil, lse,
                    w_o_mla[0].astype(BF), w_o_dil[0].astype(BF), w_out[0].astype(BF), ffn_norm_g, seq)

    act = _ffn_up(h2, wu, cw, seq)
    y = _ffn_down(act, wd, x1, final_norm_g.reshape(1, D_MODEL))
    return y.reshape(batch, seq, D_MODEL)
```

```python
import functools
import math

import numpy as np
import jax
import jax.numpy as jnp
from jax import lax
from jax.experimental import pallas as pl
from jax.experimental.pallas import tpu as pltpu

D_MODEL = 2048
MLA_HEADS = 8
QK_NOPE_DIM = 128
QK_ROPE_DIM = 64
V_HEAD_DIM = 128
Q_LORA_RANK = 512
KV_LORA_RANK = 256
ROPE_THETA = 10000.0
DIL_PATTERNS = ((128, 1), (512, 4), (2048, 16))
DIL_GROUPS = 3
DIL_HEADS_PER_GROUP = 4
DIL_HEADS = DIL_GROUPS * DIL_HEADS_PER_GROUP
DIL_HEAD_DIM = 128
DIL_BLOCK = 128
ALIBI_MAX_BIAS = 8.0
D_FF = 5504
CONV_WIDTH = 3
NORM_EPS = 1e-6

LANE = 128
SUBLANE = 8
VMEM_LIMIT = 56 * 1024 * 1024

LAT_DIM = Q_LORA_RANK + KV_LORA_RANK + 2 * QK_ROPE_DIM
QK_SLOT = 2 * LANE
V_SLOT = 2 * LANE
KV_SLOT = QK_NOPE_DIM + V_HEAD_DIM
DIL_GROUP_DIM = 3 * DIL_HEADS_PER_GROUP * DIL_HEAD_DIM
FF_CHUNK = 512
FFN_TM = 2048
FFN_DOWN_TM = 512
FFN_PIECE = 256
FFN_KSLICE = 256
FFN_ROWS = 32
D_FF_PAD = -(-D_FF // FF_CHUNK) * FF_CHUNK
FF_CHUNKS = D_FF_PAD // FF_CHUNK
NEG = -1e30

TM = 512
TQ = 512
TK = 512
FLASH_HEADS = 4
DIL_BATCH = 8

BF = jnp.bfloat16
F32 = jnp.float32


def _params(*sem):
    return pltpu.CompilerParams(dimension_semantics=sem, vmem_limit_bytes=VMEM_LIMIT)


def _rms(x, g):
    return x * lax.rsqrt(jnp.mean(x * x, axis=-1, keepdims=True) + NORM_EPS) * g


def _dot(a, b):
    return jnp.dot(a, b, preferred_element_type=F32)


def _dot_nt(a, b):
    return lax.dot_general(a, b, (((1,), (1,)), ((), ())), preferred_element_type=F32)


def _rope_pair(t):
    lane = lax.broadcasted_iota(jnp.int32, t.shape, 1)
    return jnp.where(lane < QK_ROPE_DIM, t + pltpu.roll(t, QK_ROPE_DIM, 1), 0.0)


def _mla_prep_kernel(x_ref, g_ref, wlat_ref, qg_ref, kvg_ref, wq_ref, wkv_ref, cs_ref,
                     q_ref, k_ref, v_ref):
    h = _rms(x_ref[...], g_ref[...]).astype(BF)
    lat = _dot(h, wlat_ref[...])
    cq = _rms(lat[:, :Q_LORA_RANK], qg_ref[...]).astype(BF)
    ckv = _rms(lat[:, Q_LORA_RANK:Q_LORA_RANK + KV_LORA_RANK], kvg_ref[...]).astype(BF)
    q = _dot(cq, wq_ref[...])
    kv = _dot(ckv, wkv_ref[...])
    cs = cs_ref[...]
    scale = (QK_NOPE_DIM + QK_ROPE_DIM) ** -0.5 * math.log2(math.e)
    k_rope = _rope_pair(lat[:, Q_LORA_RANK + KV_LORA_RANK:] * cs).astype(BF)
    ones = jnp.ones((x_ref.shape[0], V_SLOT - V_HEAD_DIM), BF)
    for hd in range(MLA_HEADS):
        qs = q[:, hd * QK_SLOT:(hd + 1) * QK_SLOT]
        q_ref[hd, :, :LANE] = (qs[:, :LANE] * scale).astype(BF)
        q_ref[hd, :, LANE:] = _rope_pair(qs[:, LANE:] * (cs * scale)).astype(BF)
        k_ref[hd, :, :LANE] = kv[:, hd * KV_SLOT:hd * KV_SLOT + QK_NOPE_DIM].astype(BF)
        k_ref[hd, :, LANE:] = k_rope
        v_ref[hd, :, :V_HEAD_DIM] = kv[:, hd * KV_SLOT + QK_NOPE_DIM:(hd + 1) * KV_SLOT].astype(BF)
        v_ref[hd, :, V_HEAD_DIM:] = ones


def _mla_prep(x2, g, wlat, qg, kvg, wq, wkv, cs, seq):
    t = x2.shape[0]
    nseq = seq // TM
    const = lambda i: (0, 0)
    return pl.pallas_call(
        _mla_prep_kernel,
        grid=(t // TM,),
        in_specs=[
            pl.BlockSpec((TM, D_MODEL), lambda i: (i, 0)),
            pl.BlockSpec((1, D_MODEL), const),
            pl.BlockSpec((D_MODEL, LAT_DIM), const),
            pl.BlockSpec((1, Q_LORA_RANK), const),
            pl.BlockSpec((1, KV_LORA_RANK), const),
            pl.BlockSpec((Q_LORA_RANK, MLA_HEADS * QK_SLOT), const),
            pl.BlockSpec((KV_LORA_RANK, MLA_HEADS * KV_SLOT), const),
            pl.BlockSpec((TM, LANE), lambda i: (i % nseq, 0)),
        ],
        out_specs=[
            pl.BlockSpec((MLA_HEADS, TM, QK_SLOT), lambda i: (0, i, 0)),
            pl.BlockSpec((MLA_HEADS, TM, QK_SLOT), lambda i: (0, i, 0)),
            pl.BlockSpec((MLA_HEADS, TM, V_SLOT), lambda i: (0, i, 0)),
        ],
        out_shape=[
            jax.ShapeDtypeStruct((MLA_HEADS, t, QK_SLOT), BF),
            jax.ShapeDtypeStruct((MLA_HEADS, t, QK_SLOT), BF),
            jax.ShapeDtypeStruct((MLA_HEADS, t, V_SLOT), BF),
        ],
        compiler_params=_params("arbitrary"),
        name="mla_prep",
    )(x2, g, wlat, qg, kvg, wq, wkv, cs)


def _dil_proj_kernel(x_ref, g_ref, w_ref, *rest):
    o_refs, h_ref, res_ref = rest[:DIL_GROUPS], rest[DIL_GROUPS], rest[DIL_GROUPS + 1]
    nqkv = 3 * DIL_HEADS_PER_GROUP
    h_ref[...] = _rms(x_ref[...], g_ref[...]).astype(BF)
    for gi, (_, dil) in enumerate(DIL_PATTERNS):
        sub = TM // dil
        res = _dot(h_ref[...], w_ref[:, gi * DIL_GROUP_DIM:(gi + 1) * DIL_GROUP_DIM])
        for hh in range(nqkv):
            blk = res[:, hh * DIL_HEAD_DIM:(hh + 1) * DIL_HEAD_DIM]
            if hh < DIL_HEADS_PER_GROUP:
                blk = blk * DIL_HEAD_DIM ** -0.5
            if dil == 1:
                o_refs[gi][hh, 0, 0] = blk.astype(BF)
            else:
                res_ref[hh] = blk
        if dil > 1:
            for hh in range(nqkv):
                for r in range(dil):
                    o_refs[gi][hh, 0, r] = res_ref[hh, pl.ds(r, sub, stride=dil), :].astype(BF)


def _dil_proj(x2, g, w, batch, seq):
    nseq = seq // TM
    nqkv = 3 * DIL_HEADS_PER_GROUP
    const = lambda i: (0, 0)
    return pl.pallas_call(
        _dil_proj_kernel,
        grid=(batch * nseq,),
        in_specs=[
            pl.BlockSpec((TM, D_MODEL), lambda i: (i, 0)),
            pl.BlockSpec((1, D_MODEL), const),
            pl.BlockSpec((D_MODEL, DIL_GROUPS * DIL_GROUP_DIM), const, pipeline_mode=pl.Buffered(1)),
        ],
        out_specs=[
            pl.BlockSpec((nqkv, 1, dil, TM // dil, DIL_HEAD_DIM), lambda i: (0, i // nseq, 0, i % nseq, 0))
            for _, dil in DIL_PATTERNS
        ],
        out_shape=[
            jax.ShapeDtypeStruct((nqkv, batch, dil, seq // dil, DIL_HEAD_DIM), BF)
            for _, dil in DIL_PATTERNS
        ],
        scratch_shapes=[pltpu.VMEM((TM, D_MODEL), BF), pltpu.VMEM((nqkv, TM, DIL_HEAD_DIM), F32)],
        compiler_params=_params("arbitrary"),
        name="dil_proj",
    )(x2, g, w)


def _flash_kernel(q_ref, k_ref, v_ref, o_ref, *scratch):
    nh = FLASH_HEADS
    m_refs, acc_refs = scratch[:nh], scratch[nh:2 * nh]
    s_refs = (scratch[2 * nh:3 * nh], scratch[3 * nh:4 * nh])
    qi = pl.program_id(2)
    for hh in range(nh):
        m_refs[hh][...] = jnp.full(m_refs[hh].shape, NEG, F32)
        acc_refs[hh][...] = jnp.zeros(acc_refs[hh].shape, F32)

    def key_rows(j):
        return pl.ds(pl.multiple_of(j * TK, TK), TK)

    def scores(j, slot):
        for hh in range(nh):
            s_refs[slot][hh][...] = _dot_nt(q_ref[hh], k_ref[hh, key_rows(j), :])

    def mask_diagonal(slot):
        keep = lax.broadcasted_iota(jnp.int32, (TQ, TK), 1) <= lax.broadcasted_iota(jnp.int32, (TQ, TK), 0)
        for hh in range(nh):
            s_refs[slot][hh][...] = jnp.where(keep, s_refs[slot][hh][...], NEG)

    def update(j, slot):
        for hh in range(nh):
            s = s_refs[slot][hh][...]
            m_old = m_refs[hh][...]
            m_new = jnp.maximum(m_old, jnp.max(s, axis=-1, keepdims=True))
            p = jnp.exp2(s - m_new)
            pv = _dot(p.astype(BF), v_ref[hh, key_rows(j), :])
            acc_refs[hh][...] = jnp.exp2(m_old - m_new) * acc_refs[hh][...] + pv
            m_refs[hh][...] = m_new

    scores(0, 0)

    @pl.when(qi == 0)
    def _():
        mask_diagonal(0)

    def body(j, carry):
        for slot in range(2):
            @pl.when(j % 2 == slot)
            def _(slot=slot):
                scores(j, slot)
                update(j - 1, 1 - slot)

            @pl.when((j % 2 == slot) & (j == qi))
            def _(slot=slot):
                mask_diagonal(slot)
        return carry

    lax.fori_loop(1, qi + 1, body, 0)

    for slot in range(2):
        @pl.when(qi % 2 == slot)
        def _(slot=slot):
            update(qi, slot)

    for hh in range(FLASH_HEADS):
        acc = acc_refs[hh][...]
        o_ref[hh] = (acc[:, :V_HEAD_DIM] * (1.0 / acc[:, V_HEAD_DIM:])).astype(BF)


def _mla_flash(q, k, v, batch, seq):
    nq = seq // TQ
    t = batch * seq
    hb = FLASH_HEADS
    return pl.pallas_call(
        _flash_kernel,
        grid=(batch, MLA_HEADS // hb, nq),
        in_specs=[
            pl.BlockSpec((hb, TQ, QK_SLOT), lambda b, h, i: (h, b * nq + i, 0)),
            pl.BlockSpec((hb, seq, QK_SLOT), lambda b, h, i: (h, b, 0)),
            pl.BlockSpec((hb, seq, V_SLOT), lambda b, h, i: (h, b, 0)),
        ],
        out_specs=pl.BlockSpec((hb, TQ, V_HEAD_DIM), lambda b, h, i: (h, b * nq + i, 0)),
        out_shape=jax.ShapeDtypeStruct((MLA_HEADS, t, V_HEAD_DIM), BF),
        scratch_shapes=([pltpu.VMEM((TQ, 1), F32)] * hb + [pltpu.VMEM((TQ, V_SLOT), F32)] * hb
                        + [pltpu.VMEM((TQ, TK), F32)] * (2 * hb)),
        compiler_params=_params("arbitrary", "arbitrary", "arbitrary"),
        name="mla_flash",
    )(q, k, v)


def _dilated_kernel(slope_ref, q_ref, k_ref, v_ref, o_ref, lse_ref, *, group, dil, nb):
    hs = pl.program_id(1)
    slope = slope_ref[group * DIL_HEADS_PER_GROUP + hs] * float(dil)
    qi = lax.broadcasted_iota(jnp.int32, (DIL_BLOCK, DIL_BLOCK), 0)
    ki = lax.broadcasted_iota(jnp.int32, (DIL_BLOCK, DIL_BLOCK), 1)
    dist_cur = (qi - ki).astype(F32)
    dist_prev = dist_cur + float(DIL_BLOCK)
    w_sub = float(DIL_PATTERNS[group][0] // dil)
    bias_cur = jnp.where((dist_cur >= 0.0) & (dist_cur <= w_sub), -slope * dist_cur, NEG)
    bias_prev = jnp.where(dist_prev <= w_sub, -slope * dist_prev, NEG)
    neg = jnp.full((DIL_BLOCK, DIL_BLOCK), NEG, F32)
    bias_inner = jnp.concatenate([bias_prev, bias_cur], axis=1)
    bias_first = jnp.concatenate([bias_cur, neg], axis=1)
    eye = qi == ki

    def window(n):
        start = max(n - 1, 0) * DIL_BLOCK
        return slice(start, start + 2 * DIL_BLOCK)

    blocks = [(r, n) for r in range(dil) for n in range(nb)]
    for g0 in range(0, len(blocks), DIL_BATCH):
        batch = blocks[g0:g0 + DIL_BATCH]
        scores = [_dot_nt(q_ref[0, 0, r, n * DIL_BLOCK:(n + 1) * DIL_BLOCK, :], k_ref[0, 0, r, window(n), :])
                  + (bias_first if n == 0 else bias_inner) for r, n in batch]
        maxes = [jnp.max(s, axis=-1, keepdims=True) for s in scores]
        probs = [jnp.exp(s - m) for s, m in zip(scores, maxes)]
        sums = [jnp.sum(p, axis=-1, keepdims=True) for p in probs]
        outs = [_dot(p.astype(BF), v_ref[0, 0, r, window(n), :]) for p, (r, n) in zip(probs, batch)]
        for (r, n), o, l in zip(batch, outs, sums):
            o_ref[0, 0, r, n * DIL_BLOCK:(n + 1) * DIL_BLOCK, :] = (o * (1.0 / l)).astype(BF)
        rows = [jnp.sum(jnp.where(eye, m + jnp.log(l), 0.0), axis=0, keepdims=True) for m, l in zip(maxes, sums)]
        lse_ref[0, 0, g0:g0 + DIL_BATCH, :] = jnp.concatenate(rows, axis=0)


def _dilated(qkv, slopes, group, batch, seq):
    _, dil = DIL_PATTERNS[group]
    sub = seq // dil
    nb = sub // DIL_BLOCK
    hg = DIL_HEADS_PER_GROUP
    blk = (1, 1, dil, sub, DIL_HEAD_DIM)

    def head(which):
        return lambda b, h: (which * hg + h, b, 0, 0, 0)

    return pl.pallas_call(
        functools.partial(_dilated_kernel, group=group, dil=dil, nb=nb),
        grid=(batch, hg),
        in_specs=[
            pl.BlockSpec(memory_space=pltpu.SMEM),
            pl.BlockSpec(blk, head(0)),
            pl.BlockSpec(blk, head(1)),
            pl.BlockSpec(blk, head(2)),
        ],
        out_specs=[
            pl.BlockSpec(blk, head(0)),
            pl.BlockSpec((1, 1, dil * nb, DIL_BLOCK), lambda b, h: (h, b, 0, 0)),
        ],
        out_shape=[
            jax.ShapeDtypeStruct((hg, batch, dil, sub, DIL_HEAD_DIM), BF),
            jax.ShapeDtypeStruct((hg, batch, dil * nb, DIL_BLOCK), F32),
        ],
        compiler_params=_params("arbitrary", "arbitrary"),
        name=f"dilated_d{dil}",
    )(slopes, qkv, qkv, qkv)


MERGE_TN = 512
MERGE_TM = 256


def _merge_kernel(x_ref, g_ref, wg_ref, bg_ref, oa_ref, o1_ref, o4_ref, o16_ref, lse_ref,
                  woa_ref, wob_ref, wout_ref, fg_ref, y_ref, h2_ref, h_ref, a_ref, b_ref, mg_ref, nat_ref):
    x = x_ref[...]
    tm = x.shape[0]
    h_ref[...] = _rms(x, g_ref[...]).astype(BF)
    for hd in range(MLA_HEADS):
        a_ref[:, hd * V_HEAD_DIM:(hd + 1) * V_HEAD_DIM] = oa_ref[hd]
    lse = lse_ref[...]
    o_refs = (o1_ref, o4_ref, o16_ref)
    for hs in range(DIL_HEADS_PER_GROUP):
        ls = [lse[:, gi * DIL_HEADS_PER_GROUP + hs:gi * DIL_HEADS_PER_GROUP + hs + 1] for gi in range(DIL_GROUPS)]
        mx = jnp.maximum(jnp.maximum(ls[0], ls[1]), ls[2])
        es = [jnp.exp(v - mx) for v in ls]
        inv = 1.0 / (es[0] + es[1] + es[2])
        comb = (es[0] * inv) * o_refs[0][hs, 0, 0].astype(F32)
        for gi in range(1, DIL_GROUPS):
            dil = DIL_PATTERNS[gi][1]
            for r in range(dil):
                nat_ref[pl.ds(r, tm // dil, stride=dil), :] = o_refs[gi][hs, 0, r].astype(F32)
            comb = comb + (es[gi] * inv) * nat_ref[...]
        b_ref[:, hs * DIL_HEAD_DIM:(hs + 1) * DIL_HEAD_DIM] = comb.astype(BF)
    for c in range(D_MODEL // MERGE_TN):
        cols = slice(c * MERGE_TN, (c + 1) * MERGE_TN)
        cols_b = slice(D_MODEL + c * MERGE_TN, D_MODEL + (c + 1) * MERGE_TN)
        gate_a = jax.nn.sigmoid(_dot(h_ref[...], wg_ref[:, cols]) + bg_ref[:, cols])
        gate_b = jax.nn.sigmoid(_dot(h_ref[...], wg_ref[:, cols_b]) + bg_ref[:, cols_b])
        o_a = _dot(a_ref[...], woa_ref[:, cols])
        o_b = _dot(b_ref[...], wob_ref[:, cols])
        mg_ref[:, cols] = (gate_a * o_a + gate_b * o_b).astype(BF)
    y = x + _dot(mg_ref[...], wout_ref[...])
    y_ref[...] = y
    h2_ref[...] = _rms(y, fg_ref[...]).astype(BF)


def _merge(x2, g, wg, bg, oa, o_dil, lse, woa, wob, wout, fg, seq):
    t = x2.shape[0]
    tm = MERGE_TM
    nseq = seq // tm
    const = lambda i: (0, 0)
    one = pl.Buffered(1)
    dil_specs = [
        pl.BlockSpec((DIL_HEADS_PER_GROUP, 1, dil, tm // dil, DIL_HEAD_DIM),
                     lambda i: (0, i // nseq, 0, i % nseq, 0))
        for _, dil in DIL_PATTERNS
    ]
    return pl.pallas_call(
        _merge_kernel,
        grid=(t // tm,),
        in_specs=[
            pl.BlockSpec((tm, D_MODEL), lambda i: (i, 0)),
            pl.BlockSpec((1, D_MODEL), const),
            pl.BlockSpec((D_MODEL, 2 * D_MODEL), const, pipeline_mode=one),
            pl.BlockSpec((1, 2 * D_MODEL), const),
            pl.BlockSpec((MLA_HEADS, tm, V_HEAD_DIM), lambda i: (0, i, 0)),
            *dil_specs,
            pl.BlockSpec((tm, DIL_HEADS), lambda i: (i, 0)),
            pl.BlockSpec((MLA_HEADS * V_HEAD_DIM, D_MODEL), const, pipeline_mode=one),
            pl.BlockSpec((DIL_HEADS_PER_GROUP * DIL_HEAD_DIM, D_MODEL), const, pipeline_mode=one),
            pl.BlockSpec((D_MODEL, D_MODEL), const, pipeline_mode=one),
            pl.BlockSpec((1, D_MODEL), const),
        ],
        out_specs=[pl.BlockSpec((tm, D_MODEL), lambda i: (i, 0)), pl.BlockSpec((tm, D_MODEL), lambda i: (i, 0))],
        out_shape=[jax.ShapeDtypeStruct((t, D_MODEL), F32), jax.ShapeDtypeStruct((t, D_MODEL), BF)],
        scratch_shapes=[
            pltpu.VMEM((tm, D_MODEL), BF),
            pltpu.VMEM((tm, MLA_HEADS * V_HEAD_DIM), BF),
            pltpu.VMEM((tm, DIL_HEADS_PER_GROUP * DIL_HEAD_DIM), BF),
            pltpu.VMEM((tm, D_MODEL), BF),
            pltpu.VMEM((tm, DIL_HEAD_DIM), F32),
        ],
        compiler_params=_params("arbitrary"),
        name="merge",
    )(x2, g, wg, bg, oa, *o_dil, lse, woa, wob, wout, fg)


def _ffn_up_kernel(h_ref, w_ref, cw_ref, a_ref, u0_ref, u1_ref, *, steps, tiles, tiles_per_seq):
    s = pl.program_id(0)
    i = s % tiles
    tm = h_ref.shape[0]
    u_refs = (u0_ref, u1_ref)

    def activate(slot, r1):
        u_ref = u_refs[slot]
        groups = FFN_ROWS // SUBLANE

        def conv(cols):
            window = u_ref[r1:r1 + FFN_ROWS + SUBLANE, cols]
            out = None
            for tap in range(CONV_WIDTH):
                back = CONV_WIDTH - 1 - tap
                shifted = window if back == 0 else pltpu.roll(window, back, 0)
                x = shifted[SUBLANE:].reshape(groups, SUBLANE, FF_CHUNK)
                term = cw_ref[tap * SUBLANE:(tap + 1) * SUBLANE, cols][None] * x
                out = term if out is None else out + term
            out = out + cw_ref[CONV_WIDTH * SUBLANE:(CONV_WIDTH + 1) * SUBLANE, cols][None]
            return out.reshape(FFN_ROWS, FF_CHUNK)

        up = conv(slice(0, FF_CHUNK))
        gate = conv(slice(FF_CHUNK, 2 * FF_CHUNK))
        a_ref[r1:r1 + FFN_ROWS, :] = (jax.nn.silu(gate) * up).astype(BF)

    def run(project_slot=None, activate_slot=None, first_tile=False):
        if first_tile:
            u_refs[project_slot][:SUBLANE] = jnp.zeros((SUBLANE, 2 * FF_CHUNK), F32)
        elif project_slot is not None:
            seq_start = (i % tiles_per_seq) == 0
            halo = u_refs[1 - project_slot][tm:]
            u_refs[project_slot][:SUBLANE] = jnp.where(seq_start, 0.0, halo)
        kslices = D_MODEL // FFN_KSLICE
        per_slice = FFN_PIECE // FFN_ROWS // kslices
        for r0 in range(0, tm, FFN_PIECE):
            acc = None
            for ks in range(kslices):
                if project_slot is not None:
                    kk = slice(ks * FFN_KSLICE, (ks + 1) * FFN_KSLICE)
                    part = _dot(h_ref[r0:r0 + FFN_PIECE, kk], w_ref[kk, :])
                    acc = part if acc is None else acc + part
                if activate_slot is not None:
                    for k in range(per_slice):
                        activate(activate_slot, r0 + (ks * per_slice + k) * FFN_ROWS)
            if project_slot is not None:
                u_refs[project_slot][SUBLANE + r0:SUBLANE + r0 + FFN_PIECE, :] = acc

    @pl.when(s == 0)
    def _():
        run(project_slot=0, first_tile=True)

    for slot in range(2):
        @pl.when((s > 0) & (s < steps) & (s % 2 == slot))
        def _(slot=slot):
            run(project_slot=slot, activate_slot=1 - slot)

    @pl.when(s == steps)
    def _():
        run(activate_slot=(steps - 1) % 2)


def _ffn_down_kernel(a_ref, wd_ref, x_ref, fg_ref, y_ref):
    y_ref[...] = _rms(x_ref[...] + _dot(a_ref[...], wd_ref[...]), fg_ref[...])


def _ffn_up(h2, wu, cw, seq):
    t = h2.shape[0]
    tm = FFN_TM
    tiles = t // tm
    steps = FF_CHUNKS * tiles
    projected = lambda s: jnp.minimum(s, steps - 1)
    activated = lambda s: jnp.maximum(s - 1, 0)
    return pl.pallas_call(
        functools.partial(_ffn_up_kernel, steps=steps, tiles=tiles, tiles_per_seq=seq // tm),
        grid=(steps + 1,),
        in_specs=[
            pl.BlockSpec((tm, D_MODEL), lambda s: (projected(s) % tiles, 0)),
            pl.BlockSpec((D_MODEL, 2 * FF_CHUNK), lambda s: (0, projected(s) // tiles)),
            pl.BlockSpec(((CONV_WIDTH + 1) * SUBLANE, 2 * FF_CHUNK), lambda s: (0, activated(s) // tiles)),
        ],
        out_specs=pl.BlockSpec((tm, FF_CHUNK), lambda s: (activated(s) % tiles, activated(s) // tiles)),
        out_shape=jax.ShapeDtypeStruct((t, D_FF_PAD), BF),
        scratch_shapes=[
            pltpu.VMEM((tm + SUBLANE, 2 * FF_CHUNK), F32),
            pltpu.VMEM((tm + SUBLANE, 2 * FF_CHUNK), F32),
        ],
        compiler_params=_params("arbitrary"),
        name="ffn_up",
    )(h2, wu, cw)


def _ffn_down(act, wd, x1, fg):
    t = x1.shape[0]
    tm = FFN_DOWN_TM
    return pl.pallas_call(
        _ffn_down_kernel,
        grid=(t // tm,),
        in_specs=[
            pl.BlockSpec((tm, D_FF_PAD), lambda i: (i, 0)),
            pl.BlockSpec((D_FF_PAD, D_MODEL), lambda i: (0, 0), pipeline_mode=pl.Buffered(1)),
            pl.BlockSpec((tm, D_MODEL), lambda i: (i, 0)),
            pl.BlockSpec((1, D_MODEL), lambda i: (0, 0)),
        ],
        out_specs=pl.BlockSpec((tm, D_MODEL), lambda i: (i, 0)),
        out_shape=jax.ShapeDtypeStruct((t, D_MODEL), F32),
        compiler_params=_params("arbitrary"),
        name="ffn_down",
    )(act, wd, x1, fg)


def _rotate_half_cols(w):
    half = w.shape[-1] // 2
    return jnp.concatenate([-w[..., half:], w[..., :half]], axis=-1)


PREP_PIECE = 512
PREP_ROWS = 256


def _relayout_kernel(src_ref, *out_refs, plans):
    ncols = src_ref.shape[1]
    for out_ref, plan in zip(out_refs, plans):
        col = 0
        for start, width in plan:
            for off in range(0, width, PREP_PIECE):
                w = min(PREP_PIECE, width - off)
                dst = slice(col + off, col + off + w)
                if start is None:
                    out_ref[:, dst] = jnp.zeros((out_ref.shape[0], w), out_ref.dtype)
                    continue
                lead = (start + off) % LANE
                lo = start + off - lead
                hi = min(-(-(start + off + w) // LANE) * LANE, ncols)
                out_ref[:, dst] = src_ref[:, lo:hi][:, lead:lead + w].astype(out_ref.dtype)
            col += width


def _relayout(src, plans, dtype, row_tile, name):
    rows, ncols = src.shape
    widths = [sum(w for _, w in plan) for plan in plans]
    assert rows % row_tile == 0 and all(w % LANE == 0 for plan in plans for _, w in plan)
    outs = pl.pallas_call(
        functools.partial(_relayout_kernel, plans=plans),
        grid=(rows // row_tile,),
        in_specs=[pl.BlockSpec((row_tile, ncols), lambda i: (i, 0))],
        out_specs=[pl.BlockSpec((row_tile, w), lambda i: (i, 0)) for w in widths],
        out_shape=[jax.ShapeDtypeStruct((rows, w), dtype) for w in widths],
        compiler_params=_params("arbitrary"),
        name=name,
    )(src)
    return outs


def _transpose_rows_kernel(src_hbm, out_ref, buf_ref, sem, *, row_start, chunks):
    j = pl.program_id(0)

    def fetch(jj, slot):
        rows = pl.ds(pl.multiple_of(row_start(jj), SUBLANE), PREP_PIECE)
        return pltpu.make_async_copy(src_hbm.at[rows, :], buf_ref.at[slot], sem.at[slot])

    @pl.when(j == 0)
    def _():
        fetch(j, 0).start()

    @pl.when(j + 1 < chunks)
    def _():
        fetch(j + 1, (j + 1) % 2).start()

    fetch(j, j % 2).wait()
    out_ref[...] = buf_ref[j % 2].T.astype(out_ref.dtype)


def _transpose_rows(src, row_start, chunks, name):
    ncols = src.shape[1]
    return pl.pallas_call(
        functools.partial(_transpose_rows_kernel, row_start=row_start, chunks=chunks),
        grid=(chunks,),
        in_specs=[pl.BlockSpec(memory_space=pl.ANY)],
        out_specs=pl.BlockSpec((ncols, PREP_PIECE), lambda j: (0, j)),
        out_shape=jax.ShapeDtypeStruct((ncols, chunks * PREP_PIECE), BF),
        scratch_shapes=[pltpu.VMEM((2, PREP_PIECE, ncols), F32), pltpu.SemaphoreType.DMA((2,))],
        compiler_params=_params("arbitrary"),
        name=name,
    )(src)


def _ff_chunk_plan():
    plan = []
    for c in range(FF_CHUNKS):
        lo, hi = c * FF_CHUNK, min((c + 1) * FF_CHUNK, D_FF)
        for half in (0, D_FF):
            plan.append((half + lo, hi - lo))
            if hi - lo < FF_CHUNK:
                plan.append((None, FF_CHUNK - (hi - lo)))
    return plan


def kernel(x, attn_norm_g, w_in, b_gate, q_norm_g, w_uq, kv_norm_g, w_ukv, w_o_mla, w_o_dil,
           w_out, ffn_norm_g, w_up, conv_w, conv_b, w_down, final_norm_g):
    batch, seq, _ = x.shape
    assert w_in.shape[0] == 1, "single-layer block"
    assert seq % TM == 0 and seq % FFN_TM == 0 and seq % TQ == 0 and TQ == TK and seq % (16 * DIL_BLOCK) == 0
    t = batch * seq
    x2 = x.reshape(t, D_MODEL)

    dqkv = DIL_HEADS * DIL_HEAD_DIM
    o = np.cumsum((0, Q_LORA_RANK, KV_LORA_RANK, QK_ROPE_DIM, dqkv, dqkv, dqkv, D_MODEL, D_MODEL))
    wt = w_in[0].T
    w_head = _transpose_rows(wt, lambda j: j * PREP_PIECE, -(-int(o[3]) // PREP_PIECE), "relayout_w_lat")
    wlat = jnp.concatenate([w_head[:, o[0]:o[3]], _rotate_half_cols(w_head[:, o[2]:o[3]])], axis=1)
    gw = DIL_HEADS_PER_GROUP * DIL_HEAD_DIM
    assert gw == PREP_PIECE and D_MODEL % PREP_PIECE == 0
    dil_start = lambda j: int(o[3]) + (j % 3) * (DIL_HEADS * DIL_HEAD_DIM) + (j // 3) * gw
    wdil = _transpose_rows(wt, dil_start, 3 * DIL_GROUPS, "relayout_w_dil")
    wgate = _transpose_rows(wt, lambda j: int(o[6]) + j * PREP_PIECE, 2 * D_MODEL // PREP_PIECE, "relayout_w_gate")
    wq = w_uq[0].astype(BF).reshape(Q_LORA_RANK, MLA_HEADS, QK_NOPE_DIM + QK_ROPE_DIM)
    wq_pe = wq[..., QK_NOPE_DIM:]
    wq = jnp.concatenate([wq[..., :QK_NOPE_DIM], wq_pe, _rotate_half_cols(wq_pe)], axis=-1)
    wq = wq.reshape(Q_LORA_RANK, MLA_HEADS * QK_SLOT)
    wkv = w_ukv[0].astype(BF)
    (wu,) = _relayout(w_up[0], [_ff_chunk_plan()], BF, PREP_ROWS, "relayout_w_up")
    cw = jnp.repeat(jnp.concatenate([conv_w[0], conv_b], axis=0), SUBLANE, axis=0)
    (cw,) = _relayout(cw, [_ff_chunk_plan()], F32, cw.shape[0], "relayout_conv")
    wd =jnp.concatenate([w_down[0].astype(BF), jnp.zeros((D_FF_PAD - D_FF, D_MODEL), BF)], axis=0)

    pos = jnp.arange(seq, dtype=F32)
    inv_freq = ROPE_THETA ** (-jnp.arange(0, QK_ROPE_DIM, 2, dtype=F32) / QK_ROPE_DIM)
    ang = pos[:, None] * inv_freq[None, :]
    cs = jnp.concatenate([jnp.cos(ang), jnp.cos(ang), jnp.sin(ang), jnp.sin(ang)], axis=1)
    slopes = 2.0 ** (-ALIBI_MAX_BIAS * jnp.arange(1, DIL_HEADS + 1, dtype=F32) / DIL_HEADS)

    q, k, v = _mla_prep(x2, attn_norm_g, wlat, q_norm_g, kv_norm_g, wq, wkv, cs, seq)
    o_mla = _mla_flash(q, k, v, batch, seq)
    qkv_dil = _dil_proj(x2, attn_norm_g, wdil, batch, seq)
    o_dil, lse_dil = [], []
    for gi, (_, dil) in enumerate(DIL_PATTERNS):
        o_g, lse_g = _dilated(qkv_dil[gi], slopes, gi, batch, seq)
        nb = seq // dil // DIL_BLOCK
        o_dil.append(o_g)
        lse_g = lse_g.reshape(DIL_HEADS_PER_GROUP, batch, dil, nb, DIL_BLOCK)
        lse_dil.append(lse_g.transpose(1, 3, 4, 2, 0).reshape(t, DIL_HEADS_PER_GROUP))
    lse = jnp.concatenate(lse_dil, axis=1)
    x1, h2 = _merge(x2, attn_norm_g, wgate, b_gate, o_mla, o_dil, lse,
                    w_o_mla[0].astype(BF), w_o_dil[0].astype(BF), w_out[0].astype(BF), ffn_norm_g, seq)

    act = _ffn_up(h2, wu, cw, seq)
    y = _ffn_down(act, wd, x1, final_norm_g.reshape(1, D_MODEL))
    return y.reshape(batch, seq, D_MODEL)
```
